```python
import math
import jax, jax.numpy as jnp
from jax import lax
import numpy as np

D_MODEL = 1024
BATCH = 4
SEQ = 4096
DEPTH = 4
DEC_BATCH = 32
DEC_SEQ = 4
PAST_LEN = 8192
PAGE_SIZE = 128

H_A = 4
D_HEAD_A = 64
W_A = H_A * 2 * D_HEAD_A
H_B = 4
D_HEAD_B = 64
W_B = H_B * D_HEAD_B
H_C = 4
D_HEAD_C = 64
W_C = H_C * D_HEAD_C
DECAY_LORA = 64
ICLR_LORA = 64
GATE_LORA = 128
RWKV_GN_EPS = 64e-5
MIX_WIDTH = W_A + W_B + W_C
PROJ_A = 3 * W_A
PROJ_B = 4 * W_B
PROJ_C = 3 * W_C + DECAY_LORA + ICLR_LORA + GATE_LORA
PROJ_WIDTH = PROJ_A + PROJ_B + PROJ_C
N_BUCKETS = 32
MAX_DISTANCE = 128
ATTN_Q_BLOCK = 128
RET_CHUNK = 128
D_FF = 2816
N_EXPERTS = 8
TOP_K = 2
D_EXPERT = 1024
ALPHA = (2 * DEPTH) ** 0.25
BETA = (8 * DEPTH) ** -0.25
LN_EPS = 1e-5

kernel_name = 'hybrid_diffattn_retnet_rwkv7_step'


def layer_norm(x, g, b):
    xf = x.astype(jnp.float32)
    mu = jnp.mean(xf, axis=-1, keepdims=True)
    var = jnp.mean(jnp.square(xf - mu), axis=-1, keepdims=True)
    return ((xf - mu) * lax.rsqrt(var + LN_EPS) * g + b).astype(x.dtype)


def rms_norm(x, g=None, eps=1e-5):
    xf = x.astype(jnp.float32)
    y = xf * lax.rsqrt(jnp.mean(xf * xf, axis=-1, keepdims=True) + eps)
    return y if g is None else y * g


def rel_buckets(dist):
    n = jnp.maximum(dist, 0)
    max_exact = N_BUCKETS // 2
    large = max_exact + (jnp.log(jnp.maximum(n, 1).astype(jnp.float32) / max_exact)
                         / math.log(MAX_DISTANCE / max_exact) * (N_BUCKETS - max_exact)).astype(jnp.int32)
    large = jnp.minimum(large, N_BUCKETS - 1)
    return jnp.where(n < max_exact, n, large)


def diff_logits(q, k, rel_bias, dist):
    s = jnp.einsum('bqhmd,bkhmd->bhmqk', q, k, preferred_element_type=jnp.float32) * (D_HEAD_A ** -0.5)
    bias = jnp.moveaxis(rel_bias[rel_buckets(dist)], -1, 0).astype(jnp.float32)
    s = s + bias[:, None]
    return jnp.where(dist >= 0, s, -jnp.inf)


def diff_attn_prompt(q, k, v, lam, rel_bias):
    B, T = q.shape[:2]
    n_blk = T // ATTN_Q_BLOCK
    k_pos = jnp.arange(T, dtype=jnp.int32)

    def one_block(i):
        start = i * ATTN_Q_BLOCK
        q_blk = lax.dynamic_slice_in_dim(q, start, ATTN_Q_BLOCK, axis=1)
        q_pos = start + jnp.arange(ATTN_Q_BLOCK, dtype=jnp.int32)
        p = jax.nn.softmax(diff_logits(q_blk, k, rel_bias, q_pos[:, None] - k_pos[None, :]), axis=-1)
        a = (p[:, :, 0] - lam * p[:, :, 1]).astype(v.dtype)
        return jnp.einsum('bhqk,bkhe->bqhe', a, v)

    out = lax.map(one_block, jnp.arange(n_blk, dtype=jnp.int32))
    return jnp.moveaxis(out, 0, 1).reshape(B, T, H_A, 2 * D_HEAD_A)


def diff_attn_sample(q, k, v, k_past, v_past, lam, rel_bias):
    T = q.shape[1]
    P = k_past.shape[1]
    q_pos = P + jnp.arange(T, dtype=jnp.int32)
    lp = diff_logits(q, k_past, rel_bias, q_pos[:, None] - jnp.arange(P, dtype=jnp.int32)[None, :])
    ln = diff_logits(q, k, rel_bias, q_pos[:, None] - q_pos[None, :])
    p = jax.nn.softmax(jnp.concatenate([lp, ln], axis=-1), axis=-1)
    a = (p[:, :, 0] - lam * p[:, :, 1]).astype(v.dtype)
    return (jnp.einsum('bhqk,bkhe->bqhe', a[..., :P], v_past)
            + jnp.einsum('bhqk,bkhe->bqhe', a[..., P:], v))


def rotary(x, pos):
    half = x.shape[-1] // 2
    inv = 1.0 / (10000.0 ** (jnp.arange(half, dtype=jnp.float32) / half))
    ang = pos.astype(jnp.float32)[:, None] * inv[None, :]
    cos = jnp.cos(ang)[None, :, None, :]
    sin = jnp.sin(ang)[None, :, None, :]
    x1, x2 = x[..., :half], x[..., half:]
    return jnp.concatenate([x1 * cos - x2 * sin, x1 * sin + x2 * cos], axis=-1)


def retention_chunkwise(q, k, v, S0, chunk):
    B, T, H, d = q.shape
    nc = T // chunk
    log_g = jnp.log1p(-jnp.exp2(-5.0 - jnp.arange(H, dtype=jnp.float32)))
    i = jnp.arange(chunk, dtype=jnp.float32)
    diff = i[:, None] - i[None, :]
    dmask = jnp.where(diff >= 0, jnp.exp(log_g[:, None, None] * jnp.maximum(diff, 0.0)), 0.0)
    xi = jnp.exp(log_g[None, :] * (i[:, None] + 1.0))[None, :, :, None]
    zeta = jnp.exp(log_g[None, :] * (chunk - 1.0 - i[:, None]))[None, :, :, None]
    g_chunk = jnp.exp(log_g * chunk)[None, :, None, None]

    def to_chunks(t):
        return jnp.moveaxis(t.reshape(B, nc, chunk, H, d), 1, 0)

    def step(S, qkv):
        qc, kc, vc = qkv
        s = jnp.einsum('bihd,bjhd->bhij', qc, kc) * dmask[None]
        inner = jnp.einsum('bhij,bjhe->bihe', s, vc)
        cross = jnp.einsum('bihd,bhde->bihe', qc, S) * xi
        S_new = S * g_chunk + jnp.einsum('bjhd,bjhe->bhde', kc * zeta, vc)
        return S_new, inner + cross

    S_fin, o = lax.scan(step, S0, (to_chunks(q), to_chunks(k), to_chunks(v)))
    return jnp.moveaxis(o, 0, 1).reshape(B, T, H, d), S_fin


def retention_mixer(p_b, pos0, S0):
    B, T, _ = p_b.shape
    q, k, v, g = [t.reshape(B, T, H_B, D_HEAD_B).astype(jnp.float32) for t in jnp.split(p_b, 4, axis=-1)]
    pos = pos0 + jnp.arange(T, dtype=jnp.int32)
    q = rotary(q, pos)
    k = rotary(k, pos) * (D_HEAD_B ** -0.5)
    o, S = retention_chunkwise(q, k, v, S0.astype(jnp.float32), min(RET_CHUNK, T))
    return (jax.nn.silu(g) * rms_norm(o)).reshape(B, T, W_B), S


def rwkv7_scan(r, w, k, v, kk, a, S0):
    def step(S, inp):
        rt, wt, kt, vt, kkt, at = inp
        sa = jnp.einsum('bhvk,bhk->bhv', S, -kkt)
        S = (S * wt[:, :, None, :] + sa[..., None] * (kkt * at)[:, :, None, :]
             + vt[..., None] * kt[:, :, None, :])
        return S, jnp.einsum('bhvk,bhk->bhv', S, rt)

    xs = tuple(jnp.moveaxis(t, 1, 0) for t in (r, w, k, v, kk, a))
    S, o = lax.scan(step, S0, xs)
    return jnp.moveaxis(o, 0, 1), S


def rwkv7_mixer(p_c, shift0, S0, mu, w0, w2, a0, a2, g2, k_k, k_a, r_k, lnx_g, lnx_b):
    B, T, _ = p_c.shape
    pc = p_c.astype(jnp.float32)
    prev = jnp.concatenate([shift0.astype(jnp.float32)[:, None], pc[:, :-1]], axis=1)
    xm = pc + (prev - pc) * mu
    r, k, v = xm[..., :W_C], xm[..., W_C:2 * W_C], xm[..., 2 * W_C:3 * W_C]
    o0 = 3 * W_C
    xw = xm[..., o0:o0 + DECAY_LORA]
    xa = xm[..., o0 + DECAY_LORA:o0 + DECAY_LORA + ICLR_LORA]
    xg = xm[..., o0 + DECAY_LORA + ICLR_LORA:]
    w_log = -jax.nn.softplus(-(w0 + jnp.tanh(xw) @ w2)) - 0.5
    decay = jnp.exp(-jnp.exp(w_log))
    a = jax.nn.sigmoid(a0 + xa @ a2)
    g = jax.nn.sigmoid(xg) @ g2

    def heads(t):
        return t.reshape(B, T, H_C, D_HEAD_C)

    kk = heads(k * k_k)
    kk = kk / jnp.maximum(jnp.sqrt(jnp.sum(kk * kk, axis=-1, keepdims=True)), 1e-12)
    k = k * (1.0 + (a - 1.0) * k_a)
    r_h, k_h, v_h = heads(r), heads(k), heads(v)
    o_h, S = rwkv7_scan(r_h, heads(decay), k_h, v_h, kk, heads(a), S0.astype(jnp.float32))
    mean = jnp.mean(o_h, axis=-1, keepdims=True)
    var = jnp.mean(jnp.square(o_h - mean), axis=-1, keepdims=True)
    o_n = ((o_h - mean) * lax.rsqrt(var + RWKV_GN_EPS)).reshape(B, T, W_C) * lnx_g + lnx_b
    bonus = (jnp.sum(r_h * k_h * r_k, axis=-1, keepdims=True) * v_h).reshape(B, T, W_C)
    return (o_n + bonus) * g, S, p_c[:, -1]


def swiglu(x, w1, w3, w2):
    return (jax.nn.silu(x @ w1) * (x @ w3)) @ w2


def moe_swiglu(x, router, w1, w3, w2):
    logits = jnp.einsum('btd,de->bte', x, router, preferred_element_type=jnp.float32)
    top_val, top_idx = lax.top_k(logits, TOP_K)
    gate = jax.nn.softmax(top_val, axis=-1)
    dense_gate = jnp.einsum('btk,btke->bte', gate, jax.nn.one_hot(top_idx, N_EXPERTS, dtype=jnp.float32))
    y = jnp.zeros(x.shape, jnp.float32)
    for e in range(N_EXPERTS):
        y = y + dense_gate[..., e:e + 1] * swiglu(x, w1[e], w3[e], w2[e]).astype(jnp.float32)
    return y.astype(x.dtype)


def setup_inputs(seed: int = 0) -> dict:
    key = jax.random.key(seed)
    keys = iter(jax.random.split(key, 48))
    f32 = jnp.float32

    def normal(shape, scale=1.0):
        return jax.random.normal(next(keys), shape, f32) * scale

    def uniform(shape, lo, hi):
        return jax.random.uniform(next(keys), shape, f32, lo, hi)

    n_pages = PAST_LEN // PAGE_SIZE
    n_used = DEC_BATCH * n_pages
    n_pool = n_used + n_used // 4
    page_table = jax.random.permutation(next(keys), n_pool)[:n_used].reshape(DEC_BATCH, n_pages).astype(jnp.int32)
    n_dense = (DEPTH + 1) // 2
    n_moe = DEPTH // 2
    return {
        'x_prompt': normal((BATCH, SEQ, D_MODEL)),
        'x_sample': normal((DEC_BATCH, DEC_SEQ, D_MODEL)),
        'cache_k': normal((DEPTH, n_pool, PAGE_SIZE, H_A, 2 * D_HEAD_A)),
        'cache_v': normal((DEPTH, n_pool, PAGE_SIZE, H_A, 2 * D_HEAD_A)),
        'page_table': page_table,
        'state_ret': normal((DEPTH, DEC_BATCH, H_B, D_HEAD_B, D_HEAD_B), 0.5),
        'state_wkv': normal((DEPTH, DEC_BATCH, H_C, D_HEAD_C, D_HEAD_C), 0.5),
        'state_shift': normal((DEPTH, DEC_BATCH, PROJ_C)),
        'rel_bias': normal((N_BUCKETS, H_A), 0.5),
        'w_in': normal((DEPTH, D_MODEL, PROJ_WIDTH), D_MODEL ** -0.5),
        'w_out': normal((DEPTH, MIX_WIDTH, D_MODEL), BETA * MIX_WIDTH ** -0.5),
        'lambda_q1': normal((DEPTH, D_HEAD_A), 0.1),
        'lambda_k1': normal((DEPTH, D_HEAD_A), 0.1),
        'lambda_q2': normal((DEPTH, D_HEAD_A), 0.1),
        'lambda_k2': normal((DEPTH, D_HEAD_A), 0.1),
        'subln_g': 1.0 + normal((DEPTH, 2 * D_HEAD_A), 0.02),
        'tshift_mu': uniform((DEPTH, PROJ_C), 0.0, 1.0),
        'decay_w0': uniform((DEPTH, W_C), -6.0, 1.0),
        'decay_w2': normal((DEPTH, DECAY_LORA, W_C), 0.5 * DECAY_LORA ** -0.5),
        'iclr_a0': normal((DEPTH, W_C), 0.1),
        'iclr_a2': normal((DEPTH, ICLR_LORA, W_C), 0.5 * ICLR_LORA ** -0.5),
        'gate_w2': normal((DEPTH, GATE_LORA, W_C), GATE_LORA ** -0.5),
        'k_k': 0.85 + normal((DEPTH, W_C), 0.05),
        'k_a': 1.0 + normal((DEPTH, W_C), 0.05),
        'r_k': normal((DEPTH, H_C, D_HEAD_C), 0.1),
        'lnx_g': 1.0 + normal((DEPTH, W_C), 0.02),
        'lnx_b': normal((DEPTH, W_C), 0.02),
        'ln1_g': 1.0 + normal((DEPTH, D_MODEL), 0.02),
        'ln1_b': normal((DEPTH, D_MODEL), 0.02),
        'ln2_g': 1.0 + normal((DEPTH, D_MODEL), 0.02),
        'ln2_b': normal((DEPTH, D_MODEL), 0.02),
        'ffn_w1': normal((n_dense, D_MODEL, D_FF), D_MODEL ** -0.5),
        'ffn_w3': normal((n_dense, D_MODEL, D_FF), D_MODEL ** -0.5),
        'ffn_w2': normal((n_dense, D_FF, D_MODEL), BETA * D_FF ** -0.5),
        'router_w': normal((n_moe, D_MODEL, N_EXPERTS), D_MODEL ** -0.5),
        'expert_w1': normal((n_moe, N_EXPERTS, D_MODEL, D_EXPERT), D_MODEL ** -0.5),
        'expert_w3': normal((n_moe, N_EXPERTS, D_MODEL, D_EXPERT), D_MODEL ** -0.5),
        'expert_w2': normal((n_moe, N_EXPERTS, D_EXPERT, D_MODEL), BETA * D_EXPERT ** -0.5),
    }


def reference(x_prompt, x_sample, cache_k, cache_v, page_table, state_ret, state_wkv, state_shift,
              rel_bias, w_in, w_out, lambda_q1, lambda_k1, lambda_q2, lambda_k2, subln_g,
              tshift_mu, decay_w0, decay_w2, iclr_a0, iclr_a2, gate_w2, k_k, k_a, r_k, lnx_g, lnx_b,
              ln1_g, ln1_b, ln2_g, ln2_b, ffn_w1, ffn_w3, ffn_w2, router_w, expert_w1, expert_w3, expert_w2):
    f32 = jnp.float32

    def run_layer(l, x, pos0, kv_past, s_ret0, s_wkv0, shift0):
        B, T, _ = x.shape
        p = jnp.einsum('btd,dc->btc', x, w_in[l])
        p_a = p[..., :PROJ_A]
        p_b = p[..., PROJ_A:PROJ_A + PROJ_B]
        p_c = p[..., PROJ_A + PROJ_B:]
        q_a, k_rows, v_rows = [t.reshape(B, T, H_A, 2 * D_HEAD_A) for t in jnp.split(p_a, 3, axis=-1)]
        lam_init = 0.8 - 0.6 * math.exp(-0.3 * l)
        lam = (jnp.exp(jnp.sum(lambda_q1[l].astype(f32) * lambda_k1[l].astype(f32)))
               - jnp.exp(jnp.sum(lambda_q2[l].astype(f32) * lambda_k2[l].astype(f32))) + lam_init)
        qm = q_a.reshape(B, T, H_A, 2, D_HEAD_A)
        km = k_rows.reshape(B, T, H_A, 2, D_HEAD_A)
        if kv_past is None:
            o_a = diff_attn_prompt(qm, km, v_rows, lam, rel_bias)
        else:
            o_a = diff_attn_sample(qm, km, v_rows, kv_past[0], kv_past[1], lam, rel_bias)
        out_a = (rms_norm(o_a, subln_g[l]) * (1.0 - lam_init)).reshape(B, T, W_A).astype(x.dtype)
        out_b, s_ret = retention_mixer(p_b, pos0, s_ret0)
        out_c, s_wkv, shift = rwkv7_mixer(p_c, shift0, s_wkv0, tshift_mu[l], decay_w0[l], decay_w2[l],
                                          iclr_a0[l], iclr_a2[l], gate_w2[l], k_k[l], k_a[l], r_k[l],
                                          lnx_g[l], lnx_b[l])
        mixed = jnp.concatenate([out_a, out_b.astype(x.dtype), out_c.astype(x.dtype)], axis=-1)
        x = layer_norm(ALPHA * x + jnp.einsum('btc,cd->btd', mixed, w_out[l]), ln1_g[l], ln1_b[l])
        j = l // 2
        if l % 2 == 0:
            f = swiglu(x, ffn_w1[j], ffn_w3[j], ffn_w2[j])
        else:
            f = moe_swiglu(x, router_w[j], expert_w1[j], expert_w3[j], expert_w2[j])
        x = layer_norm(ALPHA * x + f, ln2_g[l], ln2_b[l])
        return x, k_rows, v_rows, s_ret, s_wkv, shift

    bp = x_prompt.shape[0]
    n_dec = page_table.shape[0]
    zero_ret = jnp.zeros((bp, H_B, D_HEAD_B, D_HEAD_B), f32)
    zero_wkv = jnp.zeros((bp, H_C, D_HEAD_C, D_HEAD_C), f32)
    zero_shift = jnp.zeros((bp, PROJ_C), x_prompt.dtype)
    xp, xs = x_prompt, x_sample
    outs_p, outs_s = [], []
    for l in range(DEPTH):
        xp, *st_p = run_layer(l, xp, 0, None, zero_ret, zero_wkv, zero_shift)
        k_past = cache_k[l, page_table].reshape(n_dec, PAST_LEN, H_A, 2, D_HEAD_A)
        v_past = cache_v[l, page_table].reshape(n_dec, PAST_LEN, H_A, 2 * D_HEAD_A)
        xs, *st_s = run_layer(l, xs, PAST_LEN, (k_past, v_past), state_ret[l], state_wkv[l], state_shift[l])
        outs_p.append(st_p)
        outs_s.append(st_s)

    def stack(outs, i):
        return jnp.stack([o[i] for o in outs], axis=0)

    k_prompt, v_prompt = stack(outs_p, 0), stack(outs_p, 1)
    ret_prompt, wkv_prompt, shift_prompt = stack(outs_p, 2), stack(outs_p, 3), stack(outs_p, 4)
    k_sample, v_sample = stack(outs_s, 0), stack(outs_s, 1)
    ret_sample, wkv_sample, shift_sample = stack(outs_s, 2), stack(outs_s, 3), stack(outs_s, 4)
    return (xp, xs, k_prompt, v_prompt, k_sample, v_sample,
            ret_prompt, ret_sample, wkv_prompt, wkv_sample, shift_prompt, shift_sample)
```

```python
import functools
import math

import jax
import jax.numpy as jnp
from jax import lax
from jax.experimental import pallas as pl
from jax.experimental.pallas import tpu as pltpu

F32 = jnp.float32
BF16 = jnp.bfloat16

D_MODEL = 1024
N_HEADS = 4
D_HEAD_A = 64
W_A = N_HEADS * 2 * D_HEAD_A
D_HEAD = 64
W_BC = N_HEADS * D_HEAD
PROJ_A = 3 * W_A
PROJ_B = 4 * W_BC
PROJ_C = 1024
N_BUCKETS = 32
MAX_DISTANCE = 128
NEAR_DIST = 113
RWKV_GN_EPS = 64e-5
LN_EPS = 1e-5
RMS_EPS = 1e-5
NEG = -1e30

VMEM_LIMIT_BYTES = 56 * 1024 * 1024
LANES = 128


def _cparams(*sem):
    return pltpu.CompilerParams(dimension_semantics=sem, vmem_limit_bytes=VMEM_LIMIT_BYTES)


def _dot(a, b):
    return jnp.dot(a, b, preferred_element_type=F32)


def _dot_nt(a, b):
    return lax.dot_general(a, b, (((1,), (1,)), ((), ())), preferred_element_type=F32)


def _dot_tn(a, b):
    return lax.dot_general(a, b, (((0,), (0,)), ((), ())), preferred_element_type=F32)


def _split2(x):
    hi = x.astype(BF16)
    lo = (x - hi.astype(F32)).astype(BF16)
    return hi, lo


def _split3(x):
    hi = x.astype(BF16)
    r1 = x - hi.astype(F32)
    mid = r1.astype(BF16)
    lo = (r1 - mid.astype(F32)).astype(BF16)
    return hi, mid, lo


def _softplus(z):
    return jnp.maximum(z, 0.0) + jnp.log1p(jnp.exp(-jnp.abs(z)))


def _layer_norm(z, g, b):
    mu = jnp.mean(z, axis=-1, keepdims=True)
    d = z - mu
    var = jnp.mean(d * d, axis=-1, keepdims=True)
    return d * lax.rsqrt(var + LN_EPS) * g + b


def _head_masks():
    lane = lax.broadcasted_iota(jnp.int32, (1, W_BC), 1)
    return [(lane >= h * D_HEAD) & (lane < (h + 1) * D_HEAD) for h in range(N_HEADS)]


def _stack_heads(x, masks):
    zero = jnp.zeros((), x.dtype)
    return jnp.concatenate([jnp.where(m, x, zero) for m in masks], axis=0)


def _unstack_heads(z, c):
    return z[0:c] + z[c:2 * c] + z[2 * c:3 * c] + z[3 * c:4 * c]


def _head_ones():
    r = lax.broadcasted_iota(jnp.int32, (W_BC, W_BC), 0)
    c = lax.broadcasted_iota(jnp.int32, (W_BC, W_BC), 1)
    return jnp.where((r // D_HEAD) == (c // D_HEAD), 1.0, 0.0).astype(BF16)


def _head_sum(x, ones_bd):
    hi, lo = _split2(x)
    return _dot(hi, ones_bd) + _dot(lo, ones_bd)


def _proj_in_kernel(x_ref, w_ref, qa_ref, k_ref, v_ref, kb_ref, vb_ref, pb_ref, pc_ref):
    xb = x_ref[...].astype(BF16)

    def mm(lo, hi):
        return _dot(xb, w_ref[:, lo:hi])

    qa_ref[...] = (mm(0, W_A) * (D_HEAD_A ** -0.5)).astype(BF16)
    k = mm(W_A, 2 * W_A)
    k_ref[...] = k
    kb_ref[...] = k.astype(BF16)
    v = mm(2 * W_A, 3 * W_A)
    v_ref[...] = v
    vb_ref[...] = v.astype(BF16)
    pb_ref[...] = mm(PROJ_A, PROJ_A + PROJ_B)
    pc_ref[...] = mm(PROJ_A + PROJ_B, PROJ_A + PROJ_B + PROJ_C)


def _proj_in(x2d, w_bf, tm):
    m = x2d.shape[0]
    n = w_bf.shape[1]
    row = lambda i: (i, 0)
    return pl.pallas_call(
        _proj_in_kernel,
        grid=(m // tm,),
        in_specs=[pl.BlockSpec((tm, D_MODEL), row), pl.BlockSpec((D_MODEL, n), lambda i: (0, 0))],
        out_specs=[pl.BlockSpec((tm, W_A), row), pl.BlockSpec((tm, W_A), row), pl.BlockSpec((tm, W_A), row),
                   pl.BlockSpec((tm, W_A), row), pl.BlockSpec((tm, W_A), row),
                   pl.BlockSpec((tm, PROJ_B), row), pl.BlockSpec((tm, PROJ_C), row)],
        out_shape=[jax.ShapeDtypeStruct((m, W_A), BF16), jax.ShapeDtypeStruct((m, W_A), F32),
                   jax.ShapeDtypeStruct((m, W_A), F32), jax.ShapeDtypeStruct((m, W_A), BF16),
                   jax.ShapeDtypeStruct((m, W_A), BF16), jax.ShapeDtypeStruct((m, PROJ_B), F32),
                   jax.ShapeDtypeStruct((m, PROJ_C), F32)],
        compiler_params=_cparams("parallel"),
        name="proj_in",
    )(x2d, w_bf)


def _lambda_full(lamv_ref, laminit_ref):
    lv = lamv_ref[...]
    s1 = jnp.sum(lv[0:1] * lv[1:2], axis=-1, keepdims=True)
    s2 = jnp.sum(lv[2:3] * lv[3:4], axis=-1, keepdims=True)
    return jnp.exp(s1) - jnp.exp(s2) + laminit_ref[...]


def _sub_rms(o, g, scale):
    y = o * lax.rsqrt(jnp.mean(o * o, axis=-1, keepdims=True) + RMS_EPS)
    return y * g * scale


def _rel_bucket_table(max_dist):
    n = jnp.arange(max_dist + 1, dtype=jnp.int32)
    max_exact = N_BUCKETS // 2
    large = max_exact + (jnp.log(jnp.maximum(n, 1).astype(F32) / max_exact)
                         / math.log(MAX_DISTANCE / max_exact) * (N_BUCKETS - max_exact)).astype(jnp.int32)
    large = jnp.minimum(large, N_BUCKETS - 1)
    return jnp.where(n < max_exact, n, large)


def _attn_prompt_kernel(lamv_ref, laminit_ref, cfar_ref, q_ref, k_ref, v_ref, d0_ref, d1_ref, g_ref, o_ref, *, blk):
    i = pl.program_id(2)
    lam = _lambda_full(lamv_ref, laminit_ref)
    q = q_ref[...]
    lane = lax.broadcasted_iota(jnp.int32, (1, 2 * D_HEAD_A), 1)
    lo = lane < D_HEAD_A
    zero = jnp.zeros((), BF16)
    qmaps = (jnp.where(lo, q, zero), jnp.where(lo, zero, q))

    def scores(kblk, bias):
        return [_dot_nt(kblk, qm) + bias for qm in qmaps]

    def first(kblk, vblk, bias):
        out = []
        for s in scores(kblk, bias):
            m = jnp.max(s, axis=0, keepdims=True)
            p = jnp.exp(s - m)
            out.append((m, jnp.sum(p, axis=0, keepdims=True), _dot_tn(vblk, p.astype(BF16))))
        return tuple(out)

    def update(carry, kblk, vblk, bias):
        out = []
        for (m, l, acc), s in zip(carry, scores(kblk, bias)):
            mn = jnp.maximum(m, jnp.max(s, axis=0, keepdims=True))
            al = jnp.exp(m - mn)
            p = jnp.exp(s - mn)
            out.append((mn, al * l + jnp.sum(p, axis=0, keepdims=True),
                        al * acc + _dot_tn(vblk, p.astype(BF16))))
        return tuple(out)

    def kv(j):
        off = pl.multiple_of(j * blk, blk)
        return k_ref[pl.ds(off, blk), :], v_ref[pl.ds(off, blk), :]

    kd, vd = kv(i)
    carry = first(kd, vd, d0_ref[...])
    kp, vp = kv(jnp.maximum(i - 1, 0))
    carry = update(carry, kp, vp, jnp.where(i >= 1, d1_ref[...], NEG))
    cfar = cfar_ref[...]

    def far_body(j, c):
        kj, vj = kv(j)
        return update(c, kj, vj, cfar)

    carry = lax.fori_loop(0, jnp.maximum(i - 1, 0), far_body, carry)
    (_, l1, a1), (_, l2, a2) = carry
    o_t = a1 / l1 - lam * (a2 / l2)
    o = o_t.T
    o_ref[...] = _sub_rms(o, g_ref[...], 1.0 - laminit_ref[...]).astype(BF16)


def _attn_prompt(qa, kb, vb, lamv, laminit, rel_bias, subln_g, batch, seq, blk):
    m = batch * seq
    nq = seq // blk
    assert blk >= NEAR_DIST - 1
    table = _rel_bucket_table(2 * blk)
    cidx = jnp.arange(blk, dtype=jnp.int32)[:, None]
    ridx = jnp.arange(blk, dtype=jnp.int32)[None, :]
    dist0 = ridx - cidx
    bias0 = jnp.where((dist0 >= 0)[None], jnp.moveaxis(rel_bias[table[jnp.maximum(dist0, 0)]], -1, 0), NEG)
    bias1 = jnp.moveaxis(rel_bias[table[dist0 + blk]], -1, 0)
    cfar = rel_bias[N_BUCKETS - 1].reshape(N_HEADS, 1, 1)
    qspec = pl.BlockSpec((blk, 2 * D_HEAD_A), lambda b, h, i: (b * nq + i, h))
    kvspec = pl.BlockSpec((seq, 2 * D_HEAD_A), lambda b, h, i: (b, h))
    tile = pl.BlockSpec((None, blk, blk), lambda b, h, i: (h, 0, 0))
    const2 = lambda shape: pl.BlockSpec(shape, lambda b, h, i: (0, 0))
    return pl.pallas_call(
        functools.partial(_attn_prompt_kernel, blk=blk),
        grid=(batch, N_HEADS, nq),
        in_specs=[const2((4, D_HEAD_A)), const2((1, 1)),
                  pl.BlockSpec((None, 1, 1), lambda b, h, i: (h, 0, 0)),
                  qspec, kvspec, kvspec, tile, tile, const2((1, 2 * D_HEAD_A))],
        out_specs=qspec,
        out_shape=jax.ShapeDtypeStruct((m, W_A), BF16),
        compiler_params=_cparams("parallel", "parallel", "arbitrary"),
        name="attn_prompt",
    )(lamv, laminit, cfar, qa, kb, vb, bias0.astype(F32), bias1.astype(F32), subln_g.reshape(1, -1))


def _attn_sample_kernel(pt_ref, lamv_ref, laminit_ref, q_ref, kn_ref, vn_ref, bfar_ref, blast_ref, bnew_ref, g_ref,
                        *refs, group, n_steps, rows):
    k_refs = refs[:group]
    v_refs = refs[group:2 * group]
    o_ref = refs[2 * group]
    m_sc, l_sc, acc_sc = refs[2 * group + 1:]
    s_id = pl.program_id(1)

    q = q_ref[...]
    lane = lax.broadcasted_iota(jnp.int32, (1, 2 * D_HEAD_A), 1)
    lo = lane < D_HEAD_A
    zero = jnp.zeros((), BF16)
    wq = jnp.concatenate([jnp.where(lo, q, zero), jnp.where(lo, zero, q)], axis=0)

    @pl.when(s_id == 0)
    def _():
        m_sc[...] = jnp.full(m_sc.shape, NEG, F32)
        l_sc[...] = jnp.zeros(l_sc.shape, F32)
        acc_sc[...] = jnp.zeros(acc_sc.shape, F32)

    def update(kblk, vblk, bias):
        s = _dot_nt(wq, kblk) + bias
        m = m_sc[...]
        mn = jnp.maximum(m, jnp.max(s, axis=-1, keepdims=True))
        al = jnp.exp(m - mn)
        p = jnp.exp(s - mn)
        m_sc[...] = mn
        l_sc[...] = al * l_sc[...] + jnp.sum(p, axis=-1, keepdims=True)
        acc_sc[...] = al * acc_sc[...] + _dot(p.astype(BF16), vblk)

    @pl.when(s_id == 0)
    def _():
        update(kn_ref[...], vn_ref[...], bnew_ref[...])

    bfar = bfar_ref[...]
    for g in range(group):
        bias = bfar
        if g == group - 1:
            bias = jnp.where(s_id == n_steps - 1, blast_ref[...], bfar)
        update(k_refs[g][...].astype(BF16), v_refs[g][...].astype(BF16), bias)

    @pl.when(s_id == n_steps - 1)
    def _():
        lam = _lambda_full(lamv_ref, laminit_ref)
        o = acc_sc[...] / l_sc[...]
        od = o[0:rows] - lam * o[rows:2 * rows]
        o_ref[...] = _sub_rms(od, g_ref[...], 1.0 - laminit_ref[...]).astype(BF16)


def _attn_sample(qa, kb, vb, cache_k4, cache_v4, pt_flat, layer, lamv, laminit, rel_bias, subln_g,
                 n_dec, t_new, n_pages, page, group):
    rows = t_new * N_HEADS
    cols = page * N_HEADS
    n_steps = n_pages // group
    past = n_pages * page
    assert page + 1 >= NEAR_DIST
    table = _rel_bucket_table(page + t_new)
    r = jnp.arange(2 * rows, dtype=jnp.int32)[:, None]
    rt, rh = (r % rows) // N_HEADS, r % N_HEADS
    c = jnp.arange(cols, dtype=jnp.int32)[None, :]
    ct, ch = c // N_HEADS, c % N_HEADS
    same = rh == ch
    rb_rows = rel_bias[:, rh[:, 0]]
    far_val = rb_rows[N_BUCKETS - 1][:, None]
    bfar = jnp.where(same, far_val, NEG).astype(F32)
    dist_last = (past + rt) - ((n_pages - 1) * page + ct)
    blast = jnp.where(same, jnp.take_along_axis(rb_rows.T, table[dist_last], axis=1), NEG).astype(F32)
    cn = jnp.arange(rows, dtype=jnp.int32)[None, :]
    cnt, cnh = cn // N_HEADS, cn % N_HEADS
    dist_new = rt - cnt
    bnew = jnp.where((rh == cnh) & (dist_new >= 0),
                     jnp.take_along_axis(rb_rows.T, table[jnp.maximum(dist_new, 0)], axis=1), NEG).astype(F32)

    q3 = qa.reshape(n_dec, rows, 2 * D_HEAD_A)
    kn3 = kb.reshape(n_dec, rows, 2 * D_HEAD_A)
    vn3 = vb.reshape(n_dec, rows, 2 * D_HEAD_A)
    per_b = pl.BlockSpec((None, rows, 2 * D_HEAD_A), lambda b, s, pt: (b, 0, 0))
    const = lambda shape: pl.BlockSpec(shape, lambda b, s, pt: (0, 0))

    def page_spec(g):
        return pl.BlockSpec((None, None, cols, 2 * D_HEAD_A),
                            lambda b, s, pt: (layer, pt[b * n_pages + s * group + g], 0, 0))

    in_specs = [const((4, D_HEAD_A)), const((1, 1)), per_b, per_b, per_b,
                const((2 * rows, cols)), const((2 * rows, cols)), const((2 * rows, rows)),
                const((1, 2 * D_HEAD_A))]
    in_specs += [page_spec(g) for g in range(group)] * 2
    out = pl.pallas_call(
        functools.partial(_attn_sample_kernel, group=group, n_steps=n_steps, rows=rows),
        grid_spec=pltpu.PrefetchScalarGridSpec(
            num_scalar_prefetch=1,
            grid=(n_dec, n_steps),
            in_specs=in_specs,
            out_specs=per_b,
            scratch_shapes=[pltpu.VMEM((2 * rows, 1), F32), pltpu.VMEM((2 * rows, 1), F32),
                            pltpu.VMEM((2 * rows, 2 * D_HEAD_A), F32)],
        ),
        out_shape=jax.ShapeDtypeStruct((n_dec, rows, 2 * D_HEAD_A), BF16),
        compiler_params=_cparams("parallel", "arbitrary"),
        name="attn_sample",
    )(pt_flat, lamv, laminit, q3, kn3, vn3, bfar, blast, bnew, subln_g.reshape(1, -1),
      *([cache_k4] * group), *([cache_v4] * group))
    return out.reshape(n_dec * t_new, W_A)


def _ret_kernel(pb_ref, cos_ref, sin_ref, s0_ref, o_ref, sfin_ref, st_sc, *, chunk, n_valid, n_chunks):
    c_id = pl.program_id(1)
    lg = chunk.bit_length() - 1
    rows = N_HEADS * chunk
    masks = _head_masks()
    ones_bd = _head_ones()

    @pl.when(c_id == 0)
    def _():
        st_sc[...] = s0_ref[...]

    pb = pb_ref[...]
    q, k, v, g = pb[:, 0:W_BC], pb[:, W_BC:2 * W_BC], pb[:, 2 * W_BC:3 * W_BC], pb[:, 3 * W_BC:4 * W_BC]
    lane = lax.broadcasted_iota(jnp.int32, (1, W_BC), 1)
    first_half = (lane & (D_HEAD - 1)) < (D_HEAD // 2)
    cosf, sins = cos_ref[...], sin_ref[...]

    def rot(x):
        swapped = jnp.where(first_half, pltpu.roll(x, W_BC - D_HEAD // 2, 1), pltpu.roll(x, D_HEAD // 2, 1))
        return x * cosf + swapped * sins

    q = rot(q)
    k = rot(k) * (D_HEAD ** -0.5)
    row = lax.broadcasted_iota(jnp.int32, (chunk, 1), 0)
    if n_valid < chunk:
        valid = row < n_valid
        k = jnp.where(valid, k, 0.0)
        v = jnp.where(valid, v, 0.0)

    log_g = [math.log1p(-(2.0 ** (-5 - h))) for h in range(N_HEADS)]
    lg_lane = sum(jnp.where(m, log_g[h], 0.0) for h, m in enumerate(masks))
    ri = lax.broadcasted_iota(jnp.int32, (rows, rows), 0)
    ci = lax.broadcasted_iota(jnp.int32, (rows, rows), 1)
    rh = lax.broadcasted_iota(jnp.int32, (rows, 1), 0) >> lg
    rpos = (lax.broadcasted_iota(jnp.int32, (rows, 1), 0) & (chunk - 1)).astype(F32)
    lg_row = sum(jnp.where(rh == h, log_g[h], 0.0) for h in range(N_HEADS))
    diff = (ri & (chunk - 1)) - (ci & (chunk - 1))
    keep = ((ri >> lg) == (ci >> lg)) & (diff >= 0)
    dmask = jnp.where(keep, jnp.exp(lg_row * jnp.maximum(diff, 0).astype(F32)), 0.0)
    xi = jnp.exp(lg_row * (rpos + 1.0))
    rowf = row.astype(F32)
    zeta = jnp.exp(lg_lane * jnp.maximum(float(n_valid - 1) - rowf, 0.0))
    g_chunk = jnp.exp(lg_lane * float(n_valid))

    q_st = _stack_heads(q.astype(BF16), masks)
    k_st = _stack_heads(k.astype(BF16), masks)
    v_st = _stack_heads(v.astype(BF16), masks)
    kz_st = _stack_heads((k * zeta).astype(BF16), masks)
    state = st_sc[...]
    s_blk = _dot_nt(q_st, k_st) * dmask
    inner = _dot(s_blk.astype(BF16), v_st)
    cross = _dot(q_st, state.astype(BF16)) * xi
    o = _unstack_heads(inner + cross, chunk)
    st_new = state * g_chunk + _dot_tn(kz_st, v_st)
    st_sc[...] = st_new

    ms = _head_sum(o * o, ones_bd) * (1.0 / D_HEAD)
    o_ref[...] = (jax.nn.silu(g) * (o * lax.rsqrt(ms + RMS_EPS))).astype(BF16)

    @pl.when(c_id == n_chunks - 1)
    def _():
        sfin_ref[...] = st_new


def _retention(pb, cos_t, sin_t, s0_bd, batch, t_pad, chunk, n_valid):
    n_chunks = t_pad // chunk
    rowspec = lambda w: pl.BlockSpec((chunk, w), lambda b, c: (b * n_chunks + c, 0))
    stspec = pl.BlockSpec((None, W_BC, W_BC), lambda b, c: (b, 0, 0))
    return pl.pallas_call(
        functools.partial(_ret_kernel, chunk=chunk, n_valid=n_valid, n_chunks=n_chunks),
        grid=(batch, n_chunks),
        in_specs=[rowspec(PROJ_B), pl.BlockSpec((chunk, W_BC), lambda b, c: (c, 0)),
                  pl.BlockSpec((chunk, W_BC), lambda b, c: (c, 0)), stspec],
        out_specs=[rowspec(W_BC), stspec],
        out_shape=[jax.ShapeDtypeStruct((batch * t_pad, W_BC), BF16),
                   jax.ShapeDtypeStruct((batch, W_BC, W_BC), F32)],
        scratch_shapes=[pltpu.VMEM((W_BC, W_BC), F32)],
        compiler_params=_cparams("parallel", "arbitrary"),
        name="retention",
    )(pb, cos_t, sin_t, s0_bd)


def _rwkv_kernel(pc_ref, shift_ref, s0_ref, mu_ref, w0_ref, w2_ref, a0_ref, a2_ref, g2_ref, kk_ref, ka_ref,
                 rk_ref, lng_ref, lnb_ref, o_ref, sfin_ref, st_sc, prev_sc, *, chunk, n_valid, n_chunks):
    c_id = pl.program_id(1)
    lg = chunk.bit_length() - 1
    rows = N_HEADS * chunk
    masks = _head_masks()
    ones_bd = _head_ones()

    @pl.when(c_id == 0)
    def _():
        st_sc[...] = s0_ref[...]
        prev_sc[...] = shift_ref[...]

    pc = pc_ref[...]
    row = lax.broadcasted_iota(jnp.int32, (chunk, 1), 0)
    prev = jnp.where(row == 0, prev_sc[...], pltpu.roll(pc, 1, 0))
    prev_sc[...] = pc[chunk - 1:chunk, :]
    xm = pc + (prev - pc) * mu_ref[...]
    r, k, v, xt = xm[:, 0:W_BC], xm[:, W_BC:2 * W_BC], xm[:, 2 * W_BC:3 * W_BC], xm[:, 3 * W_BC:4 * W_BC]

    w_log = -_softplus(-(w0_ref[...] + _dot(jnp.tanh(xt).astype(BF16), w2_ref[...]))) - 0.5
    logw = -jnp.exp(w_log)
    a = jax.nn.sigmoid(a0_ref[...] + _dot(xt.astype(BF16), a2_ref[...]))
    g = _dot(jax.nn.sigmoid(xt).astype(BF16), g2_ref[...])

    kk = k * kk_ref[...]
    kk = kk / jnp.maximum(jnp.sqrt(_head_sum(kk * kk, ones_bd)), 1e-12)
    k2 = k * (1.0 + (a - 1.0) * ka_ref[...])
    if n_valid < chunk:
        valid = row < n_valid
        logw = jnp.where(valid, logw, 0.0)
        kk = jnp.where(valid, kk, 0.0)
        k2 = jnp.where(valid, k2, 0.0)
        v = jnp.where(valid, v, 0.0)

    ti = lax.broadcasted_iota(jnp.int32, (chunk, chunk), 0)
    tj = lax.broadcasted_iota(jnp.int32, (chunk, chunk), 1)
    tri = jnp.where(ti >= tj, 1.0, 0.0).astype(BF16)
    cum = sum(_dot(tri, part) for part in _split3(logw))
    cum_prev = cum - logw
    cum_end = cum[chunk - 1:chunk, :]
    e_fwd = jnp.exp(cum)
    e_bwd = jnp.exp(-cum)
    e_end = jnp.exp(cum_end - cum)
    kka = kk * a
    a_t = -kk * jnp.exp(cum_prev)
    b_t = kka * e_bwd
    k_t = k2 * e_bwd
    r_t = r * e_fwd
    b_h = kka * e_end
    k_h = k2 * e_end
    p_end = jnp.exp(cum_end)

    st = lambda x: _stack_heads(x.astype(BF16), masks)
    a_st, b_st, k_st, r_st, v_st, bh_st, kh_st = st(a_t), st(b_t), st(k_t), st(r_t), st(v), st(b_h), st(k_h)

    ri = lax.broadcasted_iota(jnp.int32, (rows, rows), 0)
    ci = lax.broadcasted_iota(jnp.int32, (rows, rows), 1)
    same = (ri >> lg) == (ci >> lg)
    dpos = (ri & (chunk - 1)) - (ci & (chunk - 1))
    strict = same & (dpos > 0)
    incl = same & (dpos >= 0)

    n_bd = jnp.where(strict, _dot_nt(a_st, b_st), 0.0)
    ak_bd = jnp.where(strict, _dot_nt(a_st, k_st), 0.0)
    m_rb = jnp.where(incl, _dot_nt(r_st, b_st), 0.0).astype(BF16)
    m_rk = jnp.where(incl, _dot_nt(r_st, k_st), 0.0).astype(BF16)

    def mm3(x, y):
        xh, xl = _split2(x)
        yh, yl = _split2(y)
        return _dot(xh, yh) + _dot(xh, yl) + _dot(xl, yh)

    eye = jnp.where(ri == ci, 1.0, 0.0)
    t_inv = eye + n_bd
    pw = n_bd
    for _ in range(lg - 1):
        pw = mm3(pw, pw)
        t_inv = t_inv + mm3(t_inv, pw)
    th, tl = _split2(t_inv)

    def t_apply(y_bf):
        return _dot(th, y_bf) + _dot(tl, y_bf)

    w_st = t_apply(a_st).astype(BF16)
    y_st = t_apply(_dot(ak_bd.astype(BF16), v_st).astype(BF16))
    o_pre = _dot(m_rk, v_st)
    g_pre = _dot_tn(v_st, kh_st)

    state = st_sc[...]
    sb = state.astype(BF16)
    u = _dot_nt(w_st, sb) + y_st
    ub = u.astype(BF16)
    o_stk = _dot_nt(r_st, sb) + _dot(m_rb, ub) + o_pre
    st_new = state * p_end + _dot_tn(ub, bh_st) + g_pre
    st_sc[...] = st_new

    o = _unstack_heads(o_stk, chunk)
    inv = 1.0 / D_HEAD
    mean = _head_sum(o, ones_bd) * inv
    d = o - mean
    var = _head_sum(d * d, ones_bd) * inv
    o_n = d * lax.rsqrt(var + RWKV_GN_EPS) * lng_ref[...] + lnb_ref[...]
    bonus = _head_sum(r * k2 * rk_ref[...], ones_bd) * v
    o_ref[...] = ((o_n + bonus) * g).astype(BF16)

    @pl.when(c_id == n_chunks - 1)
    def _():
        sfin_ref[...] = st_new


def _rwkv(pc, shift0, s0_bd, params, batch, t_pad, chunk, n_valid):
    n_chunks = t_pad // chunk
    rowspec = lambda w: pl.BlockSpec((chunk, w), lambda b, c: (b * n_chunks + c, 0))
    stspec = pl.BlockSpec((None, W_BC, W_BC), lambda b, c: (b, 0, 0))
    const = lambda arr: pl.BlockSpec(arr.shape, lambda b, c: (0,) * arr.ndim)
    return pl.pallas_call(
        functools.partial(_rwkv_kernel, chunk=chunk, n_valid=n_valid, n_chunks=n_chunks),
        grid=(batch, n_chunks),
        in_specs=[rowspec(PROJ_C), pl.BlockSpec((None, 1, PROJ_C), lambda b, c: (b, 0, 0)), stspec]
                 + [const(p) for p in params],
        out_specs=[rowspec(W_BC), stspec],
        out_shape=[jax.ShapeDtypeStruct((batch * t_pad, W_BC), BF16),
                   jax.ShapeDtypeStruct((batch, W_BC, W_BC), F32)],
        scratch_shapes=[pltpu.VMEM((W_BC, W_BC), F32), pltpu.VMEM((1, PROJ_C), F32)],
        compiler_params=_cparams("parallel", "arbitrary"),
        name="rwkv7",
    )(pc, shift0, s0_bd, *params)


def _out_proj_kernel(x_ref, a_ref, b_ref, c_ref, w_ref, g_ref, bb_ref, y_ref, *, alpha):
    acc = _dot(a_ref[...], w_ref[0:W_A, :])
    acc += _dot(b_ref[...], w_ref[W_A:W_A + W_BC, :])
    acc += _dot(c_ref[...], w_ref[W_A + W_BC:W_A + 2 * W_BC, :])
    y_ref[...] = _layer_norm(alpha * x_ref[...] + acc, g_ref[...], bb_ref[...])


def _out_proj(x2d, oa, ob, oc, w_bf, g, b, alpha, tm):
    m = x2d.shape[0]
    row = lambda w: pl.BlockSpec((tm, w), lambda i: (i, 0))
    const = lambda shape: pl.BlockSpec(shape, lambda i: (0, 0))
    return pl.pallas_call(
        functools.partial(_out_proj_kernel, alpha=alpha),
        grid=(m // tm,),
        in_specs=[row(D_MODEL), row(W_A), row(W_BC), row(W_BC), const((D_MODEL, D_MODEL)),
                  const((1, D_MODEL)), const((1, D_MODEL))],
        out_specs=row(D_MODEL),
        out_shape=jax.ShapeDtypeStruct((m, D_MODEL), F32),
        compiler_params=_cparams("parallel"),
        name="out_proj",
    )(x2d, oa, ob, oc, w_bf, g.reshape(1, -1), b.reshape(1, -1))


def _ffn_kernel(x_ref, w1_ref, w3_ref, w2_ref, g_ref, b_ref, y_ref, acc_sc, *, alpha, tf):
    f = pl.program_id(1)
    x = x_ref[...]
    xb = x.astype(BF16)

    @pl.when(f == 0)
    def _():
        acc_sc[...] = jnp.zeros(acc_sc.shape, F32)

    h = jax.nn.silu(_dot(xb, w1_ref[...])) * _dot(xb, w3_ref[...])
    acc_sc[...] += _dot(h.astype(BF16), w2_ref[...])

    @pl.when(f == pl.num_programs(1) - 1)
    def _():
        y_ref[...] = _layer_norm(alpha * x + acc_sc[...], g_ref[...], b_ref[...])


def _ffn(x2d, w1, w3, w2, g, b, alpha, tm, tf):
    m = x2d.shape[0]
    d_ff = w1.shape[1]
    return pl.pallas_call(
        functools.partial(_ffn_kernel, alpha=alpha, tf=tf),
        grid=(m // tm, d_ff // tf),
        in_specs=[pl.BlockSpec((tm, D_MODEL), lambda i, f: (i, 0)),
                  pl.BlockSpec((D_MODEL, tf), lambda i, f: (0, f)),
                  pl.BlockSpec((D_MODEL, tf), lambda i, f: (0, f)),
                  pl.BlockSpec((tf, D_MODEL), lambda i, f: (f, 0)),
                  pl.BlockSpec((1, D_MODEL), lambda i, f: (0, 0)),
                  pl.BlockSpec((1, D_MODEL), lambda i, f: (0, 0))],
        out_specs=pl.BlockSpec((tm, D_MODEL), lambda i, f: (i, 0)),
        out_shape=jax.ShapeDtypeStruct((m, D_MODEL), F32),
        scratch_shapes=[pltpu.VMEM((tm, D_MODEL), F32)],
        compiler_params=_cparams("parallel", "arbitrary"),
        name="ffn",
    )(x2d, w1, w3, w2, g.reshape(1, -1), b.reshape(1, -1))


def _moe_kernel(x_ref, rh_ref, rl_ref, w1_ref, w3_ref, w2_ref, g_ref, b_ref, y_ref, acc_sc, gate_sc, xb_sc,
                *, alpha, n_experts):
    e = pl.program_id(1)
    lane = lax.broadcasted_iota(jnp.int32, (1, LANES), 1)

    @pl.when(e == 0)
    def _():
        x = x_ref[...]
        xh, xl = _split2(x)
        xb_sc[...] = xh
        logits = _dot(xh, rh_ref[...]) + _dot(xl, rh_ref[...]) + _dot(xh, rl_ref[...])
        logits = jnp.where(lane < n_experts, logits, NEG)
        v1 = jnp.max(logits, axis=-1, keepdims=True)
        i1 = jnp.min(jnp.where(logits == v1, lane, LANES), axis=-1, keepdims=True)
        rest = jnp.where(lane == i1, NEG, logits)
        v2 = jnp.max(rest, axis=-1, keepdims=True)
        i2 = jnp.min(jnp.where(rest == v2, lane, LANES), axis=-1, keepdims=True)
        ex = jnp.exp(v2 - v1)
        g1 = 1.0 / (1.0 + ex)
        g2 = ex / (1.0 + ex)
        gate_sc[...] = jnp.where(lane == i1, g1, 0.0) + jnp.where(lane == i2, g2, 0.0)
        acc_sc[...] = jnp.zeros(acc_sc.shape, F32)

    xb = xb_sc[...]
    h = jax.nn.silu(_dot(xb, w1_ref[...])) * _dot(xb, w3_ref[...])
    f = _dot(h.astype(BF16), w2_ref[...])
    ge = jnp.sum(jnp.where(lane == e, gate_sc[...], 0.0), axis=-1, keepdims=True)
    acc_sc[...] += ge * f

    @pl.when(e == n_experts - 1)
    def _():
        y_ref[...] = _layer_norm(alpha * x_ref[...] + acc_sc[...], g_ref[...], b_ref[...])


def _moe(x2d, router, w1, w3, w2, g, b, alpha, tm):
    m = x2d.shape[0]
    n_experts, _, d_e = w1.shape
    r_pad = jnp.zeros((D_MODEL, LANES), F32).at[:, :n_experts].set(router)
    r_hi = r_pad.astype(BF16)
    r_lo = (r_pad - r_hi.astype(F32)).astype(BF16)
    return pl.pallas_call(
        functools.partial(_moe_kernel, alpha=alpha, n_experts=n_experts),
        grid=(m // tm, n_experts),
        in_specs=[pl.BlockSpec((tm, D_MODEL), lambda i, e: (i, 0)),
                  pl.BlockSpec((D_MODEL, LANES), lambda i, e: (0, 0)),
                  pl.BlockSpec((D_MODEL, LANES), lambda i, e: (0, 0)),
                  pl.BlockSpec((None, D_MODEL, d_e), lambda i, e: (e, 0, 0)),
                  pl.BlockSpec((None, D_MODEL, d_e), lambda i, e: (e, 0, 0)),
                  pl.BlockSpec((None, d_e, D_MODEL), lambda i, e: (e, 0, 0)),
                  pl.BlockSpec((1, D_MODEL), lambda i, e: (0, 0)),
                  pl.BlockSpec((1, D_MODEL), lambda i, e: (0, 0))],
        out_specs=pl.BlockSpec((tm, D_MODEL), lambda i, e: (i, 0)),
        out_shape=jax.ShapeDtypeStruct((m, D_MODEL), F32),
        scratch_shapes=[pltpu.VMEM((tm, D_MODEL), F32), pltpu.VMEM((tm, LANES), F32),
                        pltpu.VMEM((tm, D_MODEL), BF16)],
        compiler_params=_cparams("parallel", "arbitrary"),
        name="moe",
    )(x2d, r_hi, r_lo, w1, w3, w2, g.reshape(1, -1), b.reshape(1, -1))


def _to_block_diag(s, transpose):
    if transpose:
        s = jnp.swapaxes(s, -1, -2)
    b = s.shape[0]
    eye = jnp.eye(N_HEADS, dtype=s.dtype)
    return jnp.einsum("bhij,hg->bhigj", s, eye).reshape(b, W_BC, W_BC)


def _from_block_diag(sbd, transpose):
    b = sbd.shape[0]
    s5 = sbd.reshape(b, N_HEADS, D_HEAD, N_HEADS, D_HEAD)
    s = jnp.stack([s5[:, h, :, h, :] for h in range(N_HEADS)], axis=1)
    return jnp.swapaxes(s, -1, -2) if transpose else s


def _rotary_tables(pos0, t_pad):
    half = D_HEAD // 2
    inv = 1.0 / (10000.0 ** (jnp.arange(half, dtype=F32) / half))
    ang = (pos0 + jnp.arange(t_pad, dtype=jnp.int32)).astype(F32)[:, None] * inv[None, :]
    cos, sin = jnp.cos(ang), jnp.sin(ang)
    cos_t = jnp.tile(jnp.concatenate([cos, cos], axis=-1), (1, N_HEADS))
    sin_t = jnp.tile(jnp.concatenate([-sin, sin], axis=-1), (1, N_HEADS))
    return cos_t, sin_t


def _pad_rows(a, width):
    lo, arr = width
    return jnp.zeros((W_BC, W_BC), F32).at[lo:lo + arr.shape[0]].set(arr).astype(BF16)


def kernel(x_prompt, x_sample, cache_k, cache_v, page_table, state_ret, state_wkv, state_shift, rel_bias, w_in, w_out, lambda_q1, lambda_k1, lambda_q2, lambda_k2, subln_g, tshift_mu, decay_w0, decay_w2, iclr_a0, iclr_a2, gate_w2, k_k, k_a, r_k, lnx_g, lnx_b, ln1_g, ln1_b, ln2_g, ln2_b, ffn_w1, ffn_w3, ffn_w2, router_w, expert_w1, expert_w3, expert_w2):
    bp, seq, _ = x_prompt.shape
    n_dec, t_new, _ = x_sample.shape
    depth, n_pool, page, _, _ = cache_k.shape
    n_pages = page_table.shape[1]
    past = n_pages * page
    alpha = (2 * depth) ** 0.25

    mp, ms = bp * seq, n_dec * t_new
    tm_p = 256 if mp % 256 == 0 else mp
    tm_s = ms
    blk = 256 if seq % 256 == 0 else seq
    chunk_p = 64 if seq % 64 == 0 else seq
    chunk_s = 32
    group = 8 if n_pages % 8 == 0 else 1

    cache_k4 = cache_k.reshape(depth, n_pool, page * N_HEADS, 2 * D_HEAD_A)
    cache_v4 = cache_v.reshape(depth, n_pool, page * N_HEADS, 2 * D_HEAD_A)
    pt_flat = page_table.reshape(-1).astype(jnp.int32)

    cos_p, sin_p = _rotary_tables(0, seq)
    cos_s, sin_s = _rotary_tables(past, chunk_s)
    zero_bd_p = jnp.zeros((bp, W_BC, W_BC), F32)
    zero_shift_p = jnp.zeros((bp, 1, PROJ_C), F32)

    def pad_sample(a):
        w = a.shape[-1]
        a3 = a.reshape(n_dec, t_new, w)
        return jnp.pad(a3, ((0, 0), (0, chunk_s - t_new), (0, 0))).reshape(n_dec * chunk_s, w)

    def unpad_sample(a):
        w = a.shape[-1]
        return a.reshape(n_dec, chunk_s, w)[:, :t_new].reshape(n_dec * t_new, w)

    xp = x_prompt.reshape(mp, D_MODEL)
    xs = x_sample.reshape(ms, D_MODEL)
    outs = {k: [] for k in ("kp", "vp", "ks", "vs", "rp", "rs", "wp", "ws", "sp", "ss")}
    for l in range(depth):
        w_in_bf = w_in[l].astype(BF16)
        w_out_bf = w_out[l].astype(BF16)
        lam_init = 0.8 - 0.6 * math.exp(-0.3 * l)
        lamv = jnp.stack([lambda_q1[l], lambda_k1[l], lambda_q2[l], lambda_k2[l]]).astype(F32)
        laminit = jnp.full((1, 1), lam_init, F32)
        row = lambda a: a.reshape(1, -1).astype(F32)
        rwkv_params = [row(tshift_mu[l]), row(decay_w0[l]), _pad_rows(None, (0, decay_w2[l])), row(iclr_a0[l]),
                       _pad_rows(None, (64, iclr_a2[l])), _pad_rows(None, (128, gate_w2[l])),
                       row(k_k[l]), row(k_a[l]), row(r_k[l]), row(lnx_g[l]), row(lnx_b[l])]

        qa, k32, v32, kb, vb, pb, pc = _proj_in(xp, w_in_bf, tm_p)
        oa = _attn_prompt(qa, kb, vb, lamv, laminit, rel_bias, subln_g[l], bp, seq, blk)
        ob, ret_bd = _retention(pb, cos_p, sin_p, zero_bd_p, bp, seq, chunk_p, chunk_p)
        oc, wkv_bd = _rwkv(pc, zero_shift_p, zero_bd_p, rwkv_params, bp, seq, chunk_p, chunk_p)
        xp = _out_proj(xp, oa, ob, oc, w_out_bf, ln1_g[l], ln1_b[l], alpha, tm_p)
        outs["kp"].append(k32.reshape(bp, seq, N_HEADS, 2 * D_HEAD_A))
        outs["vp"].append(v32.reshape(bp, seq, N_HEADS, 2 * D_HEAD_A))
        outs["rp"].append(_from_block_diag(ret_bd, False))
        outs["wp"].append(_from_block_diag(wkv_bd, False))
        outs["sp"].append(pc.reshape(bp, seq, PROJ_C)[:, -1])

        qa, k32, v32, kb, vb, pb, pc = _proj_in(xs, w_in_bf, tm_s)
        oa = _attn_sample(qa, kb, vb, cache_k4, cache_v4, pt_flat, l, lamv, laminit, rel_bias, subln_g[l],
                          n_dec, t_new, n_pages, page, group)
        ob, ret_bd = _retention(pad_sample(pb), cos_s, sin_s, _to_block_diag(state_ret[l].astype(F32), False),
                                n_dec, chunk_s, chunk_s, t_new)
        oc, wkv_bd = _rwkv(pad_sample(pc), state_shift[l].astype(F32).reshape(n_dec, 1, PROJ_C),
                           _to_block_diag(state_wkv[l].astype(F32), False), rwkv_params,
                           n_dec, chunk_s, chunk_s, t_new)
        xs = _out_proj(xs, oa, unpad_sample(ob), unpad_sample(oc), w_out_bf, ln1_g[l], ln1_b[l], alpha, tm_s)
        outs["ks"].append(k32.reshape(n_dec, t_new, N_HEADS, 2 * D_HEAD_A))
        outs["vs"].append(v32.reshape(n_dec, t_new, N_HEADS, 2 * D_HEAD_A))
        outs["rs"].append(_from_block_diag(ret_bd, False))
        outs["ws"].append(_from_block_diag(wkv_bd, False))
        outs["ss"].append(pc.reshape(n_dec, t_new, PROJ_C)[:, -1])

        j = l // 2
        if l % 2 == 0:
            w1, w3, w2 = ffn_w1[j].astype(BF16), ffn_w3[j].astype(BF16), ffn_w2[j].astype(BF16)
            tf = 256 if w1.shape[1] % 256 == 0 else w1.shape[1]
            xp = _ffn(xp, w1, w3, w2, ln2_g[l], ln2_b[l], alpha, min(512, mp), tf)
            xs = _ffn(xs, w1, w3, w2, ln2_g[l], ln2_b[l], alpha, tm_s, tf)
        else:
            w1, w3, w2 = expert_w1[j].astype(BF16), expert_w3[j].astype(BF16), expert_w2[j].astype(BF16)
            xp = _moe(xp, router_w[j], w1, w3, w2, ln2_g[l], ln2_b[l], alpha, min(512, mp))
            xs = _moe(xs, router_w[j], w1, w3, w2, ln2_g[l], ln2_b[l], alpha, tm_s)

    st = lambda key: jnp.stack(outs[key], axis=0)
    return (xp.reshape(bp, seq, D_MODEL), xs.reshape(n_dec, t_new, D_MODEL),
            st("kp"), st("vp"), st("ks"), st("vs"), st("rp"), st("rs"), st("wp"), st("ws"), st("sp"), st("ss"))
```

```python
import functools
import math

import jax
import jax.numpy as jnp
from jax import lax
from jax.experimental import pallas as pl
from jax.experimental.pallas import tpu as pltpu

F32 = jnp.float32
BF16 = jnp.bfloat16

D_MODEL = 1024
N_HEADS = 4
D_HEAD_A = 64
W_A = N_HEADS * 2 * D_HEAD_A
D_HEAD = 64
W_BC = N_HEADS * D_HEAD
PROJ_A = 3 * W_A
PROJ_B = 4 * W_BC
PROJ_C = 1024
N_BUCKETS = 32
MAX_DISTANCE = 128
NEAR_DIST = 113
RWKV_GN_EPS = 64e-5
LN_EPS = 1e-5
RMS_EPS = 1e-5
NEG = -1e30
LOG2E = 1.4426950408889634

VMEM_LIMIT_BYTES = 56 * 1024 * 1024
LANES = 128


def _cparams(*sem):
    return pltpu.CompilerParams(dimension_semantics=sem, vmem_limit_bytes=VMEM_LIMIT_BYTES)


def _dot(a, b):
    return jnp.dot(a, b, preferred_element_type=F32)


def _dot_nt(a, b):
    return lax.dot_general(a, b, (((1,), (1,)), ((), ())), preferred_element_type=F32)


def _dot_tn(a, b):
    return lax.dot_general(a, b, (((0,), (0,)), ((), ())), preferred_element_type=F32)


def _split2(x):
    hi = x.astype(BF16)
    lo = (x - hi.astype(F32)).astype(BF16)
    return hi, lo


def _split3(x):
    hi = x.astype(BF16)
    r1 = x - hi.astype(F32)
    mid = r1.astype(BF16)
    lo = (r1 - mid.astype(F32)).astype(BF16)
    return hi, mid, lo


def _softplus(z):
    return jnp.maximum(z, 0.0) + jnp.log1p(jnp.exp(-jnp.abs(z)))


def _layer_norm(z, g, b):
    mu = jnp.mean(z, axis=-1, keepdims=True)
    d = z - mu
    var = jnp.mean(d * d, axis=-1, keepdims=True)
    return d * lax.rsqrt(var + LN_EPS) * g + b


def _head_masks():
    lane = lax.broadcasted_iota(jnp.int32, (1, W_BC), 1)
    return [(lane >= h * D_HEAD) & (lane < (h + 1) * D_HEAD) for h in range(N_HEADS)]


def _stack_heads(x, masks):
    zero = jnp.zeros((), x.dtype)
    return jnp.concatenate([jnp.where(m, x, zero) for m in masks], axis=0)


def _unstack_heads(z, c):
    return z[0:c] + z[c:2 * c] + z[2 * c:3 * c] + z[3 * c:4 * c]


def _head_ones():
    r = lax.broadcasted_iota(jnp.int32, (W_BC, W_BC), 0)
    c = lax.broadcasted_iota(jnp.int32, (W_BC, W_BC), 1)
    return jnp.where((r // D_HEAD) == (c // D_HEAD), 1.0, 0.0).astype(BF16)


def _head_sum(x, ones_bd):
    hi, lo = _split2(x)
    return _dot(hi, ones_bd) + _dot(lo, ones_bd)


def _proj_in_kernel(*refs, aliased):
    x_ref, w_ref = refs[0], refs[1]
    qa_ref, k_ref, v_ref, kb_ref, vb_ref, pb_ref, pc_ref = refs[4:] if aliased else refs[2:]
    xb = x_ref[...].astype(BF16)

    def mm(lo, hi):
        return _dot(xb, w_ref[:, lo:hi])

    qa_ref[...] = (mm(0, W_A) * (D_HEAD_A ** -0.5 * LOG2E)).astype(BF16)
    k = mm(W_A, 2 * W_A)
    k_ref[...] = k
    kb_ref[...] = k.astype(BF16)
    v = mm(2 * W_A, 3 * W_A)
    v_ref[...] = v
    vb_ref[...] = v.astype(BF16)
    pb_ref[...] = mm(PROJ_A, PROJ_A + PROJ_B)
    pc_ref[...] = mm(PROJ_A + PROJ_B, PROJ_A + PROJ_B + PROJ_C)


def _proj_in(x2d, w_bf, k_all, v_all, layer, depth, tm):
    m = x2d.shape[0]
    n = w_bf.shape[1]
    aliased = k_all is not None
    row = lambda i: (i, 0)
    stack_spec = pl.BlockSpec((None, tm, W_A), lambda i: (layer, i, 0))
    in_specs = [pl.BlockSpec((tm, D_MODEL), row), pl.BlockSpec((D_MODEL, n), lambda i: (0, 0))]
    args = [x2d, w_bf]
    if aliased:
        in_specs += [pl.BlockSpec(memory_space=pl.ANY), pl.BlockSpec(memory_space=pl.ANY)]
        args += [k_all, v_all]
    return pl.pallas_call(
        functools.partial(_proj_in_kernel, aliased=aliased),
        grid=(m // tm,),
        in_specs=in_specs,
        out_specs=[pl.BlockSpec((tm, W_A), row), stack_spec, stack_spec,
                   pl.BlockSpec((tm, W_A), row), pl.BlockSpec((tm, W_A), row),
                   pl.BlockSpec((tm, PROJ_B), row), pl.BlockSpec((tm, PROJ_C), row)],
        out_shape=[jax.ShapeDtypeStruct((m, W_A), BF16), jax.ShapeDtypeStruct((depth, m, W_A), F32),
                   jax.ShapeDtypeStruct((depth, m, W_A), F32), jax.ShapeDtypeStruct((m, W_A), BF16),
                   jax.ShapeDtypeStruct((m, W_A), BF16), jax.ShapeDtypeStruct((m, PROJ_B), F32),
                   jax.ShapeDtypeStruct((m, PROJ_C), F32)],
        input_output_aliases={2: 1, 3: 2} if aliased else {},
        compiler_params=_cparams("parallel"),
        name="proj_in",
    )(*args)


def _lambda_full(lamv_ref, laminit_ref):
    lv = lamv_ref[...]
    s1 = jnp.sum(lv[0:1] * lv[1:2], axis=-1, keepdims=True)
    s2 = jnp.sum(lv[2:3] * lv[3:4], axis=-1, keepdims=True)
    return jnp.exp(s1) - jnp.exp(s2) + laminit_ref[...]


def _sub_rms(o, g, scale):
    y = o * lax.rsqrt(jnp.mean(o * o, axis=-1, keepdims=True) + RMS_EPS)
    return y * g * scale


def _rel_bucket(n):
    max_exact = N_BUCKETS // 2
    large = max_exact + (jnp.log(jnp.maximum(n, 1).astype(F32) / max_exact)
                         / math.log(MAX_DISTANCE / max_exact) * (N_BUCKETS - max_exact)).astype(jnp.int32)
    large = jnp.minimum(large, N_BUCKETS - 1)
    return jnp.where(n < max_exact, n, large)


FAR = 8
V_AUG = 2 * D_HEAD_A + 16


def _attn_prompt_kernel(lamv_ref, laminit_ref, q_ref, k_ref, v_ref, near_ref, g_ref, o_ref,
                        vt_sc, m0_sc, m1_sc, acc0_sc, acc1_sc, s0_sc, s1_sc, p0_sc, p1_sc, *, blk, seq):
    i = pl.program_id(2)
    dv = 2 * D_HEAD_A
    m_sc, acc_sc, s_sc, p_sc = (m0_sc, m1_sc), (acc0_sc, acc1_sc), (s0_sc, s1_sc), (p0_sc, p1_sc)

    @pl.when(i == 0)
    def _():
        for c in range(seq // blk):
            vt = v_ref[c * blk:(c + 1) * blk, :].astype(F32).T
            vt_sc[0:dv, c * blk:(c + 1) * blk] = vt.astype(BF16)
        vt_sc[dv:V_AUG, :] = jnp.ones((V_AUG - dv, seq), BF16)

    q = q_ref[...]
    lane = lax.broadcasted_iota(jnp.int32, (1, dv), 1)
    lo = lane < D_HEAD_A
    zero = jnp.zeros((), BF16)
    qmaps = (jnp.where(lo, q, zero), jnp.where(lo, zero, q))

    def update(j, n, bias_ref, init):
        off = pl.multiple_of(j * blk, blk)
        m_old = [None, None] if init else [m_sc[mp][...] for mp in range(2)]
        mn = list(m_old)
        for sub in range(n):
            kblk = k_ref[pl.ds(off + sub * blk, blk), :]
            for mp in range(2):
                s = _dot_nt(kblk, qmaps[mp])
                if bias_ref is not None:
                    s = s + bias_ref[sub * blk:(sub + 1) * blk, :]
                s_sc[mp][sub] = s
                smax = jnp.max(s, axis=0, keepdims=True)
                mn[mp] = smax if mn[mp] is None else jnp.maximum(mn[mp], smax)
        for sub in range(n):
            for mp in range(2):
                p_sc[mp][sub * blk:(sub + 1) * blk, :] = jnp.exp2(s_sc[mp][sub] - mn[mp]).astype(BF16)
        vtb = vt_sc[:, pl.ds(off, n * blk)]
        for mp in range(2):
            pv = _dot(vtb, p_sc[mp][0:n * blk, :])
            m_sc[mp][...] = mn[mp]
            acc_sc[mp][...] = pv if init else jnp.exp2(m_old[mp] - mn[mp]) * acc_sc[mp][...] + pv

    n_far = jnp.maximum(i - 1, 0)
    update(n_far, 2, near_ref, True)

    def far_body(t, carry):
        update(FAR * t, FAR, None, False)
        return carry

    lax.fori_loop(0, n_far // FAR, far_body, 0)
    step = FAR // 2
    while step >= 1:
        done = (n_far // (2 * step)) * (2 * step)

        @pl.when((n_far & step) != 0)
        def _(done=done, step=step):
            update(done, step, None, False)

        step //= 2

    lam = _lambda_full(lamv_ref, laminit_ref)
    a1, a2 = acc0_sc[...], acc1_sc[...]
    o_t = a1[0:dv] / a1[dv:dv + 1] - lam * (a2[0:dv] / a2[dv:dv + 1])
    o_ref[...] = _sub_rms(o_t.T, g_ref[...], 1.0 - laminit_ref[...]).astype(BF16)


def _near_bias_tiles(rel_bias, blk):
    cidx = jnp.arange(blk, dtype=jnp.int32)[:, None]
    ridx = jnp.arange(blk, dtype=jnp.int32)[None, :]
    rel = (rel_bias - rel_bias[N_BUCKETS - 1][None, :]) * LOG2E

    def lookup(dist):
        onehot = (_rel_bucket(dist)[..., None] == jnp.arange(N_BUCKETS, dtype=jnp.int32)).astype(F32)
        return jnp.einsum("crb,bh->hcr", onehot, rel, precision=lax.Precision.HIGHEST)

    dist0 = ridx - cidx
    diag = jnp.where((dist0 >= 0)[None], lookup(jnp.maximum(dist0, 0)), NEG)
    prev = lookup(dist0 + blk)
    masked = jnp.full_like(diag, NEG)
    return jnp.stack([jnp.concatenate([diag, masked], axis=1), jnp.concatenate([prev, diag], axis=1)], axis=1)


def _attn_prompt(qa, kb, vb, lamv, laminit, near_bias, subln_g, batch, seq, blk):
    m = batch * seq
    nq = seq // blk
    assert blk >= NEAR_DIST - 1 and nq >= 2
    qspec = pl.BlockSpec((blk, 2 * D_HEAD_A), lambda b, h, i: (b * nq + i, h))
    kvspec = pl.BlockSpec((seq, 2 * D_HEAD_A), lambda b, h, i: (b, h))
    near = pl.BlockSpec((None, None, 2 * blk, blk), lambda b, h, i: (h, jnp.minimum(i, 1), 0, 0))
    const2 = lambda shape: pl.BlockSpec(shape, lambda b, h, i: (0, 0))
    per_map = lambda shape, dt: [pltpu.VMEM(shape, dt), pltpu.VMEM(shape, dt)]
    return pl.pallas_call(
        functools.partial(_attn_prompt_kernel, blk=blk, seq=seq),
        grid=(batch, N_HEADS, nq),
        in_specs=[const2((4, D_HEAD_A)), const2((1, 1)),
                  qspec, kvspec, kvspec, near, const2((1, 2 * D_HEAD_A))],
        out_specs=qspec,
        out_shape=jax.ShapeDtypeStruct((m, W_A), BF16),
        scratch_shapes=[pltpu.VMEM((V_AUG, seq), BF16)] + per_map((1, blk), F32) + per_map((V_AUG, blk), F32)
                       + per_map((FAR, blk, blk), F32) + per_map((FAR * blk, blk), BF16),
        compiler_params=_cparams("parallel", "parallel", "arbitrary"),
        name="attn_prompt",
    )(lamv, laminit, qa, kb, vb, near_bias, subln_g.reshape(1, -1))


def _attn_sample_kernel(pt_ref, lamv_ref, laminit_ref, q_ref, kn_ref, vn_ref, bfar_ref, blast_ref, bnew_ref, g_ref,
                        *refs, group, n_steps, rows):
    k_refs = refs[:group]
    v_refs = refs[group:2 * group]
    o_ref = refs[2 * group]
    m_sc, l_sc, acc_sc = refs[2 * group + 1:]
    s_id = pl.program_id(1)
    cols = k_refs[0].shape[0]

    q = q_ref[...]
    lane = lax.broadcasted_iota(jnp.int32, (1, 2 * D_HEAD_A), 1)
    lo = lane < D_HEAD_A
    zero = jnp.zeros((), BF16)
    wq = jnp.concatenate([jnp.where(lo, q, zero), jnp.where(lo, zero, q)], axis=0)

    def update(s, pv_of):
        m = m_sc[...]
        mn = jnp.maximum(m, jnp.max(s, axis=-1, keepdims=True))
        al = jnp.exp2(m - mn)
        p = jnp.exp2(s - mn)
        m_sc[...] = mn
        l_sc[...] = al * l_sc[...] + jnp.sum(p, axis=-1, keepdims=True)
        acc_sc[...] = al * acc_sc[...] + pv_of(p.astype(BF16))

    @pl.when(s_id == 0)
    def _():
        m_sc[...] = jnp.full(m_sc.shape, NEG, F32)
        l_sc[...] = jnp.zeros(l_sc.shape, F32)
        acc_sc[...] = jnp.zeros(acc_sc.shape, F32)
        update(_dot_nt(wq, kn_ref[...]) + bnew_ref[...], lambda p: _dot(p, vn_ref[...]))

    bfar = bfar_ref[...]
    parts = []
    for g in range(group):
        bias = bfar
        if g == group - 1:
            bias = jnp.where(s_id == n_steps - 1, blast_ref[...], bfar)
        parts.append(_dot_nt(wq, k_refs[g][...].astype(BF16)) + bias)

    def pv_pages(p):
        return sum(_dot(p[:, g * cols:(g + 1) * cols], v_refs[g][...].astype(BF16)) for g in range(group))

    update(jnp.concatenate(parts, axis=1), pv_pages)

    @pl.when(s_id == n_steps - 1)
    def _():
        lam = _lambda_full(lamv_ref, laminit_ref)
        o = acc_sc[...] / l_sc[...]
        od = o[0:rows] - lam * o[rows:2 * rows]
        o_ref[...] = _sub_rms(od, g_ref[...], 1.0 - laminit_ref[...]).astype(BF16)


def _sample_bias_tiles(rel_bias, t_new, n_pages, page):
    rows = t_new * N_HEADS
    cols = page * N_HEADS
    past = n_pages * page
    assert page + 1 >= NEAR_DIST
    rel = (rel_bias - rel_bias[N_BUCKETS - 1][None, :]) * LOG2E
    r = jnp.arange(2 * rows, dtype=jnp.int32)[:, None]
    rt, rh = (r % rows) // N_HEADS, r % N_HEADS
    rel_rows = rel[:, rh[:, 0]]

    def lookup(dist):
        onehot = (_rel_bucket(dist)[..., None] == jnp.arange(N_BUCKETS, dtype=jnp.int32)).astype(F32)
        return jnp.einsum("rcb,br->rc", onehot, rel_rows, precision=lax.Precision.HIGHEST)

    c = jnp.arange(cols, dtype=jnp.int32)[None, :]
    ct, ch = c // N_HEADS, c % N_HEADS
    same = rh == ch
    bfar = jnp.where(same, 0.0, NEG).astype(F32)
    dist_last = (past + rt) - ((n_pages - 1) * page + ct)
    blast = jnp.where(same, lookup(dist_last), NEG).astype(F32)
    cn = jnp.arange(rows, dtype=jnp.int32)[None, :]
    cnt, cnh = cn // N_HEADS, cn % N_HEADS
    dist_new = rt - cnt
    bnew = jnp.where((rh == cnh) & (dist_new >= 0), lookup(jnp.maximum(dist_new, 0)), NEG).astype(F32)
    return bfar, blast, bnew


def _attn_sample(qa, kb, vb, cache_k4, cache_v4, pt_flat, layer, lamv, laminit, bias_tiles, subln_g,
                 n_dec, t_new, n_pages, page, group):
    rows = t_new * N_HEADS
    cols = page * N_HEADS
    n_steps = n_pages // group
    bfar, blast, bnew = bias_tiles
    q3 = qa.reshape(n_dec, rows, 2 * D_HEAD_A)
    kn3 = kb.reshape(n_dec, rows, 2 * D_HEAD_A)
    vn3 = vb.reshape(n_dec, rows, 2 * D_HEAD_A)
    per_b = pl.BlockSpec((None, rows, 2 * D_HEAD_A), lambda b, s, pt: (b, 0, 0))
    const = lambda shape: pl.BlockSpec(shape, lambda b, s, pt: (0, 0))

    def page_spec(g):
        return pl.BlockSpec((None, None, cols, 2 * D_HEAD_A),
                            lambda b, s, pt: (layer, pt[b * n_pages + s * group + g], 0, 0))

    in_specs = [const((4, D_HEAD_A)), const((1, 1)), per_b, per_b, per_b,
                const((2 * rows, cols)), const((2 * rows, cols)), const((2 * rows, rows)),
                const((1, 2 * D_HEAD_A))]
    in_specs += [page_spec(g) for g in range(group)] * 2
    out = pl.pallas_call(
        functools.partial(_attn_sample_kernel, group=group, n_steps=n_steps, rows=rows),
        grid_spec=pltpu.PrefetchScalarGridSpec(
            num_scalar_prefetch=1,
            grid=(n_dec, n_steps),
            in_specs=in_specs,
            out_specs=per_b,
            scratch_shapes=[pltpu.VMEM((2 * rows, 1), F32), pltpu.VMEM((2 * rows, 1), F32),
                            pltpu.VMEM((2 * rows, 2 * D_HEAD_A), F32)],
        ),
        out_shape=jax.ShapeDtypeStruct((n_dec, rows, 2 * D_HEAD_A), BF16),
        compiler_params=_cparams("parallel", "arbitrary"),
        name="attn_sample",
    )(pt_flat, lamv, laminit, q3, kn3, vn3, bfar, blast, bnew, subln_g.reshape(1, -1),
      *([cache_k4] * group), *([cache_v4] * group))
    return out.reshape(n_dec * t_new, W_A)


def _ret_kernel(pb_ref, cos_ref, sin_ref, s0_ref, o_ref, sfin_ref, st_sc, *, chunk, n_valid, n_chunks):
    c_id = pl.program_id(1)
    lg = chunk.bit_length() - 1
    rows = N_HEADS * chunk
    masks = _head_masks()
    ones_bd = _head_ones()

    @pl.when(c_id == 0)
    def _():
        st_sc[...] = s0_ref[...]

    pb = pb_ref[...]
    q, k, v, g = pb[:, 0:W_BC], pb[:, W_BC:2 * W_BC], pb[:, 2 * W_BC:3 * W_BC], pb[:, 3 * W_BC:4 * W_BC]
    lane = lax.broadcasted_iota(jnp.int32, (1, W_BC), 1)
    first_half = (lane & (D_HEAD - 1)) < (D_HEAD // 2)
    cosf, sins = cos_ref[...], sin_ref[...]

    def rot(x):
        swapped = jnp.where(first_half, pltpu.roll(x, W_BC - D_HEAD // 2, 1), pltpu.roll(x, D_HEAD // 2, 1))
        return x * cosf + swapped * sins

    q = rot(q)
    k = rot(k) * (D_HEAD ** -0.5)
    row = lax.broadcasted_iota(jnp.int32, (chunk, 1), 0)
    if n_valid < chunk:
        valid = row < n_valid
        k = jnp.where(valid, k, 0.0)
        v = jnp.where(valid, v, 0.0)

    log_g = [math.log1p(-(2.0 ** (-5 - h))) for h in range(N_HEADS)]
    lg_lane = sum(jnp.where(m, log_g[h], 0.0) for h, m in enumerate(masks))
    ri = lax.broadcasted_iota(jnp.int32, (rows, rows), 0)
    ci = lax.broadcasted_iota(jnp.int32, (rows, rows), 1)
    rh = lax.broadcasted_iota(jnp.int32, (rows, 1), 0) >> lg
    rpos = (lax.broadcasted_iota(jnp.int32, (rows, 1), 0) & (chunk - 1)).astype(F32)
    lg_row = sum(jnp.where(rh == h, log_g[h], 0.0) for h in range(N_HEADS))
    diff = (ri & (chunk - 1)) - (ci & (chunk - 1))
    keep = ((ri >> lg) == (ci >> lg)) & (diff >= 0)
    dmask = jnp.where(keep, jnp.exp(lg_row * jnp.maximum(diff, 0).astype(F32)), 0.0)
    xi = jnp.exp(lg_row * (rpos + 1.0))
    rowf = row.astype(F32)
    zeta = jnp.exp(lg_lane * jnp.maximum(float(n_valid - 1) - rowf, 0.0))
    g_chunk = jnp.exp(lg_lane * float(n_valid))

    q_st = _stack_heads(q.astype(BF16), masks)
    k_st = _stack_heads(k.astype(BF16), masks)
    v_st = _stack_heads(v.astype(BF16), masks)
    kz_st = _stack_heads((k * zeta).astype(BF16), masks)
    state = st_sc[...]
    s_blk = _dot_nt(q_st, k_st) * dmask
    inner = _dot(s_blk.astype(BF16), v_st)
    cross = _dot(q_st, state.astype(BF16)) * xi
    o = _unstack_heads(inner + cross, chunk)
    st_new = state * g_chunk + _dot_tn(kz_st, v_st)
    st_sc[...] = st_new

    ms = _head_sum(o * o, ones_bd) * (1.0 / D_HEAD)
    o_ref[...] = (jax.nn.silu(g) * (o * lax.rsqrt(ms + RMS_EPS))).astype(BF16)

    @pl.when(c_id == n_chunks - 1)
    def _():
        sfin_ref[...] = st_new


def _retention(pb, cos_t, sin_t, s0_bd, batch, t_pad, chunk, n_valid):
    n_chunks = t_pad // chunk
    rowspec = lambda w: pl.BlockSpec((chunk, w), lambda b, c: (b * n_chunks + c, 0))
    stspec = pl.BlockSpec((None, W_BC, W_BC), lambda b, c: (b, 0, 0))
    return pl.pallas_call(
        functools.partial(_ret_kernel, chunk=chunk, n_valid=n_valid, n_chunks=n_chunks),
        grid=(batch, n_chunks),
        in_specs=[rowspec(PROJ_B), pl.BlockSpec((chunk, W_BC), lambda b, c: (c, 0)),
                  pl.BlockSpec((chunk, W_BC), lambda b, c: (c, 0)), stspec],
        out_specs=[rowspec(W_BC), stspec],
        out_shape=[jax.ShapeDtypeStruct((batch * t_pad, W_BC), BF16),
                   jax.ShapeDtypeStruct((batch, W_BC, W_BC), F32)],
        scratch_shapes=[pltpu.VMEM((W_BC, W_BC), F32)],
        compiler_params=_cparams("parallel", "arbitrary"),
        name="retention",
    )(pb, cos_t, sin_t, s0_bd)


def _rwkv_kernel(pc_ref, shift_ref, s0_ref, mu_ref, w0_ref, w2_ref, a0_ref, a2_ref, g2_ref, kk_ref, ka_ref,
                 rk_ref, lng_ref, lnb_ref, o_ref, sfin_ref, st_sc, prev_sc, *, chunk, n_sub, n_valid, n_steps):
    c_id = pl.program_id(1)
    lg = chunk.bit_length() - 1
    rows = N_HEADS * chunk
    tb = n_sub * chunk
    masks = _head_masks()
    ones_bd = _head_ones()

    @pl.when(c_id == 0)
    def _():
        st_sc[...] = s0_ref[...]
        prev_sc[...] = shift_ref[...]

    pc = pc_ref[...]
    row = lax.broadcasted_iota(jnp.int32, (tb, 1), 0)
    prev = jnp.where(row == 0, prev_sc[...], pltpu.roll(pc, 1, 0))
    prev_sc[...] = pc[tb - 1:tb, :]
    xm = pc + (prev - pc) * mu_ref[...]
    r, k, v, xt = xm[:, 0:W_BC], xm[:, W_BC:2 * W_BC], xm[:, 2 * W_BC:3 * W_BC], xm[:, 3 * W_BC:4 * W_BC]

    w_log = -_softplus(-(w0_ref[...] + _dot(jnp.tanh(xt).astype(BF16), w2_ref[...]))) - 0.5
    logw = -jnp.exp(w_log)
    a = jax.nn.sigmoid(a0_ref[...] + _dot(xt.astype(BF16), a2_ref[...]))
    g = _dot(jax.nn.sigmoid(xt).astype(BF16), g2_ref[...])

    kk = k * kk_ref[...]
    kk = kk / jnp.maximum(jnp.sqrt(_head_sum(kk * kk, ones_bd)), 1e-12)
    k2 = k * (1.0 + (a - 1.0) * ka_ref[...])
    if n_valid < chunk:
        valid = row < n_valid
        logw = jnp.where(valid, logw, 0.0)
        kk = jnp.where(valid, kk, 0.0)
        k2 = jnp.where(valid, k2, 0.0)
        v = jnp.where(valid, v, 0.0)

    ti = lax.broadcasted_iota(jnp.int32, (tb, tb), 0)
    tj = lax.broadcasted_iota(jnp.int32, (tb, tb), 1)
    same_chunk = (ti >> lg) == (tj >> lg)
    tri = jnp.where(same_chunk & (ti >= tj), 1.0, 0.0).astype(BF16)
    tot = jnp.where(same_chunk, 1.0, 0.0).astype(BF16)
    parts = _split3(logw)
    cum = sum(_dot(tri, part) for part in parts)
    cum_end = sum(_dot(tot, part) for part in parts)
    e_bwd = jnp.exp(-cum)
    e_end = jnp.exp(cum_end - cum)
    kka = kk * a
    a_t = -kk * jnp.exp(cum - logw)
    b_t = kka * e_bwd
    k_t = k2 * e_bwd
    r_t = r * jnp.exp(cum)
    b_h = kka * e_end
    k_h = k2 * e_end
    p_end = jnp.exp(cum_end)

    ri = lax.broadcasted_iota(jnp.int32, (rows, rows), 0)
    ci = lax.broadcasted_iota(jnp.int32, (rows, rows), 1)
    same = (ri >> lg) == (ci >> lg)
    dpos = (ri & (chunk - 1)) - (ci & (chunk - 1))
    strict = same & (dpos > 0)
    incl = same & (dpos >= 0)
    eye = jnp.where(ri == ci, 1.0, 0.0)

    def prepare(c):
        sl = slice(c * chunk, (c + 1) * chunk)
        st = lambda x: _stack_heads(x[sl].astype(BF16), masks)
        a_st, b_st, k_st, r_st, v_st, bh_st, kh_st = st(a_t), st(b_t), st(k_t), st(r_t), st(v), st(b_h), st(k_h)
        n_bd = jnp.where(strict, _dot_nt(a_st, b_st), 0.0)
        ak_bd = jnp.where(strict, _dot_nt(a_st, k_st), 0.0)
        m_rb = jnp.where(incl, _dot_nt(r_st, b_st), 0.0).astype(BF16)
        m_rk = jnp.where(incl, _dot_nt(r_st, k_st), 0.0).astype(BF16)
        t_inv = eye + n_bd
        pw = n_bd
        for _ in range(lg - 1):
            pwb = pw.astype(BF16)
            pw = _dot(pwb, pwb)
            t_inv = t_inv + _dot(t_inv.astype(BF16), pw.astype(BF16))
        tbf = t_inv.astype(BF16)
        w_st = _dot(tbf, a_st).astype(BF16)
        y_st = _dot(tbf, _dot(ak_bd.astype(BF16), v_st).astype(BF16))
        o_pre = _dot(m_rk, v_st)
        g_pre = _dot_tn(v_st, kh_st)
        return w_st, y_st, r_st, m_rb, o_pre, bh_st, g_pre, p_end[c * chunk:c * chunk + 1]

    prepared = [prepare(c) for c in range(n_sub)]

    state = st_sc[...]
    o_chunks = []
    for w_st, y_st, r_st, m_rb, o_pre, bh_st, g_pre, p_c in prepared:
        sb = state.astype(BF16)
        u = _dot_nt(w_st, sb) + y_st
        ub = u.astype(BF16)
        o_chunks.append(_unstack_heads(_dot_nt(r_st, sb) + _dot(m_rb, ub) + o_pre, chunk))
        state = state * p_c + _dot_tn(ub, bh_st) + g_pre
    st_sc[...] = state

    o = o_chunks[0] if n_sub == 1 else jnp.concatenate(o_chunks, axis=0)
    inv = 1.0 / D_HEAD
    mean = _head_sum(o, ones_bd) * inv
    d = o - mean
    var = _head_sum(d * d, ones_bd) * inv
    o_n = d * lax.rsqrt(var + RWKV_GN_EPS) * lng_ref[...] + lnb_ref[...]
    bonus = _head_sum(r * k2 * rk_ref[...], ones_bd) * v
    o_ref[...] = ((o_n + bonus) * g).astype(BF16)

    @pl.when(c_id == n_steps - 1)
    def _():
        sfin_ref[...] = state


def _rwkv(pc, shift0, s0_bd, params, batch, t_pad, chunk, n_sub, n_valid):
    tb = chunk * n_sub
    n_steps = t_pad // tb
    rowspec = lambda w: pl.BlockSpec((tb, w), lambda b, c: (b * n_steps + c, 0))
    stspec = pl.BlockSpec((None, W_BC, W_BC), lambda b, c: (b, 0, 0))
    const = lambda arr: pl.BlockSpec(arr.shape, lambda b, c: (0,) * arr.ndim)
    return pl.pallas_call(
        functools.partial(_rwkv_kernel, chunk=chunk, n_sub=n_sub, n_valid=n_valid, n_steps=n_steps),
        grid=(batch, n_steps),
        in_specs=[rowspec(PROJ_C), pl.BlockSpec((None, 1, PROJ_C), lambda b, c: (b, 0, 0)), stspec]
                 + [const(p) for p in params],
        out_specs=[rowspec(W_BC), stspec],
        out_shape=[jax.ShapeDtypeStruct((batch * t_pad, W_BC), BF16),
                   jax.ShapeDtypeStruct((batch, W_BC, W_BC), F32)],
        scratch_shapes=[pltpu.VMEM((W_BC, W_BC), F32), pltpu.VMEM((1, PROJ_C), F32)],
        compiler_params=_cparams("parallel", "arbitrary"),
        name="rwkv7",
    )(pc, shift0, s0_bd, *params)


def _out_proj_kernel(x_ref, a_ref, b_ref, c_ref, w_ref, g_ref, bb_ref, y_ref, *, alpha):
    acc = _dot(a_ref[...], w_ref[0:W_A, :])
    acc += _dot(b_ref[...], w_ref[W_A:W_A + W_BC, :])
    acc += _dot(c_ref[...], w_ref[W_A + W_BC:W_A + 2 * W_BC, :])
    y_ref[...] = _layer_norm(alpha * x_ref[...] + acc, g_ref[...], bb_ref[...])


def _out_proj(x2d, oa, ob, oc, w_bf, g, b, alpha, tm):
    m = x2d.shape[0]
    row = lambda w: pl.BlockSpec((tm, w), lambda i: (i, 0))
    const = lambda shape: pl.BlockSpec(shape, lambda i: (0, 0))
    return pl.pallas_call(
        functools.partial(_out_proj_kernel, alpha=alpha),
        grid=(m // tm,),
        in_specs=[row(D_MODEL), row(W_A), row(W_BC), row(W_BC), const((D_MODEL, D_MODEL)),
                  const((1, D_MODEL)), const((1, D_MODEL))],
        out_specs=row(D_MODEL),
        out_shape=jax.ShapeDtypeStruct((m, D_MODEL), F32),
        compiler_params=_cparams("parallel"),
        name="out_proj",
    )(x2d, oa, ob, oc, w_bf, g.reshape(1, -1), b.reshape(1, -1))


def _ffn_kernel(x_ref, w1_ref, w3_ref, w2_ref, g_ref, b_ref, y_ref, acc_sc, *, alpha):
    f = pl.program_id(1)
    x = x_ref[...]
    xb = x.astype(BF16)

    @pl.when(f == 0)
    def _():
        acc_sc[...] = jnp.zeros(acc_sc.shape, F32)

    h = jax.nn.silu(_dot(xb, w1_ref[...])) * _dot(xb, w3_ref[...])
    acc_sc[...] += _dot(h.astype(BF16), w2_ref[...])

    @pl.when(f == pl.num_programs(1) - 1)
    def _():
        y_ref[...] = _layer_norm(alpha * x + acc_sc[...], g_ref[...], b_ref[...])


def _ffn(x2d, w1, w3, w2, g, b, alpha, tm, tf):
    m = x2d.shape[0]
    d_ff = w1.shape[1]
    return pl.pallas_call(
        functools.partial(_ffn_kernel, alpha=alpha),
        grid=(m // tm, d_ff // tf),
        in_specs=[pl.BlockSpec((tm, D_MODEL), lambda i, f: (i, 0)),
                  pl.BlockSpec((D_MODEL, tf), lambda i, f: (0, f)),
                  pl.BlockSpec((D_MODEL, tf), lambda i, f: (0, f)),
                  pl.BlockSpec((tf, D_MODEL), lambda i, f: (f, 0)),
                  pl.BlockSpec((1, D_MODEL), lambda i, f: (0, 0)),
                  pl.BlockSpec((1, D_MODEL), lambda i, f: (0, 0))],
        out_specs=pl.BlockSpec((tm, D_MODEL), lambda i, f: (i, 0)),
        out_shape=jax.ShapeDtypeStruct((m, D_MODEL), F32),
        scratch_shapes=[pltpu.VMEM((tm, D_MODEL), F32)],
        compiler_params=_cparams("parallel", "arbitrary"),
        name="ffn",
    )(x2d, w1, w3, w2, g.reshape(1, -1), b.reshape(1, -1))


def _moe_kernel(x_ref, rh_ref, rl_ref, w1_ref, w3_ref, w2_ref, g_ref, b_ref, y_ref, acc_sc, gate_sc, xb_sc,
                *, alpha, n_experts):
    e = pl.program_id(1)
    lane = lax.broadcasted_iota(jnp.int32, (1, LANES), 1)

    @pl.when(e == 0)
    def _():
        x = x_ref[...]
        xh, xl = _split2(x)
        xb_sc[...] = xh
        logits = _dot(xh, rh_ref[...]) + _dot(xl, rh_ref[...]) + _dot(xh, rl_ref[...])
        logits = jnp.where(lane < n_experts, logits, NEG)
        v1 = jnp.max(logits, axis=-1, keepdims=True)
        i1 = jnp.min(jnp.where(logits == v1, lane, LANES), axis=-1, keepdims=True)
        rest = jnp.where(lane == i1, NEG, logits)
        v2 = jnp.max(rest, axis=-1, keepdims=True)
        i2 = jnp.min(jnp.where(rest == v2, lane, LANES), axis=-1, keepdims=True)
        ex = jnp.exp(v2 - v1)
        g1 = 1.0 / (1.0 + ex)
        g2 = ex / (1.0 + ex)
        gate_sc[...] = jnp.where(lane == i1, g1, 0.0) + jnp.where(lane == i2, g2, 0.0)
        acc_sc[...] = jnp.zeros(acc_sc.shape, F32)

    xb = xb_sc[...]
    h = jax.nn.silu(_dot(xb, w1_ref[...])) * _dot(xb, w3_ref[...])
    f = _dot(h.astype(BF16), w2_ref[...])
    ge = jnp.sum(jnp.where(lane == e, gate_sc[...], 0.0), axis=-1, keepdims=True)
    acc_sc[...] += ge * f

    @pl.when(e == n_experts - 1)
    def _():
        y_ref[...] = _layer_norm(alpha * x_ref[...] + acc_sc[...], g_ref[...], b_ref[...])


def _moe(x2d, router, w1, w3, w2, g, b, alpha, tm):
    m = x2d.shape[0]
    n_experts, _, d_e = w1.shape
    r_pad = jnp.zeros((D_MODEL, LANES), F32).at[:, :n_experts].set(router)
    r_hi = r_pad.astype(BF16)
    r_lo = (r_pad - r_hi.astype(F32)).astype(BF16)
    return pl.pallas_call(
        functools.partial(_moe_kernel, alpha=alpha, n_experts=n_experts),
        grid=(m // tm, n_experts),
        in_specs=[pl.BlockSpec((tm, D_MODEL), lambda i, e: (i, 0)),
                  pl.BlockSpec((D_MODEL, LANES), lambda i, e: (0, 0)),
                  pl.BlockSpec((D_MODEL, LANES), lambda i, e: (0, 0)),
                  pl.BlockSpec((None, D_MODEL, d_e), lambda i, e: (e, 0, 0)),
                  pl.BlockSpec((None, D_MODEL, d_e), lambda i, e: (e, 0, 0)),
                  pl.BlockSpec((None, d_e, D_MODEL), lambda i, e: (e, 0, 0)),
                  pl.BlockSpec((1, D_MODEL), lambda i, e: (0, 0)),
                  pl.BlockSpec((1, D_MODEL), lambda i, e: (0, 0))],
        out_specs=pl.BlockSpec((tm, D_MODEL), lambda i, e: (i, 0)),
        out_shape=jax.ShapeDtypeStruct((m, D_MODEL), F32),
        scratch_shapes=[pltpu.VMEM((tm, D_MODEL), F32), pltpu.VMEM((tm, LANES), F32),
                        pltpu.VMEM((tm, D_MODEL), BF16)],
        compiler_params=_cparams("parallel", "arbitrary"),
        name="moe",
    )(x2d, r_hi, r_lo, w1, w3, w2, g.reshape(1, -1), b.reshape(1, -1))


def _to_block_diag(s, transpose):
    if transpose:
        s = jnp.swapaxes(s, -1, -2)
    b = s.shape[0]
    eye = jnp.eye(N_HEADS, dtype=s.dtype)
    return jnp.einsum("bhij,hg->bhigj", s, eye).reshape(b, W_BC, W_BC)


def _from_block_diag(sbd, transpose):
    b = sbd.shape[0]
    s5 = sbd.reshape(b, N_HEADS, D_HEAD, N_HEADS, D_HEAD)
    s = jnp.stack([s5[:, h, :, h, :] for h in range(N_HEADS)], axis=1)
    return jnp.swapaxes(s, -1, -2) if transpose else s


def _rotary_tables(pos0, t_pad):
    half = D_HEAD // 2
    inv = 1.0 / (10000.0 ** (jnp.arange(half, dtype=F32) / half))
    ang = (pos0 + jnp.arange(t_pad, dtype=jnp.int32)).astype(F32)[:, None] * inv[None, :]
    cos, sin = jnp.cos(ang), jnp.sin(ang)
    cos_t = jnp.tile(jnp.concatenate([cos, cos], axis=-1), (1, N_HEADS))
    sin_t = jnp.tile(jnp.concatenate([-sin, sin], axis=-1), (1, N_HEADS))
    return cos_t, sin_t


def _pad_rows(lo, arr):
    return jnp.zeros((W_BC, W_BC), F32).at[lo:lo + arr.shape[0]].set(arr).astype(BF16)


def _ffn_tile(d_ff):
    best = LANES if d_ff % LANES == 0 else d_ff
    for t in range(LANES, min(d_ff, 1536) + 1, LANES):
        if d_ff % t == 0:
            best = t
    return best


def kernel(x_prompt, x_sample, cache_k, cache_v, page_table, state_ret, state_wkv, state_shift, rel_bias, w_in, w_out, lambda_q1, lambda_k1, lambda_q2, lambda_k2, subln_g, tshift_mu, decay_w0, decay_w2, iclr_a0, iclr_a2, gate_w2, k_k, k_a, r_k, lnx_g, lnx_b, ln1_g, ln1_b, ln2_g, ln2_b, ffn_w1, ffn_w3, ffn_w2, router_w, expert_w1, expert_w3, expert_w2):
    bp, seq, _ = x_prompt.shape
    n_dec, t_new, _ = x_sample.shape
    depth, n_pool, page, _, _ = cache_k.shape
    n_pages = page_table.shape[1]
    past = n_pages * page
    alpha = (2 * depth) ** 0.25

    mp, ms = bp * seq, n_dec * t_new
    tm_p = 256 if mp % 256 == 0 else mp
    tm_s = ms
    blk = 256 if seq % 256 == 0 else seq
    chunk_p = 64 if seq % 64 == 0 else seq
    sub_p = 4 if seq % (4 * chunk_p) == 0 else 1
    chunk_s = 32
    group = 8 if n_pages % 8 == 0 else 1

    cache_k4 = cache_k.reshape(depth, n_pool, page * N_HEADS, 2 * D_HEAD_A)
    cache_v4 = cache_v.reshape(depth, n_pool, page * N_HEADS, 2 * D_HEAD_A)
    pt_flat = page_table.reshape(-1).astype(jnp.int32)

    rel_bias = rel_bias.astype(F32)
    near_bias = _near_bias_tiles(rel_bias, blk)
    sample_tiles = _sample_bias_tiles(rel_bias, t_new, n_pages, page)
    cos_p, sin_p = _rotary_tables(0, seq)
    cos_s, sin_s = _rotary_tables(past, chunk_s)
    zero_bd_p = jnp.zeros((bp, W_BC, W_BC), F32)
    zero_shift_p = jnp.zeros((bp, 1, PROJ_C), F32)

    def pad_sample(a):
        w = a.shape[-1]
        a3 = a.reshape(n_dec, t_new, w)
        return jnp.pad(a3, ((0, 0), (0, chunk_s - t_new), (0, 0))).reshape(n_dec * chunk_s, w)

    def unpad_sample(a):
        w = a.shape[-1]
        return a.reshape(n_dec, chunk_s, w)[:, :t_new].reshape(n_dec * t_new, w)

    xp = x_prompt.reshape(mp, D_MODEL)
    xs = x_sample.reshape(ms, D_MODEL)
    kp_all = vp_all = ks_all = vs_all = None
    outs = {k: [] for k in ("rp", "rs", "wp", "ws", "sp", "ss")}
    for l in range(depth):
        w_in_bf = w_in[l].astype(BF16)
        w_out_bf = w_out[l].astype(BF16)
        lam_init = 0.8 - 0.6 * math.exp(-0.3 * l)
        lamv = jnp.stack([lambda_q1[l], lambda_k1[l], lambda_q2[l], lambda_k2[l]]).astype(F32)
        laminit = jnp.full((1, 1), lam_init, F32)
        row = lambda a: a.reshape(1, -1).astype(F32)
        rwkv_params = [row(tshift_mu[l]), row(decay_w0[l]), _pad_rows(0, decay_w2[l]), row(iclr_a0[l]),
                       _pad_rows(64, iclr_a2[l]), _pad_rows(128, gate_w2[l]),
                       row(k_k[l]), row(k_a[l]), row(r_k[l]), row(lnx_g[l]), row(lnx_b[l])]

        qa, kp_all, vp_all, kb, vb, pb, pc = _proj_in(xp, w_in_bf, kp_all, vp_all, l, depth, tm_p)
        oa = _attn_prompt(qa, kb, vb, lamv, laminit, near_bias, subln_g[l], bp, seq, blk)
        ob, ret_bd = _retention(pb, cos_p, sin_p, zero_bd_p, bp, seq, chunk_p, chunk_p)
        oc, wkv_bd = _rwkv(pc, zero_shift_p, zero_bd_p, rwkv_params, bp, seq, chunk_p, sub_p, chunk_p)
        xp = _out_proj(xp, oa, ob, oc, w_out_bf, ln1_g[l], ln1_b[l], alpha, tm_p)
        outs["rp"].append(_from_block_diag(ret_bd, False))
        outs["wp"].append(_from_block_diag(wkv_bd, False))
        outs["sp"].append(pc.reshape(bp, seq, PROJ_C)[:, -1])

        qa, ks_all, vs_all, kb, vb, pb, pc = _proj_in(xs, w_in_bf, ks_all, vs_all, l, depth, tm_s)
        oa = _attn_sample(qa, kb, vb, cache_k4, cache_v4, pt_flat, l, lamv, laminit, sample_tiles, subln_g[l],
                          n_dec, t_new, n_pages, page, group)
        ob, ret_bd = _retention(pad_sample(pb), cos_s, sin_s, _to_block_diag(state_ret[l].astype(F32), False),
                                n_dec, chunk_s, chunk_s, t_new)
        oc, wkv_bd = _rwkv(pad_sample(pc), state_shift[l].astype(F32).reshape(n_dec, 1, PROJ_C),
                           _to_block_diag(state_wkv[l].astype(F32), False), rwkv_params,
                           n_dec, chunk_s, chunk_s, 1, t_new)
        xs = _out_proj(xs, oa, unpad_sample(ob), unpad_sample(oc), w_out_bf, ln1_g[l], ln1_b[l], alpha, tm_s)
        outs["rs"].append(_from_block_diag(ret_bd, False))
        outs["ws"].append(_from_block_diag(wkv_bd, False))
        outs["ss"].append(pc.reshape(n_dec, t_new, PROJ_C)[:, -1])

        j = l // 2
        if l % 2 == 0:
            w1, w3, w2 = ffn_w1[j].astype(BF16), ffn_w3[j].astype(BF16), ffn_w2[j].astype(BF16)
            tf = _ffn_tile(w1.shape[1])
            xp = _ffn(xp, w1, w3, w2, ln2_g[l], ln2_b[l], alpha, min(512, mp), tf)
            xs = _ffn(xs, w1, w3, w2, ln2_g[l], ln2_b[l], alpha, tm_s, tf)
        else:
            w1, w3, w2 = expert_w1[j].astype(BF16), expert_w3[j].astype(BF16), expert_w2[j].astype(BF16)
            xp = _moe(xp, router_w[j], w1, w3, w2, ln2_g[l], ln2_b[l], alpha, min(512, mp))
            xs = _moe(xs, router_w[j], w1, w3, w2, ln2_g[l], ln2_b[l], alpha, tm_s)

    st = lambda key: jnp.stack(outs[key], axis=0)
    kv_p = lambda a: a.reshape(depth, bp, seq, N_HEADS, 2 * D_HEAD_A)
    kv_s = lambda a: a.reshape(depth, n_dec, t_new, N_HEADS, 2 * D_HEAD_A)
    return (xp.reshape(bp, seq, D_MODEL), xs.reshape(n_dec, t_new, D_MODEL),
            kv_p(kp_all), kv_p(vp_all), kv_s(ks_all), kv_s(vs_all),
            st("rp"), st("rs"), st("wp"), st("ws"), st("sp"), st("ss"))
```

```python
import functools
import math

import jax
import jax.numpy as jnp
from jax import lax
from jax.experimental import pallas as pl
from jax.experimental.pallas import tpu as pltpu

F32 = jnp.float32
BF16 = jnp.bfloat16

D_MODEL = 1024
N_HEADS = 4
D_HEAD_A = 64
W_A = N_HEADS * 2 * D_HEAD_A
D_HEAD = 64
W_BC = N_HEADS * D_HEAD
PROJ_A = 3 * W_A
PROJ_B = 4 * W_BC
PROJ_C = 1024
N_BUCKETS = 32
MAX_DISTANCE = 128
NEAR_DIST = 113
RWKV_GN_EPS = 64e-5
LN_EPS = 1e-5
RMS_EPS = 1e-5
NEG = -1e30
LOG2E = 1.4426950408889634

VMEM_LIMIT_BYTES = 56 * 1024 * 1024
LANES = 128


def _cparams(*sem):
    return pltpu.CompilerParams(dimension_semantics=sem, vmem_limit_bytes=VMEM_LIMIT_BYTES)


def _dot(a, b):
    return jnp.dot(a, b, preferred_element_type=F32)


def _dot_nt(a, b):
    return lax.dot_general(a, b, (((1,), (1,)), ((), ())), preferred_element_type=F32)


def _dot_tn(a, b):
    return lax.dot_general(a, b, (((0,), (0,)), ((), ())), preferred_element_type=F32)


def _split2(x):
    hi = x.astype(BF16)
    lo = (x - hi.astype(F32)).astype(BF16)
    return hi, lo


def _split3(x):
    hi = x.astype(BF16)
    r1 = x - hi.astype(F32)
    mid = r1.astype(BF16)
    lo = (r1 - mid.astype(F32)).astype(BF16)
    return hi, mid, lo


def _softplus(z):
    return jnp.maximum(z, 0.0) + jnp.log1p(jnp.exp(-jnp.abs(z)))


def _layer_norm(z, g, b):
    mu = jnp.mean(z, axis=-1, keepdims=True)
    d = z - mu
    var = jnp.mean(d * d, axis=-1, keepdims=True)
    return d * lax.rsqrt(var + LN_EPS) * g + b


def _head_masks():
    lane = lax.broadcasted_iota(jnp.int32, (1, W_BC), 1)
    return [(lane >= h * D_HEAD) & (lane < (h + 1) * D_HEAD) for h in range(N_HEADS)]


def _stack_heads(x, masks):
    zero = jnp.zeros((), x.dtype)
    return jnp.concatenate([jnp.where(m, x, zero) for m in masks], axis=0)


def _unstack_heads(z, c):
    return z[0:c] + z[c:2 * c] + z[2 * c:3 * c] + z[3 * c:4 * c]


def _load_block_diag(s_ref):
    rows = []
    for h in range(N_HEADS):
        pieces = [jnp.zeros((D_HEAD, D_HEAD), F32)] * N_HEADS
        pieces[h] = s_ref[h].astype(F32)
        rows.append(jnp.concatenate(pieces, axis=1))
    return jnp.concatenate(rows, axis=0)


def _store_block_diag(s_ref, state):
    for h in range(N_HEADS):
        s_ref[h] = state[h * D_HEAD:(h + 1) * D_HEAD, h * D_HEAD:(h + 1) * D_HEAD]


def _head_ones():
    r = lax.broadcasted_iota(jnp.int32, (W_BC, W_BC), 0)
    c = lax.broadcasted_iota(jnp.int32, (W_BC, W_BC), 1)
    return jnp.where((r // D_HEAD) == (c // D_HEAD), 1.0, 0.0).astype(BF16)


def _head_sum(x, ones_bd):
    hi, lo = _split2(x)
    return _dot(hi, ones_bd) + _dot(lo, ones_bd)


def _proj_in_kernel(*refs, aliased):
    x_ref, w_ref = refs[0], refs[1]
    qa_ref, k_ref, v_ref, kb_ref, vb_ref, pb_ref, pc_ref = refs[4:] if aliased else refs[2:]
    xb = x_ref[...].astype(BF16)

    def mm(lo, hi):
        return _dot(xb, w_ref[:, lo:hi])

    qa_ref[...] = (mm(0, W_A) * (D_HEAD_A ** -0.5 * LOG2E)).astype(BF16)
    k = mm(W_A, 2 * W_A)
    k_ref[...] = k
    kb_ref[...] = k.astype(BF16)
    v = mm(2 * W_A, 3 * W_A)
    v_ref[...] = v
    vb_ref[...] = v.astype(BF16)
    pb_ref[...] = mm(PROJ_A, PROJ_A + PROJ_B)
    pc_ref[...] = mm(PROJ_A + PROJ_B, PROJ_A + PROJ_B + PROJ_C)


def _proj_in(x2d, w_bf, k_all, v_all, layer, depth, tm):
    m = x2d.shape[0]
    n = w_bf.shape[1]
    aliased = k_all is not None
    row = lambda i: (i, 0)
    stack_spec = pl.BlockSpec((None, tm, W_A), lambda i: (layer, i, 0))
    in_specs = [pl.BlockSpec((tm, D_MODEL), row), pl.BlockSpec((D_MODEL, n), lambda i: (0, 0))]
    args = [x2d, w_bf]
    if aliased:
        in_specs += [pl.BlockSpec(memory_space=pl.ANY), pl.BlockSpec(memory_space=pl.ANY)]
        args += [k_all, v_all]
    return pl.pallas_call(
        functools.partial(_proj_in_kernel, aliased=aliased),
        grid=(m // tm,),
        in_specs=in_specs,
        out_specs=[pl.BlockSpec((tm, W_A), row), stack_spec, stack_spec,
                   pl.BlockSpec((tm, W_A), row), pl.BlockSpec((tm, W_A), row),
                   pl.BlockSpec((tm, PROJ_B), row), pl.BlockSpec((tm, PROJ_C), row)],
        out_shape=[jax.ShapeDtypeStruct((m, W_A), BF16), jax.ShapeDtypeStruct((depth, m, W_A), F32),
                   jax.ShapeDtypeStruct((depth, m, W_A), F32), jax.ShapeDtypeStruct((m, W_A), BF16),
                   jax.ShapeDtypeStruct((m, W_A), BF16), jax.ShapeDtypeStruct((m, PROJ_B), F32),
                   jax.ShapeDtypeStruct((m, PROJ_C), F32)],
        input_output_aliases={2: 1, 3: 2} if aliased else {},
        compiler_params=_cparams("parallel"),
        name="proj_in",
    )(*args)


def _lambda_full(lamv_ref, laminit_ref):
    lv = lamv_ref[...]
    s1 = jnp.sum(lv[0:1] * lv[1:2], axis=-1, keepdims=True)
    s2 = jnp.sum(lv[2:3] * lv[3:4], axis=-1, keepdims=True)
    return jnp.exp(s1) - jnp.exp(s2) + laminit_ref[...]


def _sub_rms(o, g, scale):
    y = o * lax.rsqrt(jnp.mean(o * o, axis=-1, keepdims=True) + RMS_EPS)
    return y * g * scale


def _rel_bucket(n):
    max_exact = N_BUCKETS // 2
    large = max_exact + (jnp.log(jnp.maximum(n, 1).astype(F32) / max_exact)
                         / math.log(MAX_DISTANCE / max_exact) * (N_BUCKETS - max_exact)).astype(jnp.int32)
    large = jnp.minimum(large, N_BUCKETS - 1)
    return jnp.where(n < max_exact, n, large)


V_AUG = 2 * D_HEAD_A + 16
PAIR = 2


def _attn_prompt_kernel(lamv_ref, laminit_ref, q_ref, k_ref, v_ref, near_ref, g_ref, o_ref,
                        vt_sc, m0_sc, m1_sc, acc0_sc, acc1_sc, sa0_sc, sa1_sc, sb0_sc, sb1_sc,
                        ca0_sc, ca1_sc, cb0_sc, cb1_sc, *, blk, seq):
    i = pl.program_id(2)
    dv = 2 * D_HEAD_A
    kp = PAIR * blk
    m_sc, acc_sc = (m0_sc, m1_sc), (acc0_sc, acc1_sc)
    s_sc = ((sa0_sc, sa1_sc), (sb0_sc, sb1_sc))
    cm_sc = ((ca0_sc, ca1_sc), (cb0_sc, cb1_sc))

    @pl.when(i == 0)
    def _():
        for c in range(seq // blk):
            vt = v_ref[c * blk:(c + 1) * blk, :].astype(F32).T
            vt_sc[0:dv, c * blk:(c + 1) * blk] = vt.astype(BF16)
        vt_sc[dv:V_AUG, :] = jnp.ones((V_AUG - dv, seq), BF16)

    q = q_ref[...]
    lane = lax.broadcasted_iota(jnp.int32, (1, dv), 1)
    lo = lane < D_HEAD_A
    zero = jnp.zeros((), BF16)
    qmaps = (jnp.where(lo, q, zero), jnp.where(lo, zero, q))
    for mp in range(2):
        m_sc[mp][...] = jnp.full(m_sc[mp].shape, NEG, F32)
        acc_sc[mp][...] = jnp.zeros(acc_sc[mp].shape, F32)

    def stage_a(pair, slot, bias_ref):
        off = pl.multiple_of(pair * kp, kp)
        cmax = [None, None]
        for sub in range(PAIR):
            kblk = k_ref[pl.ds(off + sub * blk, blk), :]
            for mp in range(2):
                s = _dot_nt(kblk, qmaps[mp])
                if bias_ref is not None:
                    s = s + bias_ref[sub * blk:(sub + 1) * blk, :]
                s_sc[slot][mp][sub] = s
                smax = jnp.max(s, axis=0, keepdims=True)
                cmax[mp] = smax if cmax[mp] is None else jnp.maximum(cmax[mp], smax)
        for mp in range(2):
            cm_sc[slot][mp][...] = cmax[mp]

    def stage_b(pair, slot):
        off = pl.multiple_of(pair * kp, kp)
        vtb = vt_sc[:, pl.ds(off, kp)]
        for mp in range(2):
            m_old = m_sc[mp][...]
            mn = jnp.maximum(m_old, cm_sc[slot][mp][...])
            p = jnp.concatenate([jnp.exp2(s_sc[slot][mp][sub] - mn).astype(BF16) for sub in range(PAIR)], axis=0)
            m_sc[mp][...] = mn
            acc_sc[mp][...] = jnp.exp2(m_old - mn) * acc_sc[mp][...] + _dot(vtb, p)

    n_pairs = (i + 2) // 2
    t1 = n_pairs - 1
    t0 = jnp.maximum(n_pairs - 2, 0)
    n_far = t0
    stage_a(t1, 0, near_ref.at[1])
    stage_a(t0, 1, near_ref.at[0])
    stage_b(t1, 0)

    def far_body(j, carry):
        stage_a(2 * j, 0, None)
        stage_b(jnp.where(j == 0, t0, 2 * j - 1), 1)
        stage_a(2 * j + 1, 1, None)
        stage_b(2 * j, 0)
        return carry

    lax.fori_loop(0, n_far // 2, far_body, 0)
    pending = jnp.where(n_far < 2, t0, ((n_far // 2) * 2) - 1)

    @pl.when(n_far % 2 == 1)
    def _():
        stage_a(n_far - 1, 0, None)
        stage_b(pending, 1)
        stage_b(n_far - 1, 0)

    @pl.when(n_far % 2 == 0)
    def _():
        stage_b(pending, 1)

    lam = _lambda_full(lamv_ref, laminit_ref)
    a1, a2 = acc0_sc[...], acc1_sc[...]
    o_t = a1[0:dv] / a1[dv:dv + 1] - lam * (a2[0:dv] / a2[dv:dv + 1])
    o_ref[...] = _sub_rms(o_t.T, g_ref[...], 1.0 - laminit_ref[...]).astype(BF16)


def _near_bias_tiles(rel_bias, blk):
    cidx = jnp.arange(blk, dtype=jnp.int32)[:, None]
    ridx = jnp.arange(blk, dtype=jnp.int32)[None, :]
    rel = (rel_bias - rel_bias[N_BUCKETS - 1][None, :]) * LOG2E

    def lookup(dist):
        onehot = (_rel_bucket(dist)[..., None] == jnp.arange(N_BUCKETS, dtype=jnp.int32)).astype(F32)
        return jnp.einsum("crb,bh->hcr", onehot, rel, precision=lax.Precision.HIGHEST)

    dist0 = ridx - cidx
    diag = jnp.where((dist0 >= 0)[None], lookup(jnp.maximum(dist0, 0)), NEG)
    prev = lookup(dist0 + blk)
    masked = jnp.full_like(diag, NEG)
    zeros = jnp.zeros_like(diag)
    cat = lambda a, b: jnp.concatenate([a, b], axis=1)
    variants = [(cat(masked, masked), cat(diag, masked)), (cat(masked, masked), cat(prev, diag)),
                (cat(zeros, prev), cat(diag, masked)), (cat(zeros, zeros), cat(prev, diag))]
    return jnp.stack([jnp.stack(v, axis=1) for v in variants], axis=1)


def _attn_prompt(qa, kb, vb, lamv, laminit, near_bias, subln_g, batch, seq, blk):
    m = batch * seq
    nq = seq // blk
    assert blk >= NEAR_DIST - 1 and nq % PAIR == 0
    qspec = pl.BlockSpec((blk, 2 * D_HEAD_A), lambda b, h, i: (b * nq + i, h))
    kvspec = pl.BlockSpec((seq, 2 * D_HEAD_A), lambda b, h, i: (b, h))
    near = pl.BlockSpec((None, None, 2, PAIR * blk, blk),
                        lambda b, h, i: (h, jnp.where(i < 2, i, 2 + (i & 1)), 0, 0, 0))
    const2 = lambda shape: pl.BlockSpec(shape, lambda b, h, i: (0, 0))
    per_map = lambda shape, dt: [pltpu.VMEM(shape, dt), pltpu.VMEM(shape, dt)]
    return pl.pallas_call(
        functools.partial(_attn_prompt_kernel, blk=blk, seq=seq),
        grid=(batch, N_HEADS, nq),
        in_specs=[const2((4, D_HEAD_A)), const2((1, 1)),
                  qspec, kvspec, kvspec, near, const2((1, 2 * D_HEAD_A))],
        out_specs=qspec,
        out_shape=jax.ShapeDtypeStruct((m, W_A), BF16),
        scratch_shapes=[pltpu.VMEM((V_AUG, seq), BF16)] + per_map((1, blk), F32) + per_map((V_AUG, blk), F32)
                       + 2 * per_map((PAIR, blk, blk), F32) + 2 * per_map((1, blk), F32),
        compiler_params=_cparams("parallel", "parallel", "arbitrary"),
        name="attn_prompt",
    )(lamv, laminit, qa, kb, vb, near_bias, subln_g.reshape(1, -1))


def _attn_sample_kernel(pt_ref, lamv_ref, laminit_ref, q_ref, kn_ref, vn_ref, bfar_ref, blast_ref, bnew_ref, g_ref,
                        *refs, group, n_steps, rows):
    k_refs = refs[:group]
    v_refs = refs[group:2 * group]
    o_ref = refs[2 * group]
    m_sc, l_sc, acc_sc = refs[2 * group + 1:]
    s_id = pl.program_id(1)
    cols = k_refs[0].shape[0]

    q = q_ref[...]
    lane = lax.broadcasted_iota(jnp.int32, (1, 2 * D_HEAD_A), 1)
    lo = lane < D_HEAD_A
    zero = jnp.zeros((), BF16)
    wq = jnp.concatenate([jnp.where(lo, q, zero), jnp.where(lo, zero, q)], axis=0)

    def update(s, pv_of):
        m = m_sc[...]
        mn = jnp.maximum(m, jnp.max(s, axis=-1, keepdims=True))
        al = jnp.exp2(m - mn)
        p = jnp.exp2(s - mn)
        m_sc[...] = mn
        l_sc[...] = al * l_sc[...] + jnp.sum(p, axis=-1, keepdims=True)
        acc_sc[...] = al * acc_sc[...] + pv_of(p.astype(BF16))

    @pl.when(s_id == 0)
    def _():
        m_sc[...] = jnp.full(m_sc.shape, NEG, F32)
        l_sc[...] = jnp.zeros(l_sc.shape, F32)
        acc_sc[...] = jnp.zeros(acc_sc.shape, F32)
        update(_dot_nt(wq, kn_ref[...]) + bnew_ref[...], lambda p: _dot(p, vn_ref[...]))

    bfar = bfar_ref[...]
    parts = []
    for g in range(group):
        bias = bfar
        if g == group - 1:
            bias = jnp.where(s_id == n_steps - 1, blast_ref[...], bfar)
        parts.append(_dot_nt(wq, k_refs[g][...].astype(BF16)) + bias)

    def pv_pages(p):
        return sum(_dot(p[:, g * cols:(g + 1) * cols], v_refs[g][...].astype(BF16)) for g in range(group))

    update(jnp.concatenate(parts, axis=1), pv_pages)

    @pl.when(s_id == n_steps - 1)
    def _():
        lam = _lambda_full(lamv_ref, laminit_ref)
        o = acc_sc[...] / l_sc[...]
        od = o[0:rows] - lam * o[rows:2 * rows]
        o_ref[...] = _sub_rms(od, g_ref[...], 1.0 - laminit_ref[...]).astype(BF16)


def _sample_bias_tiles(rel_bias, t_new, n_pages, page):
    rows = t_new * N_HEADS
    cols = page * N_HEADS
    past = n_pages * page
    assert page + 1 >= NEAR_DIST
    rel = (rel_bias - rel_bias[N_BUCKETS - 1][None, :]) * LOG2E
    r = jnp.arange(2 * rows, dtype=jnp.int32)[:, None]
    rt, rh = (r % rows) // N_HEADS, r % N_HEADS
    rel_rows = rel[:, rh[:, 0]]

    def lookup(dist):
        onehot = (_rel_bucket(dist)[..., None] == jnp.arange(N_BUCKETS, dtype=jnp.int32)).astype(F32)
        return jnp.einsum("rcb,br->rc", onehot, rel_rows, precision=lax.Precision.HIGHEST)

    c = jnp.arange(cols, dtype=jnp.int32)[None, :]
    ct, ch = c // N_HEADS, c % N_HEADS
    same = rh == ch
    bfar = jnp.where(same, 0.0, NEG).astype(F32)
    dist_last = (past + rt) - ((n_pages - 1) * page + ct)
    blast = jnp.where(same, lookup(dist_last), NEG).astype(F32)
    cn = jnp.arange(rows, dtype=jnp.int32)[None, :]
    cnt, cnh = cn // N_HEADS, cn % N_HEADS
    dist_new = rt - cnt
    bnew = jnp.where((rh == cnh) & (dist_new >= 0), lookup(jnp.maximum(dist_new, 0)), NEG).astype(F32)
    return bfar, blast, bnew


def _attn_sample(qa, kb, vb, cache_k4, cache_v4, pt_flat, layer, lamv, laminit, bias_tiles, subln_g,
                 n_dec, t_new, n_pages, page, group):
    rows = t_new * N_HEADS
    cols = page * N_HEADS
    n_steps = n_pages // group
    bfar, blast, bnew = bias_tiles
    q3 = qa.reshape(n_dec, rows, 2 * D_HEAD_A)
    kn3 = kb.reshape(n_dec, rows, 2 * D_HEAD_A)
    vn3 = vb.reshape(n_dec, rows, 2 * D_HEAD_A)
    per_b = pl.BlockSpec((None, rows, 2 * D_HEAD_A), lambda b, s, pt: (b, 0, 0))
    const = lambda shape: pl.BlockSpec(shape, lambda b, s, pt: (0, 0))

    def page_spec(g):
        return pl.BlockSpec((None, None, cols, 2 * D_HEAD_A),
                            lambda b, s, pt: (layer, pt[b * n_pages + s * group + g], 0, 0))

    in_specs = [const((4, D_HEAD_A)), const((1, 1)), per_b, per_b, per_b,
                const((2 * rows, cols)), const((2 * rows, cols)), const((2 * rows, rows)),
                const((1, 2 * D_HEAD_A))]
    in_specs += [page_spec(g) for g in range(group)] * 2
    out = pl.pallas_call(
        functools.partial(_attn_sample_kernel, group=group, n_steps=n_steps, rows=rows),
        grid_spec=pltpu.PrefetchScalarGridSpec(
            num_scalar_prefetch=1,
            grid=(n_dec, n_steps),
            in_specs=in_specs,
            out_specs=per_b,
            scratch_shapes=[pltpu.VMEM((2 * rows, 1), F32), pltpu.VMEM((2 * rows, 1), F32),
                            pltpu.VMEM((2 * rows, 2 * D_HEAD_A), F32)],
        ),
        out_shape=jax.ShapeDtypeStruct((n_dec, rows, 2 * D_HEAD_A), BF16),
        compiler_params=_cparams("parallel", "arbitrary"),
        name="attn_sample",
    )(pt_flat, lamv, laminit, q3, kn3, vn3, bfar, blast, bnew, subln_g.reshape(1, -1),
      *([cache_k4] * group), *([cache_v4] * group))
    return out.reshape(n_dec * t_new, W_A)


def _ret_kernel(pb_ref, cos_ref, sin_ref, s0_ref, o_ref, sfin_ref, st_sc, *, chunk, n_sub, n_valid, n_steps):
    c_id = pl.program_id(1)
    lg = chunk.bit_length() - 1
    rows = N_HEADS * chunk
    tb = n_sub * chunk
    masks = _head_masks()
    ones_bd = _head_ones()

    @pl.when(c_id == 0)
    def _():
        st_sc[...] = _load_block_diag(s0_ref)

    pb = pb_ref[...]
    q, k, v, g = pb[:, 0:W_BC], pb[:, W_BC:2 * W_BC], pb[:, 2 * W_BC:3 * W_BC], pb[:, 3 * W_BC:4 * W_BC]
    lane = lax.broadcasted_iota(jnp.int32, (1, W_BC), 1)
    first_half = (lane & (D_HEAD - 1)) < (D_HEAD // 2)
    cosf, sins = cos_ref[...], sin_ref[...]

    def rot(x):
        swapped = jnp.where(first_half, pltpu.roll(x, W_BC - D_HEAD // 2, 1), pltpu.roll(x, D_HEAD // 2, 1))
        return x * cosf + swapped * sins

    q = rot(q)
    k = rot(k) * (D_HEAD ** -0.5)
    row = lax.broadcasted_iota(jnp.int32, (tb, 1), 0)
    if n_valid < chunk:
        valid = row < n_valid
        k = jnp.where(valid, k, 0.0)
        v = jnp.where(valid, v, 0.0)

    log_g = [math.log1p(-(2.0 ** (-5 - h))) for h in range(N_HEADS)]
    lg_lane = sum(jnp.where(m, log_g[h], 0.0) for h, m in enumerate(masks))
    ri = lax.broadcasted_iota(jnp.int32, (rows, rows), 0)
    ci = lax.broadcasted_iota(jnp.int32, (rows, rows), 1)
    rh = lax.broadcasted_iota(jnp.int32, (rows, 1), 0) >> lg
    rpos = (lax.broadcasted_iota(jnp.int32, (rows, 1), 0) & (chunk - 1)).astype(F32)
    lg_row = sum(jnp.where(rh == h, log_g[h], 0.0) for h in range(N_HEADS))
    diff = (ri & (chunk - 1)) - (ci & (chunk - 1))
    keep = ((ri >> lg) == (ci >> lg)) & (diff >= 0)
    dmask = jnp.where(keep, jnp.exp(lg_row * jnp.maximum(diff, 0).astype(F32)), 0.0)
    xi = jnp.exp(lg_row * (rpos + 1.0))
    posf = (row & (chunk - 1)).astype(F32)
    zeta = jnp.exp(lg_lane * jnp.maximum(float(n_valid - 1) - posf, 0.0))
    g_chunk = jnp.exp(lg_lane * float(n_valid))
    qb, kb, vb, kzb = q.astype(BF16), k.astype(BF16), v.astype(BF16), (k * zeta).astype(BF16)

    def prepare(c):
        sl = slice(c * chunk, (c + 1) * chunk)
        q_st, k_st, v_st = (_stack_heads(x[sl], masks) for x in (qb, kb, vb))
        s_blk = _dot_nt(q_st, k_st) * dmask
        return q_st, _dot(s_blk.astype(BF16), v_st), _dot_tn(_stack_heads(kzb[sl], masks), v_st)

    prepared = [prepare(c) for c in range(n_sub)]
    state = st_sc[...]
    o_chunks = []
    for q_st, inner, incr in prepared:
        cross = _dot(q_st, state.astype(BF16)) * xi
        o_chunks.append(_unstack_heads(inner + cross, chunk))
        state = state * g_chunk + incr
    st_sc[...] = state

    o = o_chunks[0] if n_sub == 1 else jnp.concatenate(o_chunks, axis=0)
    ms = _head_sum(o * o, ones_bd) * (1.0 / D_HEAD)
    o_ref[...] = (jax.nn.silu(g) * (o * lax.rsqrt(ms + RMS_EPS))).astype(BF16)

    @pl.when(c_id == n_steps - 1)
    def _():
        _store_block_diag(sfin_ref, state)


def _retention(pb, cos_t, sin_t, s0, batch, t_pad, chunk, n_sub, n_valid):
    tb = chunk * n_sub
    n_steps = t_pad // tb
    rowspec = lambda w: pl.BlockSpec((tb, w), lambda b, c: (b * n_steps + c, 0))
    stspec = pl.BlockSpec((None, N_HEADS, D_HEAD, D_HEAD), lambda b, c: (b, 0, 0, 0))
    return pl.pallas_call(
        functools.partial(_ret_kernel, chunk=chunk, n_sub=n_sub, n_valid=n_valid, n_steps=n_steps),
        grid=(batch, n_steps),
        in_specs=[rowspec(PROJ_B), pl.BlockSpec((tb, W_BC), lambda b, c: (c, 0)),
                  pl.BlockSpec((tb, W_BC), lambda b, c: (c, 0)), stspec],
        out_specs=[rowspec(W_BC), stspec],
        out_shape=[jax.ShapeDtypeStruct((batch * t_pad, W_BC), BF16),
                   jax.ShapeDtypeStruct((batch, N_HEADS, D_HEAD, D_HEAD), F32)],
        scratch_shapes=[pltpu.VMEM((W_BC, W_BC), F32)],
        compiler_params=_cparams("parallel", "arbitrary"),
        name="retention",
    )(pb, cos_t, sin_t, s0)


def _rwkv_kernel(pc_ref, shift_ref, s0_ref, mu_ref, w0_ref, w2_ref, a0_ref, a2_ref, g2_ref, kk_ref, ka_ref,
                 rk_ref, lng_ref, lnb_ref, o_ref, sfin_ref, st_sc, prev_sc, *, chunk, n_sub, n_valid, n_steps):
    c_id = pl.program_id(1)
    lg = chunk.bit_length() - 1
    rows = N_HEADS * chunk
    tb = n_sub * chunk
    masks = _head_masks()
    ones_bd = _head_ones()

    @pl.when(c_id == 0)
    def _():
        st_sc[...] = _load_block_diag(s0_ref)
        prev_sc[...] = shift_ref[...]

    pc = pc_ref[...]
    row = lax.broadcasted_iota(jnp.int32, (tb, 1), 0)
    prev = jnp.where(row == 0, prev_sc[...], pltpu.roll(pc, 1, 0))
    prev_sc[...] = pc[tb - 1:tb, :]
    xm = pc + (prev - pc) * mu_ref[...]
    r, k, v, xt = xm[:, 0:W_BC], xm[:, W_BC:2 * W_BC], xm[:, 2 * W_BC:3 * W_BC], xm[:, 3 * W_BC:4 * W_BC]

    w_log = -_softplus(-(w0_ref[...] + _dot(jnp.tanh(xt).astype(BF16), w2_ref[...]))) - 0.5
    logw = -jnp.exp(w_log)
    a = jax.nn.sigmoid(a0_ref[...] + _dot(xt.astype(BF16), a2_ref[...]))
    g = _dot(jax.nn.sigmoid(xt).astype(BF16), g2_ref[...])

    kk = k * kk_ref[...]
    kk = kk / jnp.maximum(jnp.sqrt(_head_sum(kk * kk, ones_bd)), 1e-12)
    k2 = k * (1.0 + (a - 1.0) * ka_ref[...])
    if n_valid < chunk:
        valid = row < n_valid
        logw = jnp.where(valid, logw, 0.0)
        kk = jnp.where(valid, kk, 0.0)
        k2 = jnp.where(valid, k2, 0.0)
        v = jnp.where(valid, v, 0.0)

    ti = lax.broadcasted_iota(jnp.int32, (tb, tb), 0)
    tj = lax.broadcasted_iota(jnp.int32, (tb, tb), 1)
    same_chunk = (ti >> lg) == (tj >> lg)
    tri = jnp.where(same_chunk & (ti >= tj), 1.0, 0.0).astype(BF16)
    tot = jnp.where(same_chunk, 1.0, 0.0).astype(BF16)
    parts = _split3(logw)
    cum = sum(_dot(tri, part) for part in parts)
    cum_end = sum(_dot(tot, part) for part in parts)
    e_bwd = jnp.exp(-cum)
    e_end = jnp.exp(cum_end - cum)
    kka = kk * a
    a_t = -kk * jnp.exp(cum - logw)
    b_t = kka * e_bwd
    k_t = k2 * e_bwd
    r_t = r * jnp.exp(cum)
    b_h = kka * e_end
    k_h = k2 * e_end
    p_end = jnp.exp(cum_end)

    ri = lax.broadcasted_iota(jnp.int32, (rows, rows), 0)
    ci = lax.broadcasted_iota(jnp.int32, (rows, rows), 1)
    same = (ri >> lg) == (ci >> lg)
    dpos = (ri & (chunk - 1)) - (ci & (chunk - 1))
    strict = same & (dpos > 0)
    incl = same & (dpos >= 0)
    eye = jnp.where(ri == ci, 1.0, 0.0)

    def prepare(c):
        sl = slice(c * chunk, (c + 1) * chunk)
        st = lambda x: _stack_heads(x[sl].astype(BF16), masks)
        a_st, b_st, k_st, r_st, v_st, bh_st, kh_st = st(a_t), st(b_t), st(k_t), st(r_t), st(v), st(b_h), st(k_h)
        n_bd = jnp.where(strict, _dot_nt(a_st, b_st), 0.0)
        ak_bd = jnp.where(strict, _dot_nt(a_st, k_st), 0.0)
        m_rb = jnp.where(incl, _dot_nt(r_st, b_st), 0.0).astype(BF16)
        m_rk = jnp.where(incl, _dot_nt(r_st, k_st), 0.0).astype(BF16)
        t_inv = eye + n_bd
        pw = n_bd
        for _ in range(lg - 1):
            pwb = pw.astype(BF16)
            pw = _dot(pwb, pwb)
            t_inv = t_inv + _dot(t_inv.astype(BF16), pw.astype(BF16))
        tbf = t_inv.astype(BF16)
        w_st = _dot(tbf, a_st).astype(BF16)
        y_st = _dot(tbf, _dot(ak_bd.astype(BF16), v_st).astype(BF16))
        o_pre = _dot(m_rk, v_st)
        g_pre = _dot_tn(v_st, kh_st)
        return w_st, y_st, r_st, m_rb, o_pre, bh_st, g_pre, p_end[c * chunk:c * chunk + 1]

    prepared = [prepare(c) for c in range(n_sub)]

    state = st_sc[...]
    o_chunks = []
    for w_st, y_st, r_st, m_rb, o_pre, bh_st, g_pre, p_c in prepared:
        sb = state.astype(BF16)
        u = _dot_nt(w_st, sb) + y_st
        ub = u.astype(BF16)
        o_chunks.append(_unstack_heads(_dot_nt(r_st, sb) + _dot(m_rb, ub) + o_pre, chunk))
        state = state * p_c + _dot_tn(ub, bh_st) + g_pre
    st_sc[...] = state

    o = o_chunks[0] if n_sub == 1 else jnp.concatenate(o_chunks, axis=0)
    inv = 1.0 / D_HEAD
    mean = _head_sum(o, ones_bd) * inv
    d = o - mean
    var = _head_sum(d * d, ones_bd) * inv
    o_n = d * lax.rsqrt(var + RWKV_GN_EPS) * lng_ref[...] + lnb_ref[...]
    bonus = _head_sum(r * k2 * rk_ref[...], ones_bd) * v
    o_ref[...] = ((o_n + bonus) * g).astype(BF16)

    @pl.when(c_id == n_steps - 1)
    def _():
        _store_block_diag(sfin_ref, state)


def _rwkv(pc, shift0, s0, params, batch, t_pad, chunk, n_sub, n_valid):
    tb = chunk * n_sub
    n_steps = t_pad // tb
    rowspec = lambda w: pl.BlockSpec((tb, w), lambda b, c: (b * n_steps + c, 0))
    stspec = pl.BlockSpec((None, N_HEADS, D_HEAD, D_HEAD), lambda b, c: (b, 0, 0, 0))
    const = lambda arr: pl.BlockSpec(arr.shape, lambda b, c: (0,) * arr.ndim)
    return pl.pallas_call(
        functools.partial(_rwkv_kernel, chunk=chunk, n_sub=n_sub, n_valid=n_valid, n_steps=n_steps),
        grid=(batch, n_steps),
        in_specs=[rowspec(PROJ_C), pl.BlockSpec((None, 1, PROJ_C), lambda b, c: (b, 0, 0)), stspec]
                 + [const(p) for p in params],
        out_specs=[rowspec(W_BC), stspec],
        out_shape=[jax.ShapeDtypeStruct((batch * t_pad, W_BC), BF16),
                   jax.ShapeDtypeStruct((batch, N_HEADS, D_HEAD, D_HEAD), F32)],
        scratch_shapes=[pltpu.VMEM((W_BC, W_BC), F32), pltpu.VMEM((1, PROJ_C), F32)],
        compiler_params=_cparams("parallel", "arbitrary"),
        name="rwkv7",
    )(pc, shift0, s0, *params)


def _out_proj_kernel(x_ref, a_ref, b_ref, c_ref, w_ref, g_ref, bb_ref, y_ref, *, alpha):
    acc = _dot(a_ref[...], w_ref[0:W_A, :])
    acc += _dot(b_ref[...], w_ref[W_A:W_A + W_BC, :])
    acc += _dot(c_ref[...], w_ref[W_A + W_BC:W_A + 2 * W_BC, :])
    y_ref[...] = _layer_norm(alpha * x_ref[...] + acc, g_ref[...], bb_ref[...])


def _out_proj(x2d, oa, ob, oc, w_bf, g, b, alpha, tm):
    m = x2d.shape[0]
    row = lambda w: pl.BlockSpec((tm, w), lambda i: (i, 0))
    const = lambda shape: pl.BlockSpec(shape, lambda i: (0, 0))
    return pl.pallas_call(
        functools.partial(_out_proj_kernel, alpha=alpha),
        grid=(m // tm,),
        in_specs=[row(D_MODEL), row(W_A), row(W_BC), row(W_BC), const((D_MODEL, D_MODEL)),
                  const((1, D_MODEL)), const((1, D_MODEL))],
        out_specs=row(D_MODEL),
        out_shape=jax.ShapeDtypeStruct((m, D_MODEL), F32),
        compiler_params=_cparams("parallel"),
        name="out_proj",
    )(x2d, oa, ob, oc, w_bf, g.reshape(1, -1), b.reshape(1, -1))


def _ffn_kernel(x_ref, w1_ref, w3_ref, w2_ref, g_ref, b_ref, y_ref, acc_sc, *, alpha):
    f = pl.program_id(1)
    x = x_ref[...]
    xb = x.astype(BF16)

    @pl.when(f == 0)
    def _():
        acc_sc[...] = jnp.zeros(acc_sc.shape, F32)

    h = jax.nn.silu(_dot(xb, w1_ref[...])) * _dot(xb, w3_ref[...])
    acc_sc[...] += _dot(h.astype(BF16), w2_ref[...])

    @pl.when(f == pl.num_programs(1) - 1)
    def _():
        y_ref[...] = _layer_norm(alpha * x + acc_sc[...], g_ref[...], b_ref[...])


def _ffn(x2d, w1, w3, w2, g, b, alpha, tm, tf):
    m = x2d.shape[0]
    d_ff = w1.shape[1]
    return pl.pallas_call(
        functools.partial(_ffn_kernel, alpha=alpha),
        grid=(m // tm, d_ff // tf),
        in_specs=[pl.BlockSpec((tm, D_MODEL), lambda i, f: (i, 0)),
                  pl.BlockSpec((D_MODEL, tf), lambda i, f: (0, f)),
                  pl.BlockSpec((D_MODEL, tf), lambda i, f: (0, f)),
                  pl.BlockSpec((tf, D_MODEL), lambda i, f: (f, 0)),
                  pl.BlockSpec((1, D_MODEL), lambda i, f: (0, 0)),
                  pl.BlockSpec((1, D_MODEL), lambda i, f: (0, 0))],
        out_specs=pl.BlockSpec((tm, D_MODEL), lambda i, f: (i, 0)),
        out_shape=jax.ShapeDtypeStruct((m, D_MODEL), F32),
        scratch_shapes=[pltpu.VMEM((tm, D_MODEL), F32)],
        compiler_params=_cparams("parallel", "arbitrary"),
        name="ffn",
    )(x2d, w1, w3, w2, g.reshape(1, -1), b.reshape(1, -1))


def _moe_kernel(x_ref, rh_ref, rl_ref, w1_ref, w3_ref, w2_ref, g_ref, b_ref, y_ref, acc_sc, gate_sc, xb_sc,
                *, alpha, n_experts):
    e = pl.program_id(1)
    lane = lax.broadcasted_iota(jnp.int32, (1, LANES), 1)

    @pl.when(e == 0)
    def _():
        x = x_ref[...]
        xh, xl = _split2(x)
        xb_sc[...] = xh
        logits = _dot(xh, rh_ref[...]) + _dot(xl, rh_ref[...]) + _dot(xh, rl_ref[...])
        logits = jnp.where(lane < n_experts, logits, NEG)
        v1 = jnp.max(logits, axis=-1, keepdims=True)
        i1 = jnp.min(jnp.where(logits == v1, lane, LANES), axis=-1, keepdims=True)
        rest = jnp.where(lane == i1, NEG, logits)
        v2 = jnp.max(rest, axis=-1, keepdims=True)
        i2 = jnp.min(jnp.where(rest == v2, lane, LANES), axis=-1, keepdims=True)
        ex = jnp.exp(v2 - v1)
        g1 = 1.0 / (1.0 + ex)
        g2 = ex / (1.0 + ex)
        gate_sc[...] = jnp.where(lane == i1, g1, 0.0) + jnp.where(lane == i2, g2, 0.0)
        acc_sc[...] = jnp.zeros(acc_sc.shape, F32)

    xb = xb_sc[...]
    h = jax.nn.silu(_dot(xb, w1_ref[...])) * _dot(xb, w3_ref[...])
    f = _dot(h.astype(BF16), w2_ref[...])
    ge = jnp.sum(jnp.where(lane == e, gate_sc[...], 0.0), axis=-1, keepdims=True)
    acc_sc[...] += ge * f

    @pl.when(e == n_experts - 1)
    def _():
        y_ref[...] = _layer_norm(alpha * x_ref[...] + acc_sc[...], g_ref[...], b_ref[...])


def _moe(x2d, router, w1, w3, w2, g, b, alpha, tm):
    m = x2d.shape[0]
    n_experts, _, d_e = w1.shape
    r_pad = jnp.zeros((D_MODEL, LANES), F32).at[:, :n_experts].set(router)
    r_hi = r_pad.astype(BF16)
    r_lo = (r_pad - r_hi.astype(F32)).astype(BF16)
    return pl.pallas_call(
        functools.partial(_moe_kernel, alpha=alpha, n_experts=n_experts),
        grid=(m // tm, n_experts),
        in_specs=[pl.BlockSpec((tm, D_MODEL), lambda i, e: (i, 0)),
                  pl.BlockSpec((D_MODEL, LANES), lambda i, e: (0, 0)),
                  pl.BlockSpec((D_MODEL, LANES), lambda i, e: (0, 0)),
                  pl.BlockSpec((None, D_MODEL, d_e), lambda i, e: (e, 0, 0)),
                  pl.BlockSpec((None, D_MODEL, d_e), lambda i, e: (e, 0, 0)),
                  pl.BlockSpec((None, d_e, D_MODEL), lambda i, e: (e, 0, 0)),
                  pl.BlockSpec((1, D_MODEL), lambda i, e: (0, 0)),
                  pl.BlockSpec((1, D_MODEL), lambda i, e: (0, 0))],
        out_specs=pl.BlockSpec((tm, D_MODEL), lambda i, e: (i, 0)),
        out_shape=jax.ShapeDtypeStruct((m, D_MODEL), F32),
        scratch_shapes=[pltpu.VMEM((tm, D_MODEL), F32), pltpu.VMEM((tm, LANES), F32),
                        pltpu.VMEM((tm, D_MODEL), BF16)],
        compiler_params=_cparams("parallel", "arbitrary"),
        name="moe",
    )(x2d, r_hi, r_lo, w1, w3, w2, g.reshape(1, -1), b.reshape(1, -1))


def _rotary_tables(pos0, t_pad):
    half = D_HEAD // 2
    inv = 1.0 / (10000.0 ** (jnp.arange(half, dtype=F32) / half))
    ang = (pos0 + jnp.arange(t_pad, dtype=jnp.int32)).astype(F32)[:, None] * inv[None, :]
    cos, sin = jnp.cos(ang), jnp.sin(ang)
    cos_t = jnp.tile(jnp.concatenate([cos, cos], axis=-1), (1, N_HEADS))
    sin_t = jnp.tile(jnp.concatenate([-sin, sin], axis=-1), (1, N_HEADS))
    return cos_t, sin_t


def _pad_rows(lo, arr):
    return jnp.zeros((W_BC, W_BC), F32).at[lo:lo + arr.shape[0]].set(arr).astype(BF16)


def _ffn_tile(d_ff):
    best = LANES if d_ff % LANES == 0 else d_ff
    for t in range(LANES, min(d_ff, 1536) + 1, LANES):
        if d_ff % t == 0:
            best = t
    return best


def kernel(x_prompt, x_sample, cache_k, cache_v, page_table, state_ret, state_wkv, state_shift, rel_bias, w_in, w_out, lambda_q1, lambda_k1, lambda_q2, lambda_k2, subln_g, tshift_mu, decay_w0, decay_w2, iclr_a0, iclr_a2, gate_w2, k_k, k_a, r_k, lnx_g, lnx_b, ln1_g, ln1_b, ln2_g, ln2_b, ffn_w1, ffn_w3, ffn_w2, router_w, expert_w1, expert_w3, expert_w2):
    bp, seq, _ = x_prompt.shape
    n_dec, t_new, _ = x_sample.shape
    depth, n_pool, page, _, _ = cache_k.shape
    n_pages = page_table.shape[1]
    past = n_pages * page
    alpha = (2 * depth) ** 0.25

    mp, ms = bp * seq, n_dec * t_new
    tm_p = 256 if mp % 256 == 0 else mp
    tm_s = ms
    blk = 256 if seq % 256 == 0 else seq
    chunk_p = 64 if seq % 64 == 0 else seq
    sub_p = 4 if seq % (4 * chunk_p) == 0 else 1
    chunk_s = 32
    group = 16 if n_pages % 16 == 0 else (8 if n_pages % 8 == 0 else 1)

    cache_k4 = cache_k.reshape(depth, n_pool, page * N_HEADS, 2 * D_HEAD_A)
    cache_v4 = cache_v.reshape(depth, n_pool, page * N_HEADS, 2 * D_HEAD_A)
    pt_flat = page_table.reshape(-1).astype(jnp.int32)

    rel_bias = rel_bias.astype(F32)
    near_bias = _near_bias_tiles(rel_bias, blk)
    sample_tiles = _sample_bias_tiles(rel_bias, t_new, n_pages, page)
    cos_p, sin_p = _rotary_tables(0, seq)
    cos_s, sin_s = _rotary_tables(past, chunk_s)
    zero_st_p = jnp.zeros((bp, N_HEADS, D_HEAD, D_HEAD), F32)
    zero_shift_p = jnp.zeros((bp, 1, PROJ_C), F32)

    def pad_sample(a):
        w = a.shape[-1]
        a3 = a.reshape(n_dec, t_new, w)
        return jnp.pad(a3, ((0, 0), (0, chunk_s - t_new), (0, 0))).reshape(n_dec * chunk_s, w)

    def unpad_sample(a):
        w = a.shape[-1]
        return a.reshape(n_dec, chunk_s, w)[:, :t_new].reshape(n_dec * t_new, w)

    xp = x_prompt.reshape(mp, D_MODEL)
    xs = x_sample.reshape(ms, D_MODEL)
    kp_all = vp_all = ks_all = vs_all = None
    outs = {k: [] for k in ("rp", "rs", "wp", "ws", "sp", "ss")}
    for l in range(depth):
        w_in_bf = w_in[l].astype(BF16)
        w_out_bf = w_out[l].astype(BF16)
        lam_init = 0.8 - 0.6 * math.exp(-0.3 * l)
        lamv = jnp.stack([lambda_q1[l], lambda_k1[l], lambda_q2[l], lambda_k2[l]]).astype(F32)
        laminit = jnp.full((1, 1), lam_init, F32)
        row = lambda a: a.reshape(1, -1).astype(F32)
        rwkv_params = [row(tshift_mu[l]), row(decay_w0[l]), _pad_rows(0, decay_w2[l]), row(iclr_a0[l]),
                       _pad_rows(64, iclr_a2[l]), _pad_rows(128, gate_w2[l]),
                       row(k_k[l]), row(k_a[l]), row(r_k[l]), row(lnx_g[l]), row(lnx_b[l])]

        qa, kp_all, vp_all, kb, vb, pb, pc = _proj_in(xp, w_in_bf, kp_all, vp_all, l, depth, tm_p)
        oa = _attn_prompt(qa, kb, vb, lamv, laminit, near_bias, subln_g[l], bp, seq, blk)
        ob, ret_p = _retention(pb, cos_p, sin_p, zero_st_p, bp, seq, chunk_p, sub_p, chunk_p)
        oc, wkv_p = _rwkv(pc, zero_shift_p, zero_st_p, rwkv_params, bp, seq, chunk_p, sub_p, chunk_p)
        xp = _out_proj(xp, oa, ob, oc, w_out_bf, ln1_g[l], ln1_b[l], alpha, tm_p)
        outs["rp"].append(ret_p)
        outs["wp"].append(wkv_p)
        outs["sp"].append(pc.reshape(bp, seq, PROJ_C)[:, -1])

        qa, ks_all, vs_all, kb, vb, pb, pc = _proj_in(xs, w_in_bf, ks_all, vs_all, l, depth, tm_s)
        oa = _attn_sample(qa, kb, vb, cache_k4, cache_v4, pt_flat, l, lamv, laminit, sample_tiles, subln_g[l],
                          n_dec, t_new, n_pages, page, group)
        ob, ret_s = _retention(pad_sample(pb), cos_s, sin_s, state_ret[l], n_dec, chunk_s, chunk_s, 1, t_new)
        oc, wkv_s = _rwkv(pad_sample(pc), state_shift[l].astype(F32).reshape(n_dec, 1, PROJ_C),
                          state_wkv[l], rwkv_params, n_dec, chunk_s, chunk_s, 1, t_new)
        xs = _out_proj(xs, oa, unpad_sample(ob), unpad_sample(oc), w_out_bf, ln1_g[l], ln1_b[l], alpha, tm_s)
        outs["rs"].append(ret_s)
        outs["ws"].append(wkv_s)
        outs["ss"].append(pc.reshape(n_dec, t_new, PROJ_C)[:, -1])

        j = l // 2
        if l % 2 == 0:
            w1, w3, w2 = ffn_w1[j].astype(BF16), ffn_w3[j].astype(BF16), ffn_w2[j].astype(BF16)
            tf = _ffn_tile(w1.shape[1])
            xp = _ffn(xp, w1, w3, w2, ln2_g[l], ln2_b[l], alpha, min(512, mp), tf)
            xs = _ffn(xs, w1, w3, w2, ln2_g[l], ln2_b[l], alpha, tm_s, tf)
        else:
            w1, w3, w2 = expert_w1[j].astype(BF16), expert_w3[j].astype(BF16), expert_w2[j].astype(BF16)
            xp = _moe(xp, router_w[j], w1, w3, w2, ln2_g[l], ln2_b[l], alpha, min(512, mp))
            xs = _moe(xs, router_w[j], w1, w3, w2, ln2_g[l], ln2_b[l], alpha, tm_s)

    st = lambda key: jnp.stack(outs[key], axis=0)
    kv_p = lambda a: a.reshape(depth, bp, seq, N_HEADS, 2 * D_HEAD_A)
    kv_s = lambda a: a.reshape(depth, n_dec, t_new, N_HEADS, 2 * D_HEAD_A)
    return (xp.reshape(bp, seq, D_MODEL), xs.reshape(n_dec, t_new, D_MODEL),
            kv_p(kp_all), kv_p(vp_all), kv_s(ks_all), kv_s(vs_all),
            st("rp"), st("rs"), st("wp"), st("ws"), st("sp"), st("ss"))
```

```python
import functools
import math

import jax
import jax.numpy as jnp
from jax import lax
from jax.experimental import pallas as pl
from jax.experimental.pallas import tpu as pltpu

F32 = jnp.float32
BF16 = jnp.bfloat16

D_MODEL = 1024
N_HEADS = 4
D_HEAD_A = 64
W_A = N_HEADS * 2 * D_HEAD_A
D_HEAD = 64
W_BC = N_HEADS * D_HEAD
PROJ_A = 3 * W_A
PROJ_B = 4 * W_BC
PROJ_C = 1024
N_BUCKETS = 32
MAX_DISTANCE = 128
NEAR_DIST = 113
RWKV_GN_EPS = 64e-5
LN_EPS = 1e-5
RMS_EPS = 1e-5
NEG = -1e30
LOG2E = 1.4426950408889634

VMEM_LIMIT_BYTES = 56 * 1024 * 1024
LANES = 128


def _cparams(*sem):
    return pltpu.CompilerParams(dimension_semantics=sem, vmem_limit_bytes=VMEM_LIMIT_BYTES)


def _dot(a, b):
    return jnp.dot(a, b, preferred_element_type=F32)


def _dot_nt(a, b):
    return lax.dot_general(a, b, (((1,), (1,)), ((), ())), preferred_element_type=F32)


def _dot_tn(a, b):
    return lax.dot_general(a, b, (((0,), (0,)), ((), ())), preferred_element_type=F32)


def _split2(x):
    hi = x.astype(BF16)
    lo = (x - hi.astype(F32)).astype(BF16)
    return hi, lo


def _split3(x):
    hi = x.astype(BF16)
    r1 = x - hi.astype(F32)
    mid = r1.astype(BF16)
    lo = (r1 - mid.astype(F32)).astype(BF16)
    return hi, mid, lo


def _softplus(z):
    return jnp.maximum(z, 0.0) + jnp.log1p(jnp.exp(-jnp.abs(z)))


def _layer_norm(z, g, b):
    mu = jnp.mean(z, axis=-1, keepdims=True)
    d = z - mu
    var = jnp.mean(d * d, axis=-1, keepdims=True)
    return d * lax.rsqrt(var + LN_EPS) * g + b


def _head_masks():
    lane = lax.broadcasted_iota(jnp.int32, (1, W_BC), 1)
    return [(lane >= h * D_HEAD) & (lane < (h + 1) * D_HEAD) for h in range(N_HEADS)]


def _stack_heads(x, masks):
    zero = jnp.zeros((), x.dtype)
    return jnp.concatenate([jnp.where(m, x, zero) for m in masks], axis=0)


def _unstack_heads(z, c):
    return z[0:c] + z[c:2 * c] + z[2 * c:3 * c] + z[3 * c:4 * c]


def _load_block_diag(s_ref):
    rows = []
    for h in range(N_HEADS):
        pieces = [jnp.zeros((D_HEAD, D_HEAD), F32)] * N_HEADS
        pieces[h] = s_ref[h].astype(F32)
        rows.append(jnp.concatenate(pieces, axis=1))
    return jnp.concatenate(rows, axis=0)


def _store_block_diag(s_ref, state):
    for h in range(N_HEADS):
        s_ref[h] = state[h * D_HEAD:(h + 1) * D_HEAD, h * D_HEAD:(h + 1) * D_HEAD]


def _head_ones():
    r = lax.broadcasted_iota(jnp.int32, (W_BC, W_BC), 0)
    c = lax.broadcasted_iota(jnp.int32, (W_BC, W_BC), 1)
    return jnp.where((r // D_HEAD) == (c // D_HEAD), 1.0, 0.0).astype(BF16)


def _head_sum(x, ones_bd):
    hi, lo = _split2(x)
    return _dot(hi, ones_bd) + _dot(lo, ones_bd)


def _proj_in_kernel(*refs, aliased):
    x_ref, w_ref = refs[0], refs[1]
    qa_ref, k_ref, v_ref, kb_ref, vb_ref, pb_ref, pc_ref = refs[4:] if aliased else refs[2:]
    xb = x_ref[...].astype(BF16)

    def mm(lo, hi):
        return _dot(xb, w_ref[:, lo:hi])

    qa_ref[...] = (mm(0, W_A) * (D_HEAD_A ** -0.5 * LOG2E)).astype(BF16)
    tm = xb.shape[0]
    dv = 2 * D_HEAD_A
    for ref, bref, lo in ((k_ref, kb_ref, W_A), (v_ref, vb_ref, 2 * W_A)):
        val = mm(lo, lo + W_A)
        bref[...] = val.astype(BF16)
        for h in range(N_HEADS):
            ref[pl.ds(h, tm, stride=N_HEADS), :] = val[:, h * dv:(h + 1) * dv]
    pb_ref[...] = mm(PROJ_A, PROJ_A + PROJ_B)
    pc_ref[...] = mm(PROJ_A + PROJ_B, PROJ_A + PROJ_B + PROJ_C)


def _proj_in(x2d, w_bf, k_all, v_all, layer, depth, tm):
    m = x2d.shape[0]
    n = w_bf.shape[1]
    aliased = k_all is not None
    row = lambda i: (i, 0)
    stack_spec = pl.BlockSpec((None, tm * N_HEADS, 2 * D_HEAD_A), lambda i: (layer, i, 0))
    stack_shape = jax.ShapeDtypeStruct((depth, m * N_HEADS, 2 * D_HEAD_A), F32)
    in_specs = [pl.BlockSpec((tm, D_MODEL), row), pl.BlockSpec((D_MODEL, n), lambda i: (0, 0))]
    args = [x2d, w_bf]
    if aliased:
        in_specs += [pl.BlockSpec(memory_space=pl.ANY), pl.BlockSpec(memory_space=pl.ANY)]
        args += [k_all, v_all]
    return pl.pallas_call(
        functools.partial(_proj_in_kernel, aliased=aliased),
        grid=(m // tm,),
        in_specs=in_specs,
        out_specs=[pl.BlockSpec((tm, W_A), row), stack_spec, stack_spec,
                   pl.BlockSpec((tm, W_A), row), pl.BlockSpec((tm, W_A), row),
                   pl.BlockSpec((tm, PROJ_B), row), pl.BlockSpec((tm, PROJ_C), row)],
        out_shape=[jax.ShapeDtypeStruct((m, W_A), BF16), stack_shape, stack_shape,
                   jax.ShapeDtypeStruct((m, W_A), BF16),
                   jax.ShapeDtypeStruct((m, W_A), BF16), jax.ShapeDtypeStruct((m, PROJ_B), F32),
                   jax.ShapeDtypeStruct((m, PROJ_C), F32)],
        input_output_aliases={2: 1, 3: 2} if aliased else {},
        compiler_params=_cparams("parallel"),
        name="proj_in",
    )(*args)


def _lambda_full(lamv_ref, laminit_ref):
    lv = lamv_ref[...]
    s1 = jnp.sum(lv[0:1] * lv[1:2], axis=-1, keepdims=True)
    s2 = jnp.sum(lv[2:3] * lv[3:4], axis=-1, keepdims=True)
    return jnp.exp(s1) - jnp.exp(s2) + laminit_ref[...]


def _sub_rms(o, g, scale):
    y = o * lax.rsqrt(jnp.mean(o * o, axis=-1, keepdims=True) + RMS_EPS)
    return y * g * scale


def _rel_bucket(n):
    max_exact = N_BUCKETS // 2
    large = max_exact + (jnp.log(jnp.maximum(n, 1).astype(F32) / max_exact)
                         / math.log(MAX_DISTANCE / max_exact) * (N_BUCKETS - max_exact)).astype(jnp.int32)
    large = jnp.minimum(large, N_BUCKETS - 1)
    return jnp.where(n < max_exact, n, large)


V_AUG = 2 * D_HEAD_A + 16
PAIR = 2
HPS = 4


def _attn_prompt_kernel(lamv_ref, laminit_ref, q_ref, k_ref, v_ref, near_ref, g_ref, o_ref, *scratch, blk, seq):
    i = pl.program_id(2)
    dv = 2 * D_HEAD_A
    kp = PAIR * blk
    per_head = 1 + 2 + 2 + 4 + 4
    heads = []
    for h in range(HPS):
        sc = scratch[h * per_head:(h + 1) * per_head]
        heads.append(dict(vt=sc[0], m=sc[1:3], acc=sc[3:5], s=(sc[5:7], sc[7:9]), cm=(sc[9:11], sc[11:13])))
    lane = lax.broadcasted_iota(jnp.int32, (1, dv), 1)
    lo = lane < D_HEAD_A
    zero = jnp.zeros((), BF16)

    @pl.when(i == 0)
    def _():
        for h, hd in enumerate(heads):
            for c in range(seq // blk):
                vt = v_ref[c * blk:(c + 1) * blk, h * dv:(h + 1) * dv].astype(F32).T
                hd["vt"][0:dv, c * blk:(c + 1) * blk] = vt.astype(BF16)
            hd["vt"][dv:V_AUG, :] = jnp.ones((V_AUG - dv, seq), BF16)

    for h, hd in enumerate(heads):
        q = q_ref[:, h * dv:(h + 1) * dv]
        hd["q"] = (jnp.where(lo, q, zero), jnp.where(lo, zero, q))
        for mp in range(2):
            hd["m"][mp][...] = jnp.full(hd["m"][mp].shape, NEG, F32)
            hd["acc"][mp][...] = jnp.zeros(hd["acc"][mp].shape, F32)

    def stage_a(pair, slot, near_idx):
        off = pl.multiple_of(pair * kp, kp)
        for h, hd in enumerate(heads):
            cmax = [None, None]
            for sub in range(PAIR):
                kblk = k_ref[pl.ds(off + sub * blk, blk), h * dv:(h + 1) * dv]
                for mp in range(2):
                    s = _dot_nt(kblk, hd["q"][mp])
                    if near_idx is not None:
                        s = s + near_ref[h, near_idx, sub * blk:(sub + 1) * blk, :]
                    hd["s"][slot][mp][sub] = s
                    smax = jnp.max(s, axis=0, keepdims=True)
                    cmax[mp] = smax if cmax[mp] is None else jnp.maximum(cmax[mp], smax)
            for mp in range(2):
                hd["cm"][slot][mp][...] = cmax[mp]

    def stage_b(pair, slot):
        off = pl.multiple_of(pair * kp, kp)
        for hd in heads:
            vtb = hd["vt"][:, pl.ds(off, kp)]
            for mp in range(2):
                m_old = hd["m"][mp][...]
                mn = jnp.maximum(m_old, hd["cm"][slot][mp][...])
                p = jnp.concatenate([jnp.exp2(hd["s"][slot][mp][sub] - mn).astype(BF16) for sub in range(PAIR)],
                                    axis=0)
                hd["m"][mp][...] = mn
                hd["acc"][mp][...] = jnp.exp2(m_old - mn) * hd["acc"][mp][...] + _dot(vtb, p)

    n_pairs = (i + 2) // 2
    t1 = n_pairs - 1
    t0 = jnp.maximum(n_pairs - 2, 0)
    n_far = t0
    stage_a(t1, 0, 1)
    stage_a(t0, 1, 0)
    stage_b(t1, 0)

    def far_body(j, carry):
        stage_a(2 * j, 0, None)
        stage_b(jnp.where(j == 0, t0, 2 * j - 1), 1)
        stage_a(2 * j + 1, 1, None)
        stage_b(2 * j, 0)
        return carry

    lax.fori_loop(0, n_far // 2, far_body, 0)
    pending = jnp.where(n_far < 2, t0, ((n_far // 2) * 2) - 1)

    @pl.when(n_far % 2 == 1)
    def _():
        stage_a(n_far - 1, 0, None)
        stage_b(pending, 1)
        stage_b(n_far - 1, 0)

    @pl.when(n_far % 2 == 0)
    def _():
        stage_b(pending, 1)

    lam = _lambda_full(lamv_ref, laminit_ref)
    for h, hd in enumerate(heads):
        a1, a2 = hd["acc"][0][...], hd["acc"][1][...]
        o_t = a1[0:dv] / a1[dv:dv + 1] - lam * (a2[0:dv] / a2[dv:dv + 1])
        o_ref[:, h * dv:(h + 1) * dv] = _sub_rms(o_t.T, g_ref[...], 1.0 - laminit_ref[...]).astype(BF16)


def _near_bias_tiles(rel_bias, blk):
    cidx = jnp.arange(blk, dtype=jnp.int32)[:, None]
    ridx = jnp.arange(blk, dtype=jnp.int32)[None, :]
    rel = (rel_bias - rel_bias[N_BUCKETS - 1][None, :]) * LOG2E

    def lookup(dist):
        onehot = (_rel_bucket(dist)[..., None] == jnp.arange(N_BUCKETS, dtype=jnp.int32)).astype(F32)
        return jnp.einsum("crb,bh->hcr", onehot, rel, precision=lax.Precision.HIGHEST)

    dist0 = ridx - cidx
    diag = jnp.where((dist0 >= 0)[None], lookup(jnp.maximum(dist0, 0)), NEG)
    prev = lookup(dist0 + blk)
    masked = jnp.full_like(diag, NEG)
    zeros = jnp.zeros_like(diag)
    cat = lambda a, b: jnp.concatenate([a, b], axis=1)
    variants = [(cat(masked, masked), cat(diag, masked)), (cat(masked, masked), cat(prev, diag)),
                (cat(zeros, prev), cat(diag, masked)), (cat(zeros, zeros), cat(prev, diag))]
    return jnp.stack([jnp.stack(v, axis=1) for v in variants], axis=1)


def _attn_prompt(qa, kb, vb, lamv, laminit, near_bias, subln_g, batch, seq, blk):
    m = batch * seq
    nq = seq // blk
    dv = 2 * D_HEAD_A
    assert blk >= NEAR_DIST - 1 and nq % PAIR == 0 and N_HEADS % HPS == 0
    qspec = pl.BlockSpec((blk, HPS * dv), lambda b, h, i: (b * nq + i, h))
    kvspec = pl.BlockSpec((seq, HPS * dv), lambda b, h, i: (b, h))
    near = pl.BlockSpec((HPS, None, 2, PAIR * blk, blk),
                        lambda b, h, i: (h, jnp.where(i < 2, i, 2 + (i & 1)), 0, 0, 0))
    const2 = lambda shape: pl.BlockSpec(shape, lambda b, h, i: (0, 0))
    per_head = ([pltpu.VMEM((V_AUG, seq), BF16)] + [pltpu.VMEM((1, blk), F32)] * 2
                + [pltpu.VMEM((V_AUG, blk), F32)] * 2 + [pltpu.VMEM((PAIR, blk, blk), F32)] * 4
                + [pltpu.VMEM((1, blk), F32)] * 4)
    return pl.pallas_call(
        functools.partial(_attn_prompt_kernel, blk=blk, seq=seq),
        grid=(batch, N_HEADS // HPS, nq),
        in_specs=[const2((4, D_HEAD_A)), const2((1, 1)),
                  qspec, kvspec, kvspec, near, const2((1, dv))],
        out_specs=qspec,
        out_shape=jax.ShapeDtypeStruct((m, W_A), BF16),
        scratch_shapes=per_head * HPS,
        compiler_params=_cparams("parallel", "parallel", "arbitrary"),
        name="attn_prompt",
    )(lamv, laminit, qa, kb, vb, near_bias, subln_g.reshape(1, -1))


def _attn_sample_kernel(pt_ref, lamv_ref, laminit_ref, q_ref, kn_ref, vn_ref, bfar_ref, blast_ref, bnew_ref, g_ref,
                        *refs, group, n_steps, rows):
    k_refs = refs[:group]
    v_refs = refs[group:2 * group]
    o_ref = refs[2 * group]
    m_sc, l_sc, acc_sc = refs[2 * group + 1:]
    s_id = pl.program_id(1)
    cols = k_refs[0].shape[0]

    q = q_ref[...]
    lane = lax.broadcasted_iota(jnp.int32, (1, 2 * D_HEAD_A), 1)
    lo = lane < D_HEAD_A
    zero = jnp.zeros((), BF16)
    wq = jnp.concatenate([jnp.where(lo, q, zero), jnp.where(lo, zero, q)], axis=0)

    def update(s, pv_of):
        m = m_sc[...]
        mn = jnp.maximum(m, jnp.max(s, axis=-1, keepdims=True))
        al = jnp.exp2(m - mn)
        p = jnp.exp2(s - mn)
        m_sc[...] = mn
        l_sc[...] = al * l_sc[...] + jnp.sum(p, axis=-1, keepdims=True)
        acc_sc[...] = al * acc_sc[...] + pv_of(p.astype(BF16))

    @pl.when(s_id == 0)
    def _():
        m_sc[...] = jnp.full(m_sc.shape, NEG, F32)
        l_sc[...] = jnp.zeros(l_sc.shape, F32)
        acc_sc[...] = jnp.zeros(acc_sc.shape, F32)
        update(_dot_nt(wq, kn_ref[...]) + bnew_ref[...], lambda p: _dot(p, vn_ref[...]))

    bfar = bfar_ref[...]
    parts = []
    for g in range(group):
        bias = bfar
        if g == group - 1:
            bias = jnp.where(s_id == n_steps - 1, blast_ref[...], bfar)
        parts.append(_dot_nt(wq, k_refs[g][...].astype(BF16)) + bias)

    def pv_pages(p):
        return sum(_dot(p[:, g * cols:(g + 1) * cols], v_refs[g][...].astype(BF16)) for g in range(group))

    update(jnp.concatenate(parts, axis=1), pv_pages)

    @pl.when(s_id == n_steps - 1)
    def _():
        lam = _lambda_full(lamv_ref, laminit_ref)
        o = acc_sc[...] / l_sc[...]
        od = o[0:rows] - lam * o[rows:2 * rows]
        o_ref[...] = _sub_rms(od, g_ref[...], 1.0 - laminit_ref[...]).astype(BF16)


def _sample_bias_tiles(rel_bias, t_new, n_pages, page):
    rows = t_new * N_HEADS
    cols = page * N_HEADS
    past = n_pages * page
    assert page + 1 >= NEAR_DIST
    rel = (rel_bias - rel_bias[N_BUCKETS - 1][None, :]) * LOG2E
    r = jnp.arange(2 * rows, dtype=jnp.int32)[:, None]
    rt, rh = (r % rows) // N_HEADS, r % N_HEADS
    rel_rows = rel[:, rh[:, 0]]

    def lookup(dist):
        onehot = (_rel_bucket(dist)[..., None] == jnp.arange(N_BUCKETS, dtype=jnp.int32)).astype(F32)
        return jnp.einsum("rcb,br->rc", onehot, rel_rows, precision=lax.Precision.HIGHEST)

    c = jnp.arange(cols, dtype=jnp.int32)[None, :]
    ct, ch = c // N_HEADS, c % N_HEADS
    same = rh == ch
    bfar = jnp.where(same, 0.0, NEG).astype(F32)
    dist_last = (past + rt) - ((n_pages - 1) * page + ct)
    blast = jnp.where(same, lookup(dist_last), NEG).astype(F32)
    cn = jnp.arange(rows, dtype=jnp.int32)[None, :]
    cnt, cnh = cn // N_HEADS, cn % N_HEADS
    dist_new = rt - cnt
    bnew = jnp.where((rh == cnh) & (dist_new >= 0), lookup(jnp.maximum(dist_new, 0)), NEG).astype(F32)
    return bfar, blast, bnew


def _attn_sample(qa, kb, vb, cache_k4, cache_v4, pt_flat, layer, lamv, laminit, bias_tiles, subln_g,
                 n_dec, t_new, n_pages, page, group):
    rows = t_new * N_HEADS
    cols = page * N_HEADS
    n_steps = n_pages // group
    bfar, blast, bnew = bias_tiles
    q3 = qa.reshape(n_dec, rows, 2 * D_HEAD_A)
    kn3 = kb.reshape(n_dec, rows, 2 * D_HEAD_A)
    vn3 = vb.reshape(n_dec, rows, 2 * D_HEAD_A)
    per_b = pl.BlockSpec((None, rows, 2 * D_HEAD_A), lambda b, s, pt: (b, 0, 0))
    const = lambda shape: pl.BlockSpec(shape, lambda b, s, pt: (0, 0))

    def page_spec(g):
        return pl.BlockSpec((None, None, cols, 2 * D_HEAD_A),
                            lambda b, s, pt: (layer, pt[b * n_pages + s * group + g], 0, 0))

    in_specs = [const((4, D_HEAD_A)), const((1, 1)), per_b, per_b, per_b,
                const((2 * rows, cols)), const((2 * rows, cols)), const((2 * rows, rows)),
                const((1, 2 * D_HEAD_A))]
    in_specs += [page_spec(g) for g in range(group)] * 2
    out = pl.pallas_call(
        functools.partial(_attn_sample_kernel, group=group, n_steps=n_steps, rows=rows),
        grid_spec=pltpu.PrefetchScalarGridSpec(
            num_scalar_prefetch=1,
            grid=(n_dec, n_steps),
            in_specs=in_specs,
            out_specs=per_b,
            scratch_shapes=[pltpu.VMEM((2 * rows, 1), F32), pltpu.VMEM((2 * rows, 1), F32),
                            pltpu.VMEM((2 * rows, 2 * D_HEAD_A), F32)],
        ),
        out_shape=jax.ShapeDtypeStruct((n_dec, rows, 2 * D_HEAD_A), BF16),
        compiler_params=_cparams("parallel", "arbitrary"),
        name="attn_sample",
    )(pt_flat, lamv, laminit, q3, kn3, vn3, bfar, blast, bnew, subln_g.reshape(1, -1),
      *([cache_k4] * group), *([cache_v4] * group))
    return out.reshape(n_dec * t_new, W_A)


def _ret_kernel(pb_ref, cos_ref, sin_ref, s0_ref, o_ref, sfin_ref, st_sc, *, chunk, n_sub, n_valid, n_steps):
    c_id = pl.program_id(1)
    lg = chunk.bit_length() - 1
    rows = N_HEADS * chunk
    tb = n_sub * chunk
    masks = _head_masks()
    ones_bd = _head_ones()

    @pl.when(c_id == 0)
    def _():
        st_sc[...] = _load_block_diag(s0_ref)

    pb = pb_ref[...]
    q, k, v, g = pb[:, 0:W_BC], pb[:, W_BC:2 * W_BC], pb[:, 2 * W_BC:3 * W_BC], pb[:, 3 * W_BC:4 * W_BC]
    lane = lax.broadcasted_iota(jnp.int32, (1, W_BC), 1)
    first_half = (lane & (D_HEAD - 1)) < (D_HEAD // 2)
    cosf, sins = cos_ref[...], sin_ref[...]

    def rot(x):
        swapped = jnp.where(first_half, pltpu.roll(x, W_BC - D_HEAD // 2, 1), pltpu.roll(x, D_HEAD // 2, 1))
        return x * cosf + swapped * sins

    q = rot(q)
    k = rot(k) * (D_HEAD ** -0.5)
    row = lax.broadcasted_iota(jnp.int32, (tb, 1), 0)
    if n_valid < chunk:
        valid = row < n_valid
        k = jnp.where(valid, k, 0.0)
        v = jnp.where(valid, v, 0.0)

    log_g = [math.log1p(-(2.0 ** (-5 - h))) for h in range(N_HEADS)]
    lg_lane = sum(jnp.where(m, log_g[h], 0.0) for h, m in enumerate(masks))
    ri = lax.broadcasted_iota(jnp.int32, (rows, rows), 0)
    ci = lax.broadcasted_iota(jnp.int32, (rows, rows), 1)
    rh = lax.broadcasted_iota(jnp.int32, (rows, 1), 0) >> lg
    rpos = (lax.broadcasted_iota(jnp.int32, (rows, 1), 0) & (chunk - 1)).astype(F32)
    lg_row = sum(jnp.where(rh == h, log_g[h], 0.0) for h in range(N_HEADS))
    diff = (ri & (chunk - 1)) - (ci & (chunk - 1))
    keep = ((ri >> lg) == (ci >> lg)) & (diff >= 0)
    dmask = jnp.where(keep, jnp.exp(lg_row * jnp.maximum(diff, 0).astype(F32)), 0.0)
    xi = jnp.exp(lg_row * (rpos + 1.0))
    posf = (row & (chunk - 1)).astype(F32)
    zeta = jnp.exp(lg_lane * jnp.maximum(float(n_valid - 1) - posf, 0.0))
    g_chunk = jnp.exp(lg_lane * float(n_valid))
    qb, kb, vb, kzb = q.astype(BF16), k.astype(BF16), v.astype(BF16), (k * zeta).astype(BF16)

    def prepare(c):
        sl = slice(c * chunk, (c + 1) * chunk)
        q_st, k_st, v_st = (_stack_heads(x[sl], masks) for x in (qb, kb, vb))
        s_blk = _dot_nt(q_st, k_st) * dmask
        return q_st, _dot(s_blk.astype(BF16), v_st), _dot_tn(_stack_heads(kzb[sl], masks), v_st)

    prepared = [prepare(c) for c in range(n_sub)]
    state = st_sc[...]
    o_chunks = []
    for q_st, inner, incr in prepared:
        cross = _dot(q_st, state.astype(BF16)) * xi
        o_chunks.append(_unstack_heads(inner + cross, chunk))
        state = state * g_chunk + incr
    st_sc[...] = state

    o = o_chunks[0] if n_sub == 1 else jnp.concatenate(o_chunks, axis=0)
    ms = _head_sum(o * o, ones_bd) * (1.0 / D_HEAD)
    o_ref[...] = (jax.nn.silu(g) * (o * lax.rsqrt(ms + RMS_EPS))).astype(BF16)

    @pl.when(c_id == n_steps - 1)
    def _():
        _store_block_diag(sfin_ref, state)


def _retention(pb, cos_t, sin_t, s0, batch, t_pad, chunk, n_sub, n_valid):
    tb = chunk * n_sub
    n_steps = t_pad // tb
    rowspec = lambda w: pl.BlockSpec((tb, w), lambda b, c: (b * n_steps + c, 0))
    stspec = pl.BlockSpec((None, N_HEADS, D_HEAD, D_HEAD), lambda b, c: (b, 0, 0, 0))
    return pl.pallas_call(
        functools.partial(_ret_kernel, chunk=chunk, n_sub=n_sub, n_valid=n_valid, n_steps=n_steps),
        grid=(batch, n_steps),
        in_specs=[rowspec(PROJ_B), pl.BlockSpec((tb, W_BC), lambda b, c: (c, 0)),
                  pl.BlockSpec((tb, W_BC), lambda b, c: (c, 0)), stspec],
        out_specs=[rowspec(W_BC), stspec],
        out_shape=[jax.ShapeDtypeStruct((batch * t_pad, W_BC), BF16),
                   jax.ShapeDtypeStruct((batch, N_HEADS, D_HEAD, D_HEAD), F32)],
        scratch_shapes=[pltpu.VMEM((W_BC, W_BC), F32)],
        compiler_params=_cparams("parallel", "arbitrary"),
        name="retention",
    )(pb, cos_t, sin_t, s0)


def _rwkv_kernel(pc_ref, shift_ref, s0_ref, mu_ref, w0_ref, w2_ref, a0_ref, a2_ref, g2_ref, kk_ref, ka_ref,
                 rk_ref, lng_ref, lnb_ref, o_ref, sfin_ref, st_sc, prev_sc, *, chunk, n_sub, n_valid, n_steps):
    c_id = pl.program_id(1)
    lg = chunk.bit_length() - 1
    rows = N_HEADS * chunk
    tb = n_sub * chunk
    masks = _head_masks()
    ones_bd = _head_ones()

    @pl.when(c_id == 0)
    def _():
        st_sc[...] = _load_block_diag(s0_ref)
        prev_sc[...] = shift_ref[...]

    pc = pc_ref[...]
    row = lax.broadcasted_iota(jnp.int32, (tb, 1), 0)
    prev = jnp.where(row == 0, prev_sc[...], pltpu.roll(pc, 1, 0))
    prev_sc[...] = pc[tb - 1:tb, :]
    xm = pc + (prev - pc) * mu_ref[...]
    r, k, v, xt = xm[:, 0:W_BC], xm[:, W_BC:2 * W_BC], xm[:, 2 * W_BC:3 * W_BC], xm[:, 3 * W_BC:4 * W_BC]

    w_log = -_softplus(-(w0_ref[...] + _dot(jnp.tanh(xt).astype(BF16), w2_ref[...]))) - 0.5
    logw = -jnp.exp(w_log)
    a = jax.nn.sigmoid(a0_ref[...] + _dot(xt.astype(BF16), a2_ref[...]))
    g = _dot(jax.nn.sigmoid(xt).astype(BF16), g2_ref[...])

    kk = k * kk_ref[...]
    kk = kk / jnp.maximum(jnp.sqrt(_head_sum(kk * kk, ones_bd)), 1e-12)
    k2 = k * (1.0 + (a - 1.0) * ka_ref[...])
    if n_valid < chunk:
        valid = row < n_valid
        logw = jnp.where(valid, logw, 0.0)
        kk = jnp.where(valid, kk, 0.0)
        k2 = jnp.where(valid, k2, 0.0)
        v = jnp.where(valid, v, 0.0)

    ti = lax.broadcasted_iota(jnp.int32, (tb, tb), 0)
    tj = lax.broadcasted_iota(jnp.int32, (tb, tb), 1)
    same_chunk = (ti >> lg) == (tj >> lg)
    tri = jnp.where(same_chunk & (ti >= tj), 1.0, 0.0).astype(BF16)
    tot = jnp.where(same_chunk, 1.0, 0.0).astype(BF16)
    parts = _split3(logw)
    cum = sum(_dot(tri, part) for part in parts)
    cum_end = sum(_dot(tot, part) for part in parts)
    e_bwd = jnp.exp(-cum)
    e_end = jnp.exp(cum_end - cum)
    kka = kk * a
    a_t = -kk * jnp.exp(cum - logw)
    b_t = kka * e_bwd
    k_t = k2 * e_bwd
    r_t = r * jnp.exp(cum)
    b_h = kka * e_end
    k_h = k2 * e_end
    p_end = jnp.exp(cum_end)

    ri = lax.broadcasted_iota(jnp.int32, (rows, rows), 0)
    ci = lax.broadcasted_iota(jnp.int32, (rows, rows), 1)
    same = (ri >> lg) == (ci >> lg)
    dpos = (ri & (chunk - 1)) - (ci & (chunk - 1))
    strict = same & (dpos > 0)
    incl = same & (dpos >= 0)
    eye = jnp.where(ri == ci, 1.0, 0.0)

    def prepare(c):
        sl = slice(c * chunk, (c + 1) * chunk)
        st = lambda x: _stack_heads(x[sl].astype(BF16), masks)
        a_st, b_st, k_st, r_st, v_st, bh_st, kh_st = st(a_t), st(b_t), st(k_t), st(r_t), st(v), st(b_h), st(k_h)
        n_bd = jnp.where(strict, _dot_nt(a_st, b_st), 0.0)
        ak_bd = jnp.where(strict, _dot_nt(a_st, k_st), 0.0)
        m_rb = jnp.where(incl, _dot_nt(r_st, b_st), 0.0).astype(BF16)
        m_rk = jnp.where(incl, _dot_nt(r_st, k_st), 0.0).astype(BF16)
        t_inv = eye + n_bd
        pw = n_bd
        for _ in range(lg - 1):
            pwb = pw.astype(BF16)
            pw = _dot(pwb, pwb)
            t_inv = t_inv + _dot(t_inv.astype(BF16), pw.astype(BF16))
        tbf = t_inv.astype(BF16)
        w_st = _dot(tbf, a_st).astype(BF16)
        y_st = _dot(tbf, _dot(ak_bd.astype(BF16), v_st).astype(BF16))
        o_pre = _dot(m_rk, v_st)
        g_pre = _dot_tn(v_st, kh_st)
        return w_st, y_st, r_st, m_rb, o_pre, bh_st, g_pre, p_end[c * chunk:c * chunk + 1]

    prepared = [prepare(c) for c in range(n_sub)]

    state = st_sc[...]
    o_chunks = []
    for w_st, y_st, r_st, m_rb, o_pre, bh_st, g_pre, p_c in prepared:
        sb = state.astype(BF16)
        u = _dot_nt(w_st, sb) + y_st
        ub = u.astype(BF16)
        o_chunks.append(_unstack_heads(_dot_nt(r_st, sb) + _dot(m_rb, ub) + o_pre, chunk))
        state = state * p_c + _dot_tn(ub, bh_st) + g_pre
    st_sc[...] = state

    o = o_chunks[0] if n_sub == 1 else jnp.concatenate(o_chunks, axis=0)
    inv = 1.0 / D_HEAD
    mean = _head_sum(o, ones_bd) * inv
    d = o - mean
    var = _head_sum(d * d, ones_bd) * inv
    o_n = d * lax.rsqrt(var + RWKV_GN_EPS) * lng_ref[...] + lnb_ref[...]
    bonus = _head_sum(r * k2 * rk_ref[...], ones_bd) * v
    o_ref[...] = ((o_n + bonus) * g).astype(BF16)

    @pl.when(c_id == n_steps - 1)
    def _():
        _store_block_diag(sfin_ref, state)


def _rwkv(pc, shift0, s0, params, batch, t_pad, chunk, n_sub, n_valid):
    tb = chunk * n_sub
    n_steps = t_pad // tb
    rowspec = lambda w: pl.BlockSpec((tb, w), lambda b, c: (b * n_steps + c, 0))
    stspec = pl.BlockSpec((None, N_HEADS, D_HEAD, D_HEAD), lambda b, c: (b, 0, 0, 0))
    const = lambda arr: pl.BlockSpec(arr.shape, lambda b, c: (0,) * arr.ndim)
    return pl.pallas_call(
        functools.partial(_rwkv_kernel, chunk=chunk, n_sub=n_sub, n_valid=n_valid, n_steps=n_steps),
        grid=(batch, n_steps),
        in_specs=[rowspec(PROJ_C), pl.BlockSpec((None, 1, PROJ_C), lambda b, c: (b, 0, 0)), stspec]
                 + [const(p) for p in params],
        out_specs=[rowspec(W_BC), stspec],
        out_shape=[jax.ShapeDtypeStruct((batch * t_pad, W_BC), BF16),
                   jax.ShapeDtypeStruct((batch, N_HEADS, D_HEAD, D_HEAD), F32)],
        scratch_shapes=[pltpu.VMEM((W_BC, W_BC), F32), pltpu.VMEM((1, PROJ_C), F32)],
        compiler_params=_cparams("parallel", "arbitrary"),
        name="rwkv7",
    )(pc, shift0, s0, *params)


def _out_proj_kernel(x_ref, a_ref, b_ref, c_ref, w_ref, g_ref, bb_ref, y_ref, *, alpha):
    acc = _dot(a_ref[...], w_ref[0:W_A, :])
    acc += _dot(b_ref[...], w_ref[W_A:W_A + W_BC, :])
    acc += _dot(c_ref[...], w_ref[W_A + W_BC:W_A + 2 * W_BC, :])
    y_ref[...] = _layer_norm(alpha * x_ref[...] + acc, g_ref[...], bb_ref[...])


def _out_proj(x2d, oa, ob, oc, w_bf, g, b, alpha, tm):
    m = x2d.shape[0]
    row = lambda w: pl.BlockSpec((tm, w), lambda i: (i, 0))
    const = lambda shape: pl.BlockSpec(shape, lambda i: (0, 0))
    return pl.pallas_call(
        functools.partial(_out_proj_kernel, alpha=alpha),
        grid=(m // tm,),
        in_specs=[row(D_MODEL), row(W_A), row(W_BC), row(W_BC), const((D_MODEL, D_MODEL)),
                  const((1, D_MODEL)), const((1, D_MODEL))],
        out_specs=row(D_MODEL),
        out_shape=jax.ShapeDtypeStruct((m, D_MODEL), F32),
        compiler_params=_cparams("parallel"),
        name="out_proj",
    )(x2d, oa, ob, oc, w_bf, g.reshape(1, -1), b.reshape(1, -1))


def _ffn_kernel(x_ref, w1_ref, w3_ref, w2_ref, g_ref, b_ref, y_ref, acc_sc, *, alpha):
    f = pl.program_id(1)
    x = x_ref[...]
    xb = x.astype(BF16)

    @pl.when(f == 0)
    def _():
        acc_sc[...] = jnp.zeros(acc_sc.shape, F32)

    h = jax.nn.silu(_dot(xb, w1_ref[...])) * _dot(xb, w3_ref[...])
    acc_sc[...] += _dot(h.astype(BF16), w2_ref[...])

    @pl.when(f == pl.num_programs(1) - 1)
    def _():
        y_ref[...] = _layer_norm(alpha * x + acc_sc[...], g_ref[...], b_ref[...])


def _ffn(x2d, w1, w3, w2, g, b, alpha, tm, tf):
    m = x2d.shape[0]
    d_ff = w1.shape[1]
    return pl.pallas_call(
        functools.partial(_ffn_kernel, alpha=alpha),
        grid=(m // tm, d_ff // tf),
        in_specs=[pl.BlockSpec((tm, D_MODEL), lambda i, f: (i, 0)),
                  pl.BlockSpec((D_MODEL, tf), lambda i, f: (0, f)),
                  pl.BlockSpec((D_MODEL, tf), lambda i, f: (0, f)),
                  pl.BlockSpec((tf, D_MODEL), lambda i, f: (f, 0)),
                  pl.BlockSpec((1, D_MODEL), lambda i, f: (0, 0)),
                  pl.BlockSpec((1, D_MODEL), lambda i, f: (0, 0))],
        out_specs=pl.BlockSpec((tm, D_MODEL), lambda i, f: (i, 0)),
        out_shape=jax.ShapeDtypeStruct((m, D_MODEL), F32),
        scratch_shapes=[pltpu.VMEM((tm, D_MODEL), F32)],
        compiler_params=_cparams("parallel", "arbitrary"),
        name="ffn",
    )(x2d, w1, w3, w2, g.reshape(1, -1), b.reshape(1, -1))


def _moe_kernel(x_ref, rh_ref, rl_ref, w1_ref, w3_ref, w2_ref, g_ref, b_ref, y_ref, acc_sc, gate_sc, xb_sc,
                *, alpha, n_experts):
    e = pl.program_id(1)
    lane = lax.broadcasted_iota(jnp.int32, (1, LANES), 1)

    @pl.when(e == 0)
    def _():
        x = x_ref[...]
        xh, xl = _split2(x)
        xb_sc[...] = xh
        logits = _dot(xh, rh_ref[...]) + _dot(xl, rh_ref[...]) + _dot(xh, rl_ref[...])
        logits = jnp.where(lane < n_experts, logits, NEG)
        v1 = jnp.max(logits, axis=-1, keepdims=True)
        i1 = jnp.min(jnp.where(logits == v1, lane, LANES), axis=-1, keepdims=True)
        rest = jnp.where(lane == i1, NEG, logits)
        v2 = jnp.max(rest, axis=-1, keepdims=True)
        i2 = jnp.min(jnp.where(rest == v2, lane, LANES), axis=-1, keepdims=True)
        ex = jnp.exp(v2 - v1)
        g1 = 1.0 / (1.0 + ex)
        g2 = ex / (1.0 + ex)
        gate_sc[...] = jnp.where(lane == i1, g1, 0.0) + jnp.where(lane == i2, g2, 0.0)
        acc_sc[...] = jnp.zeros(acc_sc.shape, F32)

    xb = xb_sc[...]
    h = jax.nn.silu(_dot(xb, w1_ref[...])) * _dot(xb, w3_ref[...])
    f = _dot(h.astype(BF16), w2_ref[...])
    ge = jnp.sum(jnp.where(lane == e, gate_sc[...], 0.0), axis=-1, keepdims=True)
    acc_sc[...] += ge * f

    @pl.when(e == n_experts - 1)
    def _():
        y_ref[...] = _layer_norm(alpha * x_ref[...] + acc_sc[...], g_ref[...], b_ref[...])


def _moe(x2d, router, w1, w3, w2, g, b, alpha, tm):
    m = x2d.shape[0]
    n_experts, _, d_e = w1.shape
    r_pad = jnp.zeros((D_MODEL, LANES), F32).at[:, :n_experts].set(router)
    r_hi = r_pad.astype(BF16)
    r_lo = (r_pad - r_hi.astype(F32)).astype(BF16)
    return pl.pallas_call(
        functools.partial(_moe_kernel, alpha=alpha, n_experts=n_experts),
        grid=(m // tm, n_experts),
        in_specs=[pl.BlockSpec((tm, D_MODEL), lambda i, e: (i, 0)),
                  pl.BlockSpec((D_MODEL, LANES), lambda i, e: (0, 0)),
                  pl.BlockSpec((D_MODEL, LANES), lambda i, e: (0, 0)),
                  pl.BlockSpec((None, D_MODEL, d_e), lambda i, e: (e, 0, 0)),
                  pl.BlockSpec((None, D_MODEL, d_e), lambda i, e: (e, 0, 0)),
                  pl.BlockSpec((None, d_e, D_MODEL), lambda i, e: (e, 0, 0)),
                  pl.BlockSpec((1, D_MODEL), lambda i, e: (0, 0)),
                  pl.BlockSpec((1, D_MODEL), lambda i, e: (0, 0))],
        out_specs=pl.BlockSpec((tm, D_MODEL), lambda i, e: (i, 0)),
        out_shape=jax.ShapeDtypeStruct((m, D_MODEL), F32),
        scratch_shapes=[pltpu.VMEM((tm, D_MODEL), F32), pltpu.VMEM((tm, LANES), F32),
                        pltpu.VMEM((tm, D_MODEL), BF16)],
        compiler_params=_cparams("parallel", "arbitrary"),
        name="moe",
    )(x2d, r_hi, r_lo, w1, w3, w2, g.reshape(1, -1), b.reshape(1, -1))


def _rotary_tables(pos0, t_pad):
    half = D_HEAD // 2
    inv = 1.0 / (10000.0 ** (jnp.arange(half, dtype=F32) / half))
    ang = (pos0 + jnp.arange(t_pad, dtype=jnp.int32)).astype(F32)[:, None] * inv[None, :]
    cos, sin = jnp.cos(ang), jnp.sin(ang)
    cos_t = jnp.tile(jnp.concatenate([cos, cos], axis=-1), (1, N_HEADS))
    sin_t = jnp.tile(jnp.concatenate([-sin, sin], axis=-1), (1, N_HEADS))
    return cos_t, sin_t


def _pad_rows(lo, arr):
    return jnp.zeros((W_BC, W_BC), F32).at[lo:lo + arr.shape[0]].set(arr).astype(BF16)


def _ffn_tile(d_ff):
    best = LANES if d_ff % LANES == 0 else d_ff
    for t in range(LANES, min(d_ff, 1536) + 1, LANES):
        if d_ff % t == 0:
            best = t
    return best


def kernel(x_prompt, x_sample, cache_k, cache_v, page_table, state_ret, state_wkv, state_shift, rel_bias, w_in, w_out, lambda_q1, lambda_k1, lambda_q2, lambda_k2, subln_g, tshift_mu, decay_w0, decay_w2, iclr_a0, iclr_a2, gate_w2, k_k, k_a, r_k, lnx_g, lnx_b, ln1_g, ln1_b, ln2_g, ln2_b, ffn_w1, ffn_w3, ffn_w2, router_w, expert_w1, expert_w3, expert_w2):
    bp, seq, _ = x_prompt.shape
    n_dec, t_new, _ = x_sample.shape
    depth, n_pool, page, _, _ = cache_k.shape
    n_pages = page_table.shape[1]
    past = n_pages * page
    alpha = (2 * depth) ** 0.25

    mp, ms = bp * seq, n_dec * t_new
    tm_p = 256 if mp % 256 == 0 else mp
    tm_s = ms
    blk = 256 if seq % 256 == 0 else seq
    chunk_p = 64 if seq % 64 == 0 else seq
    sub_p = 4 if seq % (4 * chunk_p) == 0 else 1
    chunk_s = 32
    group = 16 if n_pages % 16 == 0 else (8 if n_pages % 8 == 0 else 1)

    cache_k4 = cache_k.reshape(depth, n_pool, page * N_HEADS, 2 * D_HEAD_A)
    cache_v4 = cache_v.reshape(depth, n_pool, page * N_HEADS, 2 * D_HEAD_A)
    pt_flat = page_table.reshape(-1).astype(jnp.int32)

    rel_bias = rel_bias.astype(F32)
    near_bias = _near_bias_tiles(rel_bias, blk)
    sample_tiles = _sample_bias_tiles(rel_bias, t_new, n_pages, page)
    cos_p, sin_p = _rotary_tables(0, seq)
    cos_s, sin_s = _rotary_tables(past, chunk_s)
    zero_st_p = jnp.zeros((bp, N_HEADS, D_HEAD, D_HEAD), F32)
    zero_shift_p = jnp.zeros((bp, 1, PROJ_C), F32)

    def pad_sample(a):
        w = a.shape[-1]
        a3 = a.reshape(n_dec, t_new, w)
        return jnp.pad(a3, ((0, 0), (0, chunk_s - t_new), (0, 0))).reshape(n_dec * chunk_s, w)

    def unpad_sample(a):
        w = a.shape[-1]
        return a.reshape(n_dec, chunk_s, w)[:, :t_new].reshape(n_dec * t_new, w)

    xp = x_prompt.reshape(mp, D_MODEL)
    xs = x_sample.reshape(ms, D_MODEL)
    kp_all = vp_all = ks_all = vs_all = None
    outs = {k: [] for k in ("rp", "rs", "wp", "ws", "sp", "ss")}
    for l in range(depth):
        w_in_bf = w_in[l].astype(BF16)
        w_out_bf = w_out[l].astype(BF16)
        lam_init = 0.8 - 0.6 * math.exp(-0.3 * l)
        lamv = jnp.stack([lambda_q1[l], lambda_k1[l], lambda_q2[l], lambda_k2[l]]).astype(F32)
        laminit = jnp.full((1, 1), lam_init, F32)
        row = lambda a: a.reshape(1, -1).astype(F32)
        rwkv_params = [row(tshift_mu[l]), row(decay_w0[l]), _pad_rows(0, decay_w2[l]), row(iclr_a0[l]),
                       _pad_rows(64, iclr_a2[l]), _pad_rows(128, gate_w2[l]),
                       row(k_k[l]), row(k_a[l]), row(r_k[l]), row(lnx_g[l]), row(lnx_b[l])]

        qa, kp_all, vp_all, kb, vb, pb, pc = _proj_in(xp, w_in_bf, kp_all, vp_all, l, depth, tm_p)
        oa = _attn_prompt(qa, kb, vb, lamv, laminit, near_bias, subln_g[l], bp, seq, blk)
        ob, ret_p = _retention(pb, cos_p, sin_p, zero_st_p, bp, seq, chunk_p, sub_p, chunk_p)
        oc, wkv_p = _rwkv(pc, zero_shift_p, zero_st_p, rwkv_params, bp, seq, chunk_p, sub_p, chunk_p)
        xp = _out_proj(xp, oa, ob, oc, w_out_bf, ln1_g[l], ln1_b[l], alpha, 512 if mp % 512 == 0 else tm_p)
        outs["rp"].append(ret_p)
        outs["wp"].append(wkv_p)
        outs["sp"].append(pc.reshape(bp, seq, PROJ_C)[:, -1])

        qa, ks_all, vs_all, kb, vb, pb, pc = _proj_in(xs, w_in_bf, ks_all, vs_all, l, depth, tm_s)
        oa = _attn_sample(qa, kb, vb, cache_k4, cache_v4, pt_flat, l, lamv, laminit, sample_tiles, subln_g[l],
                          n_dec, t_new, n_pages, page, group)
        ob, ret_s = _retention(pad_sample(pb), cos_s, sin_s, state_ret[l], n_dec, chunk_s, chunk_s, 1, t_new)
        oc, wkv_s = _rwkv(pad_sample(pc), state_shift[l].astype(F32).reshape(n_dec, 1, PROJ_C),
                          state_wkv[l], rwkv_params, n_dec, chunk_s, chunk_s, 1, t_new)
        xs = _out_proj(xs, oa, unpad_sample(ob), unpad_sample(oc), w_out_bf, ln1_g[l], ln1_b[l], alpha, tm_s)
        outs["rs"].append(ret_s)
        outs["ws"].append(wkv_s)
        outs["ss"].append(pc.reshape(n_dec, t_new, PROJ_C)[:, -1])

        j = l // 2
        if l % 2 == 0:
            w1, w3, w2 = ffn_w1[j].astype(BF16), ffn_w3[j].astype(BF16), ffn_w2[j].astype(BF16)
            tf = _ffn_tile(w1.shape[1])
            xp = _ffn(xp, w1, w3, w2, ln2_g[l], ln2_b[l], alpha, min(512, mp), tf)
            xs = _ffn(xs, w1, w3, w2, ln2_g[l], ln2_b[l], alpha, tm_s, tf)
        else:
            w1, w3, w2 = expert_w1[j].astype(BF16), expert_w3[j].astype(BF16), expert_w2[j].astype(BF16)
            xp = _moe(xp, router_w[j], w1, w3, w2, ln2_g[l], ln2_b[l], alpha, min(512, mp))
            xs = _moe(xs, router_w[j], w1, w3, w2, ln2_g[l], ln2_b[l], alpha, tm_s)

    st = lambda key: jnp.stack(outs[key], axis=0)
    kv_p = lambda a: a.reshape(depth, bp, seq, N_HEADS, 2 * D_HEAD_A)
    kv_s = lambda a: a.reshape(depth, n_dec, t_new, N_HEADS, 2 * D_HEAD_A)
    return (xp.reshape(bp, seq, D_MODEL), xs.reshape(n_dec, t_new, D_MODEL),
            kv_p(kp_all), kv_p(vp_all), kv_s(ks_all), kv_s(vs_all),
            st("rp"), st("rs"), st("wp"), st("ws"), st("sp"), st("ss"))
```

```python
import functools
import math

import jax
import jax.numpy as jnp
from jax import lax
from jax.experimental import pallas as pl
from jax.experimental.pallas import tpu as pltpu

F32 = jnp.float32
BF16 = jnp.bfloat16

D_MODEL = 1024
N_HEADS = 4
D_HEAD_A = 64
W_A = N_HEADS * 2 * D_HEAD_A
D_HEAD = 64
W_BC = N_HEADS * D_HEAD
PROJ_A = 3 * W_A
PROJ_B = 4 * W_BC
PROJ_C = 1024
N_BUCKETS = 32
MAX_DISTANCE = 128
NEAR_DIST = 113
RWKV_GN_EPS = 64e-5
LN_EPS = 1e-5
RMS_EPS = 1e-5
NEG = -1e30
LOG2E = 1.4426950408889634

VMEM_LIMIT_BYTES = 56 * 1024 * 1024
LANES = 128


def _cparams(*sem):
    return pltpu.CompilerParams(dimension_semantics=sem, vmem_limit_bytes=VMEM_LIMIT_BYTES)


def _dot(a, b):
    return jnp.dot(a, b, preferred_element_type=F32)


def _dot_nt(a, b):
    return lax.dot_general(a, b, (((1,), (1,)), ((), ())), preferred_element_type=F32)


def _dot_tn(a, b):
    return lax.dot_general(a, b, (((0,), (0,)), ((), ())), preferred_element_type=F32)


def _split2(x):
    hi = x.astype(BF16)
    lo = (x - hi.astype(F32)).astype(BF16)
    return hi, lo


def _split3(x):
    hi = x.astype(BF16)
    r1 = x - hi.astype(F32)
    mid = r1.astype(BF16)
    lo = (r1 - mid.astype(F32)).astype(BF16)
    return hi, mid, lo


def _softplus(z):
    return jnp.maximum(z, 0.0) + jnp.log1p(jnp.exp(-jnp.abs(z)))


def _layer_norm(z, g, b):
    mu = jnp.mean(z, axis=-1, keepdims=True)
    d = z - mu
    var = jnp.mean(d * d, axis=-1, keepdims=True)
    return d * lax.rsqrt(var + LN_EPS) * g + b


def _head_masks():
    lane = lax.broadcasted_iota(jnp.int32, (1, W_BC), 1)
    return [(lane >= h * D_HEAD) & (lane < (h + 1) * D_HEAD) for h in range(N_HEADS)]


def _stack_heads(x, masks):
    zero = jnp.zeros((), x.dtype)
    return jnp.concatenate([jnp.where(m, x, zero) for m in masks], axis=0)


def _unstack_heads(z, c):
    return z[0:c] + z[c:2 * c] + z[2 * c:3 * c] + z[3 * c:4 * c]


def _load_block_diag(s_ref):
    rows = []
    for h in range(N_HEADS):
        pieces = [jnp.zeros((D_HEAD, D_HEAD), F32)] * N_HEADS
        pieces[h] = s_ref[h].astype(F32)
        rows.append(jnp.concatenate(pieces, axis=1))
    return jnp.concatenate(rows, axis=0)


def _store_block_diag(s_ref, state):
    for h in range(N_HEADS):
        s_ref[h] = state[h * D_HEAD:(h + 1) * D_HEAD, h * D_HEAD:(h + 1) * D_HEAD]


def _head_ones():
    r = lax.broadcasted_iota(jnp.int32, (W_BC, W_BC), 0)
    c = lax.broadcasted_iota(jnp.int32, (W_BC, W_BC), 1)
    return jnp.where((r // D_HEAD) == (c // D_HEAD), 1.0, 0.0).astype(BF16)


def _head_sum(x, ones_bd):
    hi, lo = _split2(x)
    return _dot(hi, ones_bd) + _dot(lo, ones_bd)


def _proj_in_kernel(x_ref, w_ref, k_in_ref, v_in_ref, qa_ref, k_ref, v_ref, kb_ref, vb_ref, pb_ref, pc_ref):
    del k_in_ref, v_in_ref
    xb = x_ref[...].astype(BF16)

    def mm(lo, hi):
        return _dot(xb, w_ref[:, lo:hi])

    qa_ref[...] = (mm(0, W_A) * (D_HEAD_A ** -0.5 * LOG2E)).astype(BF16)
    tm = xb.shape[0]
    dv = 2 * D_HEAD_A
    for ref, bref, lo in ((k_ref, kb_ref, W_A), (v_ref, vb_ref, 2 * W_A)):
        val = mm(lo, lo + W_A)
        bref[...] = val.astype(BF16)
        for h in range(N_HEADS):
            ref[pl.ds(h, tm, stride=N_HEADS), :] = val[:, h * dv:(h + 1) * dv]
    pb_ref[...] = mm(PROJ_A, PROJ_A + PROJ_B)
    pc_ref[...] = mm(PROJ_A + PROJ_B, PROJ_A + PROJ_B + PROJ_C)


def _proj_in(x2d, w_bf, k_all, v_all, layer, depth, tm):
    m = x2d.shape[0]
    n = w_bf.shape[1]
    row = lambda i: (i, 0)
    stack_spec = pl.BlockSpec((None, tm * N_HEADS, 2 * D_HEAD_A), lambda i: (layer, i, 0))
    stack_shape = jax.ShapeDtypeStruct((depth, m * N_HEADS, 2 * D_HEAD_A), F32)
    return pl.pallas_call(
        _proj_in_kernel,
        grid=(m // tm,),
        in_specs=[pl.BlockSpec((tm, D_MODEL), row), pl.BlockSpec((D_MODEL, n), lambda i: (0, 0)),
                  pl.BlockSpec(memory_space=pl.ANY), pl.BlockSpec(memory_space=pl.ANY)],
        out_specs=[pl.BlockSpec((tm, W_A), row), stack_spec, stack_spec,
                   pl.BlockSpec((tm, W_A), row), pl.BlockSpec((tm, W_A), row),
                   pl.BlockSpec((tm, PROJ_B), row), pl.BlockSpec((tm, PROJ_C), row)],
        out_shape=[jax.ShapeDtypeStruct((m, W_A), BF16), stack_shape, stack_shape,
                   jax.ShapeDtypeStruct((m, W_A), BF16),
                   jax.ShapeDtypeStruct((m, W_A), BF16), jax.ShapeDtypeStruct((m, PROJ_B), F32),
                   jax.ShapeDtypeStruct((m, PROJ_C), F32)],
        input_output_aliases={2: 1, 3: 2},
        compiler_params=_cparams("parallel"),
        name="proj_in",
    )(x2d, w_bf, k_all, v_all)


def _lambda_full(lamv_ref, laminit_ref):
    lv = lamv_ref[...]
    s1 = jnp.sum(lv[0:1] * lv[1:2], axis=-1, keepdims=True)
    s2 = jnp.sum(lv[2:3] * lv[3:4], axis=-1, keepdims=True)
    return jnp.exp(s1) - jnp.exp(s2) + laminit_ref[...]


def _sub_rms(o, g, scale):
    y = o * lax.rsqrt(jnp.mean(o * o, axis=-1, keepdims=True) + RMS_EPS)
    return y * g * scale


def _rel_bucket(n):
    max_exact = N_BUCKETS // 2
    large = max_exact + (jnp.log(jnp.maximum(n, 1).astype(F32) / max_exact)
                         / math.log(MAX_DISTANCE / max_exact) * (N_BUCKETS - max_exact)).astype(jnp.int32)
    large = jnp.minimum(large, N_BUCKETS - 1)
    return jnp.where(n < max_exact, n, large)


V_AUG = 2 * D_HEAD_A + 16
PAIR = 2
HPS = 4


def _attn_prompt_kernel(lamv_ref, laminit_ref, q_ref, k_ref, v_ref, near_ref, g_ref, o_ref, *scratch, blk, seq):
    i = pl.program_id(2)
    dv = 2 * D_HEAD_A
    kp = PAIR * blk
    per_head = 1 + 2 + 2 + 4 + 4
    heads = []
    for h in range(HPS):
        sc = scratch[h * per_head:(h + 1) * per_head]
        heads.append(dict(vt=sc[0], m=sc[1:3], acc=sc[3:5], s=(sc[5:7], sc[7:9]), cm=(sc[9:11], sc[11:13])))
    lane = lax.broadcasted_iota(jnp.int32, (1, dv), 1)
    lo = lane < D_HEAD_A
    zero = jnp.zeros((), BF16)

    @pl.when(i == 0)
    def _():
        for h, hd in enumerate(heads):
            for c in range(seq // blk):
                vt = v_ref[c * blk:(c + 1) * blk, h * dv:(h + 1) * dv].astype(F32).T
                hd["vt"][0:dv, c * blk:(c + 1) * blk] = vt.astype(BF16)
            hd["vt"][dv:V_AUG, :] = jnp.ones((V_AUG - dv, seq), BF16)

    for h, hd in enumerate(heads):
        q = q_ref[:, h * dv:(h + 1) * dv]
        hd["q"] = (jnp.where(lo, q, zero), jnp.where(lo, zero, q))
        for mp in range(2):
            hd["m"][mp][...] = jnp.full(hd["m"][mp].shape, NEG, F32)
            hd["acc"][mp][...] = jnp.zeros(hd["acc"][mp].shape, F32)

    def stage_a(pair, slot, near_idx):
        off = pl.multiple_of(pair * kp, kp)
        for h, hd in enumerate(heads):
            cmax = [None, None]
            for sub in range(PAIR):
                kblk = k_ref[pl.ds(off + sub * blk, blk), h * dv:(h + 1) * dv]
                for mp in range(2):
                    s = _dot_nt(kblk, hd["q"][mp])
                    if near_idx is not None:
                        s = s + near_ref[h, near_idx, sub * blk:(sub + 1) * blk, :]
                    hd["s"][slot][mp][sub] = s
                    smax = jnp.max(s, axis=0, keepdims=True)
                    cmax[mp] = smax if cmax[mp] is None else jnp.maximum(cmax[mp], smax)
            for mp in range(2):
                hd["cm"][slot][mp][...] = cmax[mp]

    def stage_b(pair, slot):
        off = pl.multiple_of(pair * kp, kp)
        for hd in heads:
            vtb = hd["vt"][:, pl.ds(off, kp)]
            for mp in range(2):
                m_old = hd["m"][mp][...]
                mn = jnp.maximum(m_old, hd["cm"][slot][mp][...])
                p = jnp.concatenate([jnp.exp2(hd["s"][slot][mp][sub] - mn).astype(BF16) for sub in range(PAIR)],
                                    axis=0)
                hd["m"][mp][...] = mn
                hd["acc"][mp][...] = jnp.exp2(m_old - mn) * hd["acc"][mp][...] + _dot(vtb, p)

    n_pairs = (i + 2) // 2
    t1 = n_pairs - 1
    t0 = jnp.maximum(n_pairs - 2, 0)
    n_far = t0
    stage_a(t1, 0, 1)
    stage_a(t0, 1, 0)
    stage_b(t1, 0)

    def far_body(j, carry):
        stage_a(2 * j, 0, None)
        stage_b(jnp.where(j == 0, t0, 2 * j - 1), 1)
        stage_a(2 * j + 1, 1, None)
        stage_b(2 * j, 0)
        return carry

    lax.fori_loop(0, n_far // 2, far_body, 0)
    pending = jnp.where(n_far < 2, t0, ((n_far // 2) * 2) - 1)

    @pl.when(n_far % 2 == 1)
    def _():
        stage_a(n_far - 1, 0, None)
        stage_b(pending, 1)
        stage_b(n_far - 1, 0)

    @pl.when(n_far % 2 == 0)
    def _():
        stage_b(pending, 1)

    lam = _lambda_full(lamv_ref, laminit_ref)
    for h, hd in enumerate(heads):
        a1, a2 = hd["acc"][0][...], hd["acc"][1][...]
        o_t = a1[0:dv] / a1[dv:dv + 1] - lam * (a2[0:dv] / a2[dv:dv + 1])
        o_ref[:, h * dv:(h + 1) * dv] = _sub_rms(o_t.T, g_ref[...], 1.0 - laminit_ref[...]).astype(BF16)


def _near_bias_tiles(rel_bias, blk):
    cidx = jnp.arange(blk, dtype=jnp.int32)[:, None]
    ridx = jnp.arange(blk, dtype=jnp.int32)[None, :]
    rel = (rel_bias - rel_bias[N_BUCKETS - 1][None, :]) * LOG2E

    def lookup(dist):
        onehot = (_rel_bucket(dist)[..., None] == jnp.arange(N_BUCKETS, dtype=jnp.int32)).astype(F32)
        return jnp.einsum("crb,bh->hcr", onehot, rel, precision=lax.Precision.HIGHEST)

    dist0 = ridx - cidx
    diag = jnp.where((dist0 >= 0)[None], lookup(jnp.maximum(dist0, 0)), NEG)
    prev = lookup(dist0 + blk)
    masked = jnp.full_like(diag, NEG)
    zeros = jnp.zeros_like(diag)
    cat = lambda a, b: jnp.concatenate([a, b], axis=1)
    variants = [(cat(masked, masked), cat(diag, masked)), (cat(masked, masked), cat(prev, diag)),
                (cat(zeros, prev), cat(diag, masked)), (cat(zeros, zeros), cat(prev, diag))]
    return jnp.stack([jnp.stack(v, axis=1) for v in variants], axis=1)


def _attn_prompt(qa, kb, vb, lamv, laminit, near_bias, subln_g, batch, seq, blk):
    m = batch * seq
    nq = seq // blk
    dv = 2 * D_HEAD_A
    assert blk >= NEAR_DIST - 1 and nq % PAIR == 0 and N_HEADS % HPS == 0
    qspec = pl.BlockSpec((blk, HPS * dv), lambda b, h, i: (b * nq + i, h))
    kvspec = pl.BlockSpec((seq, HPS * dv), lambda b, h, i: (b, h))
    near = pl.BlockSpec((HPS, None, 2, PAIR * blk, blk),
                        lambda b, h, i: (h, jnp.where(i < 2, i, 2 + (i & 1)), 0, 0, 0))
    const2 = lambda shape: pl.BlockSpec(shape, lambda b, h, i: (0, 0))
    per_head = ([pltpu.VMEM((V_AUG, seq), BF16)] + [pltpu.VMEM((1, blk), F32)] * 2
                + [pltpu.VMEM((V_AUG, blk), F32)] * 2 + [pltpu.VMEM((PAIR, blk, blk), F32)] * 4
                + [pltpu.VMEM((1, blk), F32)] * 4)
    return pl.pallas_call(
        functools.partial(_attn_prompt_kernel, blk=blk, seq=seq),
        grid=(batch, N_HEADS // HPS, nq),
        in_specs=[const2((4, D_HEAD_A)), const2((1, 1)),
                  qspec, kvspec, kvspec, near, const2((1, dv))],
        out_specs=qspec,
        out_shape=jax.ShapeDtypeStruct((m, W_A), BF16),
        scratch_shapes=per_head * HPS,
        compiler_params=_cparams("parallel", "parallel", "arbitrary"),
        name="attn_prompt",
    )(lamv, laminit, qa, kb, vb, near_bias, subln_g.reshape(1, -1))


def _attn_sample_kernel(pt_ref, lamv_ref, laminit_ref, q_ref, kn_ref, vn_ref, bfar_ref, blast_ref, bnew_ref, g_ref,
                        *refs, group, n_steps, rows):
    k_refs = refs[:group]
    v_refs = refs[group:2 * group]
    o_ref = refs[2 * group]
    m_sc, l_sc, acc_sc = refs[2 * group + 1:]
    s_id = pl.program_id(1)
    cols = k_refs[0].shape[0]

    q = q_ref[...]
    lane = lax.broadcasted_iota(jnp.int32, (1, 2 * D_HEAD_A), 1)
    lo = lane < D_HEAD_A
    zero = jnp.zeros((), BF16)
    wq = jnp.concatenate([jnp.where(lo, q, zero), jnp.where(lo, zero, q)], axis=0)

    def update(s, pv_of):
        m = m_sc[...]
        mn = jnp.maximum(m, jnp.max(s, axis=-1, keepdims=True))
        al = jnp.exp2(m - mn)
        p = jnp.exp2(s - mn)
        m_sc[...] = mn
        l_sc[...] = al * l_sc[...] + jnp.sum(p, axis=-1, keepdims=True)
        acc_sc[...] = al * acc_sc[...] + pv_of(p.astype(BF16))

    @pl.when(s_id == 0)
    def _():
        m_sc[...] = jnp.full(m_sc.shape, NEG, F32)
        l_sc[...] = jnp.zeros(l_sc.shape, F32)
        acc_sc[...] = jnp.zeros(acc_sc.shape, F32)
        update(_dot_nt(wq, kn_ref[...]) + bnew_ref[...], lambda p: _dot(p, vn_ref[...]))

    bfar = bfar_ref[...]
    parts = []
    for g in range(group):
        bias = bfar
        if g == group - 1:
            bias = jnp.where(s_id == n_steps - 1, blast_ref[...], bfar)
        parts.append(_dot_nt(wq, k_refs[g][...].astype(BF16)) + bias)

    def pv_pages(p):
        return sum(_dot(p[:, g * cols:(g + 1) * cols], v_refs[g][...].astype(BF16)) for g in range(group))

    update(jnp.concatenate(parts, axis=1), pv_pages)

    @pl.when(s_id == n_steps - 1)
    def _():
        lam = _lambda_full(lamv_ref, laminit_ref)
        o = acc_sc[...] / l_sc[...]
        od = o[0:rows] - lam * o[rows:2 * rows]
        o_ref[...] = _sub_rms(od, g_ref[...], 1.0 - laminit_ref[...]).astype(BF16)


def _sample_bias_tiles(rel_bias, t_new, n_pages, page):
    rows = t_new * N_HEADS
    cols = page * N_HEADS
    past = n_pages * page
    assert page + 1 >= NEAR_DIST
    rel = (rel_bias - rel_bias[N_BUCKETS - 1][None, :]) * LOG2E
    r = jnp.arange(2 * rows, dtype=jnp.int32)[:, None]
    rt, rh = (r % rows) // N_HEADS, r % N_HEADS
    rel_rows = rel[:, rh[:, 0]]

    def lookup(dist):
        onehot = (_rel_bucket(dist)[..., None] == jnp.arange(N_BUCKETS, dtype=jnp.int32)).astype(F32)
        return jnp.einsum("rcb,br->rc", onehot, rel_rows, precision=lax.Precision.HIGHEST)

    c = jnp.arange(cols, dtype=jnp.int32)[None, :]
    ct, ch = c // N_HEADS, c % N_HEADS
    same = rh == ch
    bfar = jnp.where(same, 0.0, NEG).astype(F32)
    dist_last = (past + rt) - ((n_pages - 1) * page + ct)
    blast = jnp.where(same, lookup(dist_last), NEG).astype(F32)
    cn = jnp.arange(rows, dtype=jnp.int32)[None, :]
    cnt, cnh = cn // N_HEADS, cn % N_HEADS
    dist_new = rt - cnt
    bnew = jnp.where((rh == cnh) & (dist_new >= 0), lookup(jnp.maximum(dist_new, 0)), NEG).astype(F32)
    return bfar, blast, bnew


def _attn_sample(qa, kb, vb, cache_k4, cache_v4, pt_flat, layer, lamv, laminit, bias_tiles, subln_g,
                 n_dec, t_new, n_pages, page, group):
    rows = t_new * N_HEADS
    cols = page * N_HEADS
    n_steps = n_pages // group
    bfar, blast, bnew = bias_tiles
    q3 = qa.reshape(n_dec, rows, 2 * D_HEAD_A)
    kn3 = kb.reshape(n_dec, rows, 2 * D_HEAD_A)
    vn3 = vb.reshape(n_dec, rows, 2 * D_HEAD_A)
    per_b = pl.BlockSpec((None, rows, 2 * D_HEAD_A), lambda b, s, pt: (b, 0, 0))
    const = lambda shape: pl.BlockSpec(shape, lambda b, s, pt: (0, 0))

    def page_spec(g):
        return pl.BlockSpec((None, None, cols, 2 * D_HEAD_A),
                            lambda b, s, pt: (layer, pt[b * n_pages + s * group + g], 0, 0))

    in_specs = [const((4, D_HEAD_A)), const((1, 1)), per_b, per_b, per_b,
                const((2 * rows, cols)), const((2 * rows, cols)), const((2 * rows, rows)),
                const((1, 2 * D_HEAD_A))]
    in_specs += [page_spec(g) for g in range(group)] * 2
    out = pl.pallas_call(
        functools.partial(_attn_sample_kernel, group=group, n_steps=n_steps, rows=rows),
        grid_spec=pltpu.PrefetchScalarGridSpec(
            num_scalar_prefetch=1,
            grid=(n_dec, n_steps),
            in_specs=in_specs,
            out_specs=per_b,
            scratch_shapes=[pltpu.VMEM((2 * rows, 1), F32), pltpu.VMEM((2 * rows, 1), F32),
                            pltpu.VMEM((2 * rows, 2 * D_HEAD_A), F32)],
        ),
        out_shape=jax.ShapeDtypeStruct((n_dec, rows, 2 * D_HEAD_A), BF16),
        compiler_params=_cparams("parallel", "arbitrary"),
        name="attn_sample",
    )(pt_flat, lamv, laminit, q3, kn3, vn3, bfar, blast, bnew, subln_g.reshape(1, -1),
      *([cache_k4] * group), *([cache_v4] * group))
    return out.reshape(n_dec * t_new, W_A)


def _ret_kernel(pb_ref, cos_ref, sin_ref, s0_ref, o_ref, sfin_ref, st_sc, *, chunk, n_sub, n_valid, n_steps):
    c_id = pl.program_id(1)
    lg = chunk.bit_length() - 1
    rows = N_HEADS * chunk
    tb = n_sub * chunk
    masks = _head_masks()
    ones_bd = _head_ones()

    @pl.when(c_id == 0)
    def _():
        st_sc[...] = _load_block_diag(s0_ref)

    pb = pb_ref[...]
    q, k, v, g = pb[:, 0:W_BC], pb[:, W_BC:2 * W_BC], pb[:, 2 * W_BC:3 * W_BC], pb[:, 3 * W_BC:4 * W_BC]
    lane = lax.broadcasted_iota(jnp.int32, (1, W_BC), 1)
    first_half = (lane & (D_HEAD - 1)) < (D_HEAD // 2)
    cosf, sins = cos_ref[...], sin_ref[...]

    def rot(x):
        swapped = jnp.where(first_half, pltpu.roll(x, W_BC - D_HEAD // 2, 1), pltpu.roll(x, D_HEAD // 2, 1))
        return x * cosf + swapped * sins

    q = rot(q)
    k = rot(k) * (D_HEAD ** -0.5)
    row = lax.broadcasted_iota(jnp.int32, (tb, 1), 0)
    if n_valid < chunk:
        valid = row < n_valid
        k = jnp.where(valid, k, 0.0)
        v = jnp.where(valid, v, 0.0)

    log_g = [math.log1p(-(2.0 ** (-5 - h))) for h in range(N_HEADS)]
    lg_lane = sum(jnp.where(m, log_g[h], 0.0) for h, m in enumerate(masks))
    ri = lax.broadcasted_iota(jnp.int32, (rows, rows), 0)
    ci = lax.broadcasted_iota(jnp.int32, (rows, rows), 1)
    rh = lax.broadcasted_iota(jnp.int32, (rows, 1), 0) >> lg
    rpos = (lax.broadcasted_iota(jnp.int32, (rows, 1), 0) & (chunk - 1)).astype(F32)
    lg_row = sum(jnp.where(rh == h, log_g[h], 0.0) for h in range(N_HEADS))
    diff = (ri & (chunk - 1)) - (ci & (chunk - 1))
    keep = ((ri >> lg) == (ci >> lg)) & (diff >= 0)
    dmask = jnp.where(keep, jnp.exp(lg_row * jnp.maximum(diff, 0).astype(F32)), 0.0)
    xi = jnp.exp(lg_row * (rpos + 1.0))
    posf = (row & (chunk - 1)).astype(F32)
    zeta = jnp.exp(lg_lane * jnp.maximum(float(n_valid - 1) - posf, 0.0))
    g_chunk = jnp.exp(lg_lane * float(n_valid))
    qb, kb, vb, kzb = q.astype(BF16), k.astype(BF16), v.astype(BF16), (k * zeta).astype(BF16)

    def prepare(c):
        sl = slice(c * chunk, (c + 1) * chunk)
        q_st, k_st, v_st = (_stack_heads(x[sl], masks) for x in (qb, kb, vb))
        s_blk = _dot_nt(q_st, k_st) * dmask
        return q_st, _dot(s_blk.astype(BF16), v_st), _dot_tn(_stack_heads(kzb[sl], masks), v_st)

    prepared = [prepare(c) for c in range(n_sub)]
    state = st_sc[...]
    o_chunks = []
    for q_st, inner, incr in prepared:
        cross = _dot(q_st, state.astype(BF16)) * xi
        o_chunks.append(_unstack_heads(inner + cross, chunk))
        state = state * g_chunk + incr
    st_sc[...] = state

    o = o_chunks[0] if n_sub == 1 else jnp.concatenate(o_chunks, axis=0)
    ms = _head_sum(o * o, ones_bd) * (1.0 / D_HEAD)
    o_ref[...] = (jax.nn.silu(g) * (o * lax.rsqrt(ms + RMS_EPS))).astype(BF16)

    @pl.when(c_id == n_steps - 1)
    def _():
        _store_block_diag(sfin_ref, state)


def _retention(pb, cos_t, sin_t, s0, batch, t_pad, chunk, n_sub, n_valid):
    tb = chunk * n_sub
    n_steps = t_pad // tb
    rowspec = lambda w: pl.BlockSpec((tb, w), lambda b, c: (b * n_steps + c, 0))
    stspec = pl.BlockSpec((None, N_HEADS, D_HEAD, D_HEAD), lambda b, c: (b, 0, 0, 0))
    return pl.pallas_call(
        functools.partial(_ret_kernel, chunk=chunk, n_sub=n_sub, n_valid=n_valid, n_steps=n_steps),
        grid=(batch, n_steps),
        in_specs=[rowspec(PROJ_B), pl.BlockSpec((tb, W_BC), lambda b, c: (c, 0)),
                  pl.BlockSpec((tb, W_BC), lambda b, c: (c, 0)), stspec],
        out_specs=[rowspec(W_BC), stspec],
        out_shape=[jax.ShapeDtypeStruct((batch * t_pad, W_BC), BF16),
                   jax.ShapeDtypeStruct((batch, N_HEADS, D_HEAD, D_HEAD), F32)],
        scratch_shapes=[pltpu.VMEM((W_BC, W_BC), F32)],
        compiler_params=_cparams("parallel", "arbitrary"),
        name="retention",
    )(pb, cos_t, sin_t, s0)


def _rwkv_kernel(pc_ref, shift_ref, s0_ref, mu_ref, w0_ref, w2_ref, a0_ref, a2_ref, g2_ref, kk_ref, ka_ref,
                 rk_ref, lng_ref, lnb_ref, o_ref, sfin_ref, st_sc, prev_sc, *, chunk, n_sub, n_valid, n_steps):
    c_id = pl.program_id(1)
    lg = chunk.bit_length() - 1
    rows = N_HEADS * chunk
    tb = n_sub * chunk
    masks = _head_masks()
    ones_bd = _head_ones()

    @pl.when(c_id == 0)
    def _():
        st_sc[...] = _load_block_diag(s0_ref)
        prev_sc[...] = shift_ref[...]

    pc = pc_ref[...]
    row = lax.broadcasted_iota(jnp.int32, (tb, 1), 0)
    prev = jnp.where(row == 0, prev_sc[...], pltpu.roll(pc, 1, 0))
    prev_sc[...] = pc[tb - 1:tb, :]
    xm = pc + (prev - pc) * mu_ref[...]
    r, k, v, xt = xm[:, 0:W_BC], xm[:, W_BC:2 * W_BC], xm[:, 2 * W_BC:3 * W_BC], xm[:, 3 * W_BC:4 * W_BC]

    w_log = -_softplus(-(w0_ref[...] + _dot(jnp.tanh(xt).astype(BF16), w2_ref[...]))) - 0.5
    logw = -jnp.exp(w_log)
    a = jax.nn.sigmoid(a0_ref[...] + _dot(xt.astype(BF16), a2_ref[...]))
    g = _dot(jax.nn.sigmoid(xt).astype(BF16), g2_ref[...])

    kk = k * kk_ref[...]
    kk = kk / jnp.maximum(jnp.sqrt(_head_sum(kk * kk, ones_bd)), 1e-12)
    k2 = k * (1.0 + (a - 1.0) * ka_ref[...])
    if n_valid < chunk:
        valid = row < n_valid
        logw = jnp.where(valid, logw, 0.0)
        kk = jnp.where(valid, kk, 0.0)
        k2 = jnp.where(valid, k2, 0.0)
        v = jnp.where(valid, v, 0.0)

    ti = lax.broadcasted_iota(jnp.int32, (tb, tb), 0)
    tj = lax.broadcasted_iota(jnp.int32, (tb, tb), 1)
    same_chunk = (ti >> lg) == (tj >> lg)
    tri = jnp.where(same_chunk & (ti >= tj), 1.0, 0.0).astype(BF16)
    tot = jnp.where(same_chunk, 1.0, 0.0).astype(BF16)
    parts = _split3(logw)
    cum = sum(_dot(tri, part) for part in parts)
    cum_end = sum(_dot(tot, part) for part in parts)
    e_bwd = jnp.exp(-cum)
    e_end = jnp.exp(cum_end - cum)
    kka = kk * a
    a_t = -kk * jnp.exp(cum - logw)
    b_t = kka * e_bwd
    k_t = k2 * e_bwd
    r_t = r * jnp.exp(cum)
    b_h = kka * e_end
    k_h = k2 * e_end
    p_end = jnp.exp(cum_end)

    ri = lax.broadcasted_iota(jnp.int32, (rows, rows), 0)
    ci = lax.broadcasted_iota(jnp.int32, (rows, rows), 1)
    same = (ri >> lg) == (ci >> lg)
    dpos = (ri & (chunk - 1)) - (ci & (chunk - 1))
    strict = same & (dpos > 0)
    incl = same & (dpos >= 0)
    eye = jnp.where(ri == ci, 1.0, 0.0)

    def prepare(c):
        sl = slice(c * chunk, (c + 1) * chunk)
        st = lambda x: _stack_heads(x[sl].astype(BF16), masks)
        a_st, b_st, k_st, r_st, v_st, bh_st, kh_st = st(a_t), st(b_t), st(k_t), st(r_t), st(v), st(b_h), st(k_h)
        n_bd = jnp.where(strict, _dot_nt(a_st, b_st), 0.0)
        ak_bd = jnp.where(strict, _dot_nt(a_st, k_st), 0.0)
        m_rb = jnp.where(incl, _dot_nt(r_st, b_st), 0.0).astype(BF16)
        m_rk = jnp.where(incl, _dot_nt(r_st, k_st), 0.0).astype(BF16)
        t_inv = eye + n_bd
        pw = n_bd
        for _ in range(lg - 1):
            pwb = pw.astype(BF16)
            pw = _dot(pwb, pwb)
            t_inv = t_inv + _dot(t_inv.astype(BF16), pw.astype(BF16))
        tbf = t_inv.astype(BF16)
        w_st = _dot(tbf, a_st).astype(BF16)
        y_st = _dot(tbf, _dot(ak_bd.astype(BF16), v_st).astype(BF16))
        o_pre = _dot(m_rk, v_st)
        g_pre = _dot_tn(v_st, kh_st)
        return w_st, y_st, r_st, m_rb, o_pre, bh_st, g_pre, p_end[c * chunk:c * chunk + 1]

    prepared = [prepare(c) for c in range(n_sub)]

    state = st_sc[...]
    o_chunks = []
    for w_st, y_st, r_st, m_rb, o_pre, bh_st, g_pre, p_c in prepared:
        sb = state.astype(BF16)
        u = _dot_nt(w_st, sb) + y_st
        ub = u.astype(BF16)
        o_chunks.append(_unstack_heads(_dot_nt(r_st, sb) + _dot(m_rb, ub) + o_pre, chunk))
        state = state * p_c + _dot_tn(ub, bh_st) + g_pre
    st_sc[...] = state

    o = o_chunks[0] if n_sub == 1 else jnp.concatenate(o_chunks, axis=0)
    inv = 1.0 / D_HEAD
    mean = _head_sum(o, ones_bd) * inv
    d = o - mean
    var = _head_sum(d * d, ones_bd) * inv
    o_n = d * lax.rsqrt(var + RWKV_GN_EPS) * lng_ref[...] + lnb_ref[...]
    bonus = _head_sum(r * k2 * rk_ref[...], ones_bd) * v
    o_ref[...] = ((o_n + bonus) * g).astype(BF16)

    @pl.when(c_id == n_steps - 1)
    def _():
        _store_block_diag(sfin_ref, state)


def _rwkv(pc, shift0, s0, params, batch, t_pad, chunk, n_sub, n_valid):
    tb = chunk * n_sub
    n_steps = t_pad // tb
    rowspec = lambda w: pl.BlockSpec((tb, w), lambda b, c: (b * n_steps + c, 0))
    stspec = pl.BlockSpec((None, N_HEADS, D_HEAD, D_HEAD), lambda b, c: (b, 0, 0, 0))
    const = lambda arr: pl.BlockSpec(arr.shape, lambda b, c: (0,) * arr.ndim)
    return pl.pallas_call(
        functools.partial(_rwkv_kernel, chunk=chunk, n_sub=n_sub, n_valid=n_valid, n_steps=n_steps),
        grid=(batch, n_steps),
        in_specs=[rowspec(PROJ_C), pl.BlockSpec((None, 1, PROJ_C), lambda b, c: (b, 0, 0)), stspec]
                 + [const(p) for p in params],
        out_specs=[rowspec(W_BC), stspec],
        out_shape=[jax.ShapeDtypeStruct((batch * t_pad, W_BC), BF16),
                   jax.ShapeDtypeStruct((batch, N_HEADS, D_HEAD, D_HEAD), F32)],
        scratch_shapes=[pltpu.VMEM((W_BC, W_BC), F32), pltpu.VMEM((1, PROJ_C), F32)],
        compiler_params=_cparams("parallel", "arbitrary"),
        name="rwkv7",
    )(pc, shift0, s0, *params)


def _out_proj_kernel(x_ref, a_ref, b_ref, c_ref, w_ref, g_ref, bb_ref, y_ref, *, alpha):
    acc = _dot(a_ref[...], w_ref[0:W_A, :])
    acc += _dot(b_ref[...], w_ref[W_A:W_A + W_BC, :])
    acc += _dot(c_ref[...], w_ref[W_A + W_BC:W_A + 2 * W_BC, :])
    y_ref[...] = _layer_norm(alpha * x_ref[...] + acc, g_ref[...], bb_ref[...])


def _out_proj(x2d, oa, ob, oc, w_bf, g, b, alpha, tm):
    m = x2d.shape[0]
    row = lambda w: pl.BlockSpec((tm, w), lambda i: (i, 0))
    const = lambda shape: pl.BlockSpec(shape, lambda i: (0, 0))
    return pl.pallas_call(
        functools.partial(_out_proj_kernel, alpha=alpha),
        grid=(m // tm,),
        in_specs=[row(D_MODEL), row(W_A), row(W_BC), row(W_BC), const((D_MODEL, D_MODEL)),
                  const((1, D_MODEL)), const((1, D_MODEL))],
        out_specs=row(D_MODEL),
        out_shape=jax.ShapeDtypeStruct((m, D_MODEL), F32),
        compiler_params=_cparams("parallel"),
        name="out_proj",
    )(x2d, oa, ob, oc, w_bf, g.reshape(1, -1), b.reshape(1, -1))


def _ffn_kernel(x_ref, w1_ref, w3_ref, w2_ref, g_ref, b_ref, y_ref, acc_sc, *, alpha):
    f = pl.program_id(1)
    x = x_ref[...]
    xb = x.astype(BF16)

    @pl.when(f == 0)
    def _():
        acc_sc[...] = jnp.zeros(acc_sc.shape, F32)

    h = jax.nn.silu(_dot(xb, w1_ref[...])) * _dot(xb, w3_ref[...])
    acc_sc[...] += _dot(h.astype(BF16), w2_ref[...])

    @pl.when(f == pl.num_programs(1) - 1)
    def _():
        y_ref[...] = _layer_norm(alpha * x + acc_sc[...], g_ref[...], b_ref[...])


def _ffn(x2d, w1, w3, w2, g, b, alpha, tm, tf):
    m = x2d.shape[0]
    d_ff = w1.shape[1]
    return pl.pallas_call(
        functools.partial(_ffn_kernel, alpha=alpha),
        grid=(m // tm, d_ff // tf),
        in_specs=[pl.BlockSpec((tm, D_MODEL), lambda i, f: (i, 0)),
                  pl.BlockSpec((D_MODEL, tf), lambda i, f: (0, f)),
                  pl.BlockSpec((D_MODEL, tf), lambda i, f: (0, f)),
                  pl.BlockSpec((tf, D_MODEL), lambda i, f: (f, 0)),
                  pl.BlockSpec((1, D_MODEL), lambda i, f: (0, 0)),
                  pl.BlockSpec((1, D_MODEL), lambda i, f: (0, 0))],
        out_specs=pl.BlockSpec((tm, D_MODEL), lambda i, f: (i, 0)),
        out_shape=jax.ShapeDtypeStruct((m, D_MODEL), F32),
        scratch_shapes=[pltpu.VMEM((tm, D_MODEL), F32)],
        compiler_params=_cparams("parallel", "arbitrary"),
        name="ffn",
    )(x2d, w1, w3, w2, g.reshape(1, -1), b.reshape(1, -1))


def _moe_kernel(x_ref, rh_ref, rl_ref, w1_ref, w3_ref, w2_ref, g_ref, b_ref, y_ref, acc_sc, gate_sc, xb_sc,
                *, alpha, n_experts):
    e = pl.program_id(1)
    lane = lax.broadcasted_iota(jnp.int32, (1, LANES), 1)

    @pl.when(e == 0)
    def _():
        x = x_ref[...]
        xh, xl = _split2(x)
        xb_sc[...] = xh
        logits = _dot(xh, rh_ref[...]) + _dot(xl, rh_ref[...]) + _dot(xh, rl_ref[...])
        logits = jnp.where(lane < n_experts, logits, NEG)
        v1 = jnp.max(logits, axis=-1, keepdims=True)
        i1 = jnp.min(jnp.where(logits == v1, lane, LANES), axis=-1, keepdims=True)
        rest = jnp.where(lane == i1, NEG, logits)
        v2 = jnp.max(rest, axis=-1, keepdims=True)
        i2 = jnp.min(jnp.where(rest == v2, lane, LANES), axis=-1, keepdims=True)
        ex = jnp.exp(v2 - v1)
        g1 = 1.0 / (1.0 + ex)
        g2 = ex / (1.0 + ex)
        gate_sc[...] = jnp.where(lane == i1, g1, 0.0) + jnp.where(lane == i2, g2, 0.0)
        acc_sc[...] = jnp.zeros(acc_sc.shape, F32)

    xb = xb_sc[...]
    h = jax.nn.silu(_dot(xb, w1_ref[...])) * _dot(xb, w3_ref[...])
    f = _dot(h.astype(BF16), w2_ref[...])
    ge = jnp.sum(jnp.where(lane == e, gate_sc[...], 0.0), axis=-1, keepdims=True)
    acc_sc[...] += ge * f

    @pl.when(e == n_experts - 1)
    def _():
        y_ref[...] = _layer_norm(alpha * x_ref[...] + acc_sc[...], g_ref[...], b_ref[...])


def _moe(x2d, router, w1, w3, w2, g, b, alpha, tm):
    m = x2d.shape[0]
    n_experts, _, d_e = w1.shape
    r_pad = jnp.zeros((D_MODEL, LANES), F32).at[:, :n_experts].set(router)
    r_hi = r_pad.astype(BF16)
    r_lo = (r_pad - r_hi.astype(F32)).astype(BF16)
    return pl.pallas_call(
        functools.partial(_moe_kernel, alpha=alpha, n_experts=n_experts),
        grid=(m // tm, n_experts),
        in_specs=[pl.BlockSpec((tm, D_MODEL), lambda i, e: (i, 0)),
                  pl.BlockSpec((D_MODEL, LANES), lambda i, e: (0, 0)),
                  pl.BlockSpec((D_MODEL, LANES), lambda i, e: (0, 0)),
                  pl.BlockSpec((None, D_MODEL, d_e), lambda i, e: (e, 0, 0)),
                  pl.BlockSpec((None, D_MODEL, d_e), lambda i, e: (e, 0, 0)),
                  pl.BlockSpec((None, d_e, D_MODEL), lambda i, e: (e, 0, 0)),
                  pl.BlockSpec((1, D_MODEL), lambda i, e: (0, 0)),
                  pl.BlockSpec((1, D_MODEL), lambda i, e: (0, 0))],
        out_specs=pl.BlockSpec((tm, D_MODEL), lambda i, e: (i, 0)),
        out_shape=jax.ShapeDtypeStruct((m, D_MODEL), F32),
        scratch_shapes=[pltpu.VMEM((tm, D_MODEL), F32), pltpu.VMEM((tm, LANES), F32),
                        pltpu.VMEM((tm, D_MODEL), BF16)],
        compiler_params=_cparams("parallel", "arbitrary"),
        name="moe",
    )(x2d, r_hi, r_lo, w1, w3, w2, g.reshape(1, -1), b.reshape(1, -1))


MOE_TILE = 512


def _moe_router_kernel(x_ref, rh_ref, rl_ref, info_ref, cnt_ref, carry_sc, *, n_experts):
    i = pl.program_id(0)
    tm = x_ref.shape[0]
    lane = lax.broadcasted_iota(jnp.int32, (1, LANES), 1)

    @pl.when(i == 0)
    def _():
        carry_sc[...] = jnp.zeros(carry_sc.shape, F32)

    xh, xl = _split2(x_ref[...])
    logits = _dot(xh, rh_ref[...]) + _dot(xl, rh_ref[...]) + _dot(xh, rl_ref[...])
    logits = jnp.where(lane < n_experts, logits, NEG)
    v1 = jnp.max(logits, axis=-1, keepdims=True)
    i1 = jnp.min(jnp.where(logits == v1, lane, LANES), axis=-1, keepdims=True)
    rest = jnp.where(lane == i1, NEG, logits)
    v2 = jnp.max(rest, axis=-1, keepdims=True)
    i2 = jnp.min(jnp.where(rest == v2, lane, LANES), axis=-1, keepdims=True)
    ex = jnp.exp(v2 - v1)
    g1 = 1.0 / (1.0 + ex)
    g2 = ex / (1.0 + ex)
    m1, m2 = lane == i1, lane == i2
    member = jnp.where(m1 | m2, 1.0, 0.0)
    r = lax.broadcasted_iota(jnp.int32, (tm, tm), 0)
    c = lax.broadcasted_iota(jnp.int32, (tm, tm), 1)
    before = jnp.where(r > c, 1.0, 0.0).astype(BF16)
    rank = _dot(before, member.astype(BF16)) + carry_sc[...]
    rank1 = jnp.sum(jnp.where(m1, rank, 0.0), axis=-1, keepdims=True)
    rank2 = jnp.sum(jnp.where(m2, rank, 0.0), axis=-1, keepdims=True)
    total = carry_sc[...] + jnp.sum(member, axis=0, keepdims=True)
    carry_sc[...] = total
    cnt_ref[...] = total
    fields = (i1.astype(F32), i2.astype(F32), g1, g2, rank1, rank2)
    info = jnp.zeros((tm, LANES), F32)
    for k, f in enumerate(fields):
        info = jnp.where(lane == k, f, info)
    info_ref[...] = info


def _moe_dispatch_kernel(pos1_ref, pos2_ref, x_ref, xs_in_ref, xs_ref, sem, *, tm):
    del xs_in_ref
    base = pl.program_id(0) * tm

    def row_copy(t, p):
        return pltpu.make_async_copy(x_ref.at[pl.ds(t, 1)], xs_ref.at[pl.ds(p, 1)], sem)

    def issue(t, carry):
        row_copy(t, pos1_ref[base + t]).start()
        row_copy(t, pos2_ref[base + t]).start()
        return carry

    def drain(t, carry):
        row_copy(0, 0).wait()
        row_copy(0, 0).wait()
        return carry

    lax.fori_loop(0, tm, issue, 0, unroll=8)
    lax.fori_loop(0, tm, drain, 0, unroll=8)


def _moe_expert_kernel(te_ref, nv_ref, x_ref, w1_ref, w3_ref, w2_ref, y_ref):
    j = pl.program_id(0)

    @pl.when(j < nv_ref[0])
    def _():
        xb = x_ref[...].astype(BF16)
        h = jax.nn.silu(_dot(xb, w1_ref[...])) * _dot(xb, w3_ref[...])
        y_ref[...] = _dot(h.astype(BF16), w2_ref[...])

    @pl.when(j >= nv_ref[0])
    def _():
        y_ref[...] = jnp.zeros(y_ref.shape, F32)


def _moe_combine_kernel(pos1_ref, pos2_ref, x_ref, info_ref, ys_ref, g_ref, b_ref, y_ref, buf1, buf2, sem,
                        *, tm, alpha):
    base = pl.program_id(0) * tm

    def row_copy(p, buf, t):
        return pltpu.make_async_copy(ys_ref.at[pl.ds(p, 1)], buf.at[pl.ds(t, 1)], sem)

    def issue(t, carry):
        row_copy(pos1_ref[base + t], buf1, t).start()
        row_copy(pos2_ref[base + t], buf2, t).start()
        return carry

    def drain(t, carry):
        row_copy(0, buf1, 0).wait()
        row_copy(0, buf2, 0).wait()
        return carry

    lax.fori_loop(0, tm, issue, 0, unroll=8)
    lax.fori_loop(0, tm, drain, 0, unroll=8)
    lane = lax.broadcasted_iota(jnp.int32, (1, LANES), 1)
    info = info_ref[...]
    g1 = jnp.sum(jnp.where(lane == 2, info, 0.0), axis=-1, keepdims=True)
    g2 = jnp.sum(jnp.where(lane == 3, info, 0.0), axis=-1, keepdims=True)
    f = g1 * buf1[...] + g2 * buf2[...]
    y_ref[...] = _layer_norm(alpha * x_ref[...] + f, g_ref[...], b_ref[...])


def _moe_routed(x2d, router, w1, w3, w2, g, b, alpha):
    m = x2d.shape[0]
    n_experts, _, d_e = w1.shape
    tile = MOE_TILE
    n_tiles = (2 * m) // tile + n_experts
    rows = n_tiles * tile
    r_pad = jnp.zeros((D_MODEL, LANES), F32).at[:, :n_experts].set(router)
    r_hi = r_pad.astype(BF16)
    r_lo = (r_pad - r_hi.astype(F32)).astype(BF16)
    row_spec = lambda w: pl.BlockSpec((tile, w), lambda i, *_: (i, 0))

    info, cnt = pl.pallas_call(
        functools.partial(_moe_router_kernel, n_experts=n_experts),
        grid=(m // tile,),
        in_specs=[row_spec(D_MODEL), pl.BlockSpec((D_MODEL, LANES), lambda i: (0, 0)),
                  pl.BlockSpec((D_MODEL, LANES), lambda i: (0, 0))],
        out_specs=[row_spec(LANES), pl.BlockSpec((1, LANES), lambda i: (0, 0))],
        out_shape=[jax.ShapeDtypeStruct((m, LANES), F32), jax.ShapeDtypeStruct((1, LANES), F32)],
        scratch_shapes=[pltpu.VMEM((1, LANES), F32)],
        compiler_params=_cparams("arbitrary"),
        name="moe_router",
    )(x2d, r_hi, r_lo)

    counts = cnt[0, :n_experts].astype(jnp.int32)
    tiles_e = (counts + tile - 1) // tile
    tile_end = jnp.cumsum(tiles_e)
    row_off = (tile_end - tiles_e) * tile
    eids = jnp.arange(n_experts, dtype=jnp.int32)

    def dest(col_e, col_r):
        e = info[:, col_e].astype(jnp.int32)
        off = jnp.sum(jnp.where(e[:, None] == eids[None, :], row_off[None, :], 0), axis=1)
        return off + info[:, col_r].astype(jnp.int32)

    pos1, pos2 = dest(0, 4), dest(1, 5)
    tile_ids = jnp.arange(n_tiles, dtype=jnp.int32)
    tile_expert = jnp.minimum(jnp.sum(tile_ids[:, None] >= tile_end[None, :], axis=1), n_experts - 1).astype(jnp.int32)
    n_valid = tile_end[n_experts - 1:].astype(jnp.int32)

    xs = pl.pallas_call(
        functools.partial(_moe_dispatch_kernel, tm=tile),
        grid_spec=pltpu.PrefetchScalarGridSpec(
            num_scalar_prefetch=2,
            grid=(m // tile,),
            in_specs=[row_spec(D_MODEL), pl.BlockSpec(memory_space=pl.ANY)],
            out_specs=pl.BlockSpec(memory_space=pl.ANY),
            scratch_shapes=[pltpu.SemaphoreType.DMA],
        ),
        out_shape=jax.ShapeDtypeStruct((rows, D_MODEL), F32),
        input_output_aliases={3: 0},
        compiler_params=_cparams("arbitrary"),
        name="moe_dispatch",
    )(pos1, pos2, x2d, jnp.zeros((rows, D_MODEL), F32))

    ys = pl.pallas_call(
        _moe_expert_kernel,
        grid_spec=pltpu.PrefetchScalarGridSpec(
            num_scalar_prefetch=2,
            grid=(n_tiles,),
            in_specs=[row_spec(D_MODEL),
                      pl.BlockSpec((None, D_MODEL, d_e), lambda j, te, nv: (te[j], 0, 0)),
                      pl.BlockSpec((None, D_MODEL, d_e), lambda j, te, nv: (te[j], 0, 0)),
                      pl.BlockSpec((None, d_e, D_MODEL), lambda j, te, nv: (te[j], 0, 0))],
            out_specs=row_spec(D_MODEL),
        ),
        out_shape=jax.ShapeDtypeStruct((rows, D_MODEL), F32),
        compiler_params=_cparams("arbitrary"),
        name="moe_experts",
    )(tile_expert, n_valid, xs, w1, w3, w2)

    return pl.pallas_call(
        functools.partial(_moe_combine_kernel, tm=tile, alpha=alpha),
        grid_spec=pltpu.PrefetchScalarGridSpec(
            num_scalar_prefetch=2,
            grid=(m // tile,),
            in_specs=[row_spec(D_MODEL), row_spec(LANES), pl.BlockSpec(memory_space=pl.ANY),
                      pl.BlockSpec((1, D_MODEL), lambda i, *_: (0, 0)),
                      pl.BlockSpec((1, D_MODEL), lambda i, *_: (0, 0))],
            out_specs=row_spec(D_MODEL),
            scratch_shapes=[pltpu.VMEM((tile, D_MODEL), F32), pltpu.VMEM((tile, D_MODEL), F32),
                            pltpu.SemaphoreType.DMA],
        ),
        out_shape=jax.ShapeDtypeStruct((m, D_MODEL), F32),
        compiler_params=_cparams("arbitrary"),
        name="moe_combine",
    )(pos1, pos2, x2d, info, ys, g.reshape(1, -1), b.reshape(1, -1))


def _rotary_tables(pos0, t_pad):
    half = D_HEAD // 2
    inv = 1.0 / (10000.0 ** (jnp.arange(half, dtype=F32) / half))
    ang = (pos0 + jnp.arange(t_pad, dtype=jnp.int32)).astype(F32)[:, None] * inv[None, :]
    cos, sin = jnp.cos(ang), jnp.sin(ang)
    cos_t = jnp.tile(jnp.concatenate([cos, cos], axis=-1), (1, N_HEADS))
    sin_t = jnp.tile(jnp.concatenate([-sin, sin], axis=-1), (1, N_HEADS))
    return cos_t, sin_t


def _pad_rows(lo, arr):
    return jnp.zeros((W_BC, W_BC), F32).at[lo:lo + arr.shape[0]].set(arr).astype(BF16)


def _ffn_tile(d_ff):
    best = LANES if d_ff % LANES == 0 else d_ff
    for t in range(LANES, min(d_ff, 1536) + 1, LANES):
        if d_ff % t == 0:
            best = t
    return best


def kernel(x_prompt, x_sample, cache_k, cache_v, page_table, state_ret, state_wkv, state_shift, rel_bias, w_in, w_out, lambda_q1, lambda_k1, lambda_q2, lambda_k2, subln_g, tshift_mu, decay_w0, decay_w2, iclr_a0, iclr_a2, gate_w2, k_k, k_a, r_k, lnx_g, lnx_b, ln1_g, ln1_b, ln2_g, ln2_b, ffn_w1, ffn_w3, ffn_w2, router_w, expert_w1, expert_w3, expert_w2):
    bp, seq, _ = x_prompt.shape
    n_dec, t_new, _ = x_sample.shape
    depth, n_pool, page, _, _ = cache_k.shape
    n_pages = page_table.shape[1]
    past = n_pages * page
    alpha = (2 * depth) ** 0.25

    mp, ms = bp * seq, n_dec * t_new
    tm_p = 256 if mp % 256 == 0 else mp
    tm_s = ms
    blk = 256 if seq % 256 == 0 else seq
    chunk_p = 64 if seq % 64 == 0 else seq
    sub_p = 4 if seq % (4 * chunk_p) == 0 else 1
    chunk_s = 32
    group = 16 if n_pages % 16 == 0 else (8 if n_pages % 8 == 0 else 1)

    cache_k4 = cache_k.reshape(depth, n_pool, page * N_HEADS, 2 * D_HEAD_A)
    cache_v4 = cache_v.reshape(depth, n_pool, page * N_HEADS, 2 * D_HEAD_A)
    pt_flat = page_table.reshape(-1).astype(jnp.int32)

    rel_bias = rel_bias.astype(F32)
    near_bias = _near_bias_tiles(rel_bias, blk)
    sample_tiles = _sample_bias_tiles(rel_bias, t_new, n_pages, page)
    cos_p, sin_p = _rotary_tables(0, seq)
    cos_s, sin_s = _rotary_tables(past, chunk_s)
    zero_st_p = jnp.zeros((bp, N_HEADS, D_HEAD, D_HEAD), F32)
    zero_shift_p = jnp.zeros((bp, 1, PROJ_C), F32)

    def pad_sample(a):
        w = a.shape[-1]
        a3 = a.reshape(n_dec, t_new, w)
        return jnp.pad(a3, ((0, 0), (0, chunk_s - t_new), (0, 0))).reshape(n_dec * chunk_s, w)

    def unpad_sample(a):
        w = a.shape[-1]
        return a.reshape(n_dec, chunk_s, w)[:, :t_new].reshape(n_dec * t_new, w)

    xp = x_prompt.reshape(mp, D_MODEL)
    xs = x_sample.reshape(ms, D_MODEL)
    kp_all = jnp.zeros((depth, mp * N_HEADS, 2 * D_HEAD_A), F32)
    vp_all = jnp.zeros((depth, mp * N_HEADS, 2 * D_HEAD_A), F32)
    ks_all = jnp.zeros((depth, ms * N_HEADS, 2 * D_HEAD_A), F32)
    vs_all = jnp.zeros((depth, ms * N_HEADS, 2 * D_HEAD_A), F32)
    outs = {k: [] for k in ("rp", "rs", "wp", "ws", "sp", "ss")}
    for l in range(depth):
        w_in_bf = w_in[l].astype(BF16)
        w_out_bf = w_out[l].astype(BF16)
        lam_init = 0.8 - 0.6 * math.exp(-0.3 * l)
        lamv = jnp.stack([lambda_q1[l], lambda_k1[l], lambda_q2[l], lambda_k2[l]]).astype(F32)
        laminit = jnp.full((1, 1), lam_init, F32)
        row = lambda a: a.reshape(1, -1).astype(F32)
        rwkv_params = [row(tshift_mu[l]), row(decay_w0[l]), _pad_rows(0, decay_w2[l]), row(iclr_a0[l]),
                       _pad_rows(64, iclr_a2[l]), _pad_rows(128, gate_w2[l]),
                       row(k_k[l]), row(k_a[l]), row(r_k[l]), row(lnx_g[l]), row(lnx_b[l])]

        qa, kp_all, vp_all, kb, vb, pb, pc = _proj_in(xp, w_in_bf, kp_all, vp_all, l, depth, tm_p)
        oa = _attn_prompt(qa, kb, vb, lamv, laminit, near_bias, subln_g[l], bp, seq, blk)
        ob, ret_p = _retention(pb, cos_p, sin_p, zero_st_p, bp, seq, chunk_p, sub_p, chunk_p)
        oc, wkv_p = _rwkv(pc, zero_shift_p, zero_st_p, rwkv_params, bp, seq, chunk_p, sub_p, chunk_p)
        xp = _out_proj(xp, oa, ob, oc, w_out_bf, ln1_g[l], ln1_b[l], alpha, 512 if mp % 512 == 0 else tm_p)
        outs["rp"].append(ret_p)
        outs["wp"].append(wkv_p)
        outs["sp"].append(pc.reshape(bp, seq, PROJ_C)[:, -1])

        qa, ks_all, vs_all, kb, vb, pb, pc = _proj_in(xs, w_in_bf, ks_all, vs_all, l, depth, tm_s)
        oa = _attn_sample(qa, kb, vb, cache_k4, cache_v4, pt_flat, l, lamv, laminit, sample_tiles, subln_g[l],
                          n_dec, t_new, n_pages, page, group)
        ob, ret_s = _retention(pad_sample(pb), cos_s, sin_s, state_ret[l], n_dec, chunk_s, chunk_s, 1, t_new)
        oc, wkv_s = _rwkv(pad_sample(pc), state_shift[l].astype(F32).reshape(n_dec, 1, PROJ_C),
                          state_wkv[l], rwkv_params, n_dec, chunk_s, chunk_s, 1, t_new)
        xs = _out_proj(xs, oa, unpad_sample(ob), unpad_sample(oc), w_out_bf, ln1_g[l], ln1_b[l], alpha, tm_s)
        outs["rs"].append(ret_s)
        outs["ws"].append(wkv_s)
        outs["ss"].append(pc.reshape(n_dec, t_new, PROJ_C)[:, -1])

        j = l // 2
        if l % 2 == 0:
            w1, w3, w2 = ffn_w1[j].astype(BF16), ffn_w3[j].astype(BF16), ffn_w2[j].astype(BF16)
            tf = _ffn_tile(w1.shape[1])
            xp = _ffn(xp, w1, w3, w2, ln2_g[l], ln2_b[l], alpha, min(512, mp), tf)
            xs = _ffn(xs, w1, w3, w2, ln2_g[l], ln2_b[l], alpha, tm_s, tf)
        else:
            w1, w3, w2 = expert_w1[j].astype(BF16), expert_w3[j].astype(BF16), expert_w2[j].astype(BF16)
            if mp % MOE_TILE == 0:
                xp = _moe_routed(xp, router_w[j], w1, w3, w2, ln2_g[l], ln2_b[l], alpha)
            else:
                xp = _moe(xp, router_w[j], w1, w3, w2, ln2_g[l], ln2_b[l], alpha, mp)
            xs = _moe(xs, router_w[j], w1, w3, w2, ln2_g[l], ln2_b[l], alpha, tm_s)

    st = lambda key: jnp.stack(outs[key], axis=0)
    kv_p = lambda a: a.reshape(depth, bp, seq, N_HEADS, 2 * D_HEAD_A)
    kv_s = lambda a: a.reshape(depth, n_dec, t_new, N_HEADS, 2 * D_HEAD_A)
    return (xp.reshape(bp, seq, D_MODEL), xs.reshape(n_dec, t_new, D_MODEL),
            kv_p(kp_all), kv_p(vp_all), kv_s(ks_all), kv_s(vs_all),
            st("rp"), st("rs"), st("wp"), st("ws"), st("sp"), st("ss"))
```

```python
import functools
import math

import jax
import jax.numpy as jnp
from jax import lax
from jax.experimental import pallas as pl
from jax.experimental.pallas import tpu as pltpu

F32 = jnp.float32
BF16 = jnp.bfloat16

D_MODEL = 1024
N_HEADS = 4
D_HEAD_A = 64
W_A = N_HEADS * 2 * D_HEAD_A
D_HEAD = 64
W_BC = N_HEADS * D_HEAD
PROJ_A = 3 * W_A
PROJ_B = 4 * W_BC
PROJ_C = 1024
N_BUCKETS = 32
MAX_DISTANCE = 128
NEAR_DIST = 113
RWKV_GN_EPS = 64e-5
LN_EPS = 1e-5
RMS_EPS = 1e-5
NEG = -1e30
LOG2E = 1.4426950408889634

VMEM_LIMIT_BYTES = 56 * 1024 * 1024
LANES = 128


def _cparams(*sem):
    return pltpu.CompilerParams(dimension_semantics=sem, vmem_limit_bytes=VMEM_LIMIT_BYTES)


def _dot(a, b):
    return jnp.dot(a, b, preferred_element_type=F32)


def _dot_nt(a, b):
    return lax.dot_general(a, b, (((1,), (1,)), ((), ())), preferred_element_type=F32)


def _dot_tn(a, b):
    return lax.dot_general(a, b, (((0,), (0,)), ((), ())), preferred_element_type=F32)


def _split2(x):
    hi = x.astype(BF16)
    lo = (x - hi.astype(F32)).astype(BF16)
    return hi, lo


def _split3(x):
    hi = x.astype(BF16)
    r1 = x - hi.astype(F32)
    mid = r1.astype(BF16)
    lo = (r1 - mid.astype(F32)).astype(BF16)
    return hi, mid, lo


def _softplus(z):
    return jnp.maximum(z, 0.0) + jnp.log1p(jnp.exp(-jnp.abs(z)))


def _layer_norm(z, g, b):
    mu = jnp.mean(z, axis=-1, keepdims=True)
    d = z - mu
    var = jnp.mean(d * d, axis=-1, keepdims=True)
    return d * lax.rsqrt(var + LN_EPS) * g + b


def _head_masks():
    lane = lax.broadcasted_iota(jnp.int32, (1, W_BC), 1)
    return [(lane >= h * D_HEAD) & (lane < (h + 1) * D_HEAD) for h in range(N_HEADS)]


def _stack_heads(x, masks):
    zero = jnp.zeros((), x.dtype)
    return jnp.concatenate([jnp.where(m, x, zero) for m in masks], axis=0)


def _unstack_heads(z, c):
    return z[0:c] + z[c:2 * c] + z[2 * c:3 * c] + z[3 * c:4 * c]


def _load_block_diag(s_ref):
    rows = []
    for h in range(N_HEADS):
        pieces = [jnp.zeros((D_HEAD, D_HEAD), F32)] * N_HEADS
        pieces[h] = s_ref[h].astype(F32)
        rows.append(jnp.concatenate(pieces, axis=1))
    return jnp.concatenate(rows, axis=0)


def _store_block_diag(s_ref, state):
    for h in range(N_HEADS):
        s_ref[h] = state[h * D_HEAD:(h + 1) * D_HEAD, h * D_HEAD:(h + 1) * D_HEAD]


def _head_ones():
    r = lax.broadcasted_iota(jnp.int32, (W_BC, W_BC), 0)
    c = lax.broadcasted_iota(jnp.int32, (W_BC, W_BC), 1)
    return jnp.where((r // D_HEAD) == (c // D_HEAD), 1.0, 0.0).astype(BF16)


def _head_sum(x, ones_bd):
    return _dot(x.astype(BF16), ones_bd)


def _proj_in_kernel(x_ref, w_ref, k_in_ref, v_in_ref, qa_ref, k_ref, v_ref, kb_ref, vb_ref, pb_ref, pc_ref):
    del k_in_ref, v_in_ref
    xb = x_ref[...].astype(BF16)

    def mm(lo, hi):
        return _dot(xb, w_ref[:, lo:hi])

    qa_ref[...] = (mm(0, W_A) * (D_HEAD_A ** -0.5 * LOG2E)).astype(BF16)
    tm = xb.shape[0]
    dv = 2 * D_HEAD_A
    for ref, bref, lo in ((k_ref, kb_ref, W_A), (v_ref, vb_ref, 2 * W_A)):
        val = mm(lo, lo + W_A)
        bref[...] = val.astype(BF16)
        for h in range(N_HEADS):
            ref[pl.ds(h, tm, stride=N_HEADS), :] = val[:, h * dv:(h + 1) * dv]
    pb_ref[...] = mm(PROJ_A, PROJ_A + PROJ_B)
    pc_ref[...] = mm(PROJ_A + PROJ_B, PROJ_A + PROJ_B + PROJ_C)


def _proj_in(x2d, w_bf, k_all, v_all, layer, depth, tm):
    m = x2d.shape[0]
    n = w_bf.shape[1]
    row = lambda i: (i, 0)
    stack_spec = pl.BlockSpec((None, tm * N_HEADS, 2 * D_HEAD_A), lambda i: (layer, i, 0))
    stack_shape = jax.ShapeDtypeStruct((depth, m * N_HEADS, 2 * D_HEAD_A), F32)
    return pl.pallas_call(
        _proj_in_kernel,
        grid=(m // tm,),
        in_specs=[pl.BlockSpec((tm, D_MODEL), row), pl.BlockSpec((D_MODEL, n), lambda i: (0, 0)),
                  pl.BlockSpec(memory_space=pl.ANY), pl.BlockSpec(memory_space=pl.ANY)],
        out_specs=[pl.BlockSpec((tm, W_A), row), stack_spec, stack_spec,
                   pl.BlockSpec((tm, W_A), row), pl.BlockSpec((tm, W_A), row),
                   pl.BlockSpec((tm, PROJ_B), row), pl.BlockSpec((tm, PROJ_C), row)],
        out_shape=[jax.ShapeDtypeStruct((m, W_A), BF16), stack_shape, stack_shape,
                   jax.ShapeDtypeStruct((m, W_A), BF16),
                   jax.ShapeDtypeStruct((m, W_A), BF16), jax.ShapeDtypeStruct((m, PROJ_B), F32),
                   jax.ShapeDtypeStruct((m, PROJ_C), F32)],
        input_output_aliases={2: 1, 3: 2},
        compiler_params=_cparams("parallel"),
        name="proj_in",
    )(x2d, w_bf, k_all, v_all)


def _lambda_full(lamv_ref, laminit_ref):
    lv = lamv_ref[...]
    s1 = jnp.sum(lv[0:1] * lv[1:2], axis=-1, keepdims=True)
    s2 = jnp.sum(lv[2:3] * lv[3:4], axis=-1, keepdims=True)
    return jnp.exp(s1) - jnp.exp(s2) + laminit_ref[...]


def _sub_rms(o, g, scale):
    y = o * lax.rsqrt(jnp.mean(o * o, axis=-1, keepdims=True) + RMS_EPS)
    return y * g * scale


def _rel_bucket(n):
    max_exact = N_BUCKETS // 2
    large = max_exact + (jnp.log(jnp.maximum(n, 1).astype(F32) / max_exact)
                         / math.log(MAX_DISTANCE / max_exact) * (N_BUCKETS - max_exact)).astype(jnp.int32)
    large = jnp.minimum(large, N_BUCKETS - 1)
    return jnp.where(n < max_exact, n, large)


V_AUG = 2 * D_HEAD_A + 16
PAIR = 2
HPS = 4


def _attn_prompt_kernel(lamv_ref, laminit_ref, q_ref, k_ref, v_ref, near_ref, g_ref, o_ref, *scratch, blk, seq):
    i = pl.program_id(2)
    dv = 2 * D_HEAD_A
    kp = PAIR * blk
    per_head = 1 + 2 + 2 + 4 + 4
    heads = []
    for h in range(HPS):
        sc = scratch[h * per_head:(h + 1) * per_head]
        heads.append(dict(vt=sc[0], m=sc[1:3], acc=sc[3:5], s=(sc[5:7], sc[7:9]), cm=(sc[9:11], sc[11:13])))
    lane = lax.broadcasted_iota(jnp.int32, (1, dv), 1)
    lo = lane < D_HEAD_A
    zero = jnp.zeros((), BF16)

    @pl.when(i == 0)
    def _():
        for h, hd in enumerate(heads):
            for c in range(seq // blk):
                vt = v_ref[c * blk:(c + 1) * blk, h * dv:(h + 1) * dv].astype(F32).T
                hd["vt"][0:dv, c * blk:(c + 1) * blk] = vt.astype(BF16)
            hd["vt"][dv:V_AUG, :] = jnp.ones((V_AUG - dv, seq), BF16)

    for h, hd in enumerate(heads):
        q = q_ref[:, h * dv:(h + 1) * dv]
        hd["q"] = (jnp.where(lo, q, zero), jnp.where(lo, zero, q))
        for mp in range(2):
            hd["m"][mp][...] = jnp.full(hd["m"][mp].shape, NEG, F32)
            hd["acc"][mp][...] = jnp.zeros(hd["acc"][mp].shape, F32)

    def stage_a(pair, slot, near_idx):
        off = pl.multiple_of(pair * kp, kp)
        for h, hd in enumerate(heads):
            cmax = [None, None]
            for sub in range(PAIR):
                kblk = k_ref[pl.ds(off + sub * blk, blk), h * dv:(h + 1) * dv]
                for mp in range(2):
                    s = _dot_nt(kblk, hd["q"][mp])
                    if near_idx is not None:
                        s = s + near_ref[h, near_idx, sub * blk:(sub + 1) * blk, :]
                    hd["s"][slot][mp][sub] = s
                    smax = jnp.max(s, axis=0, keepdims=True)
                    cmax[mp] = smax if cmax[mp] is None else jnp.maximum(cmax[mp], smax)
            for mp in range(2):
                hd["cm"][slot][mp][...] = cmax[mp]

    def stage_b(pair, slot):
        off = pl.multiple_of(pair * kp, kp)
        for hd in heads:
            vtb = hd["vt"][:, pl.ds(off, kp)]
            for mp in range(2):
                m_old = hd["m"][mp][...]
                mn = jnp.maximum(m_old, hd["cm"][slot][mp][...])
                p = jnp.concatenate([jnp.exp2(hd["s"][slot][mp][sub] - mn).astype(BF16) for sub in range(PAIR)],
                                    axis=0)
                hd["m"][mp][...] = mn
                hd["acc"][mp][...] = jnp.exp2(m_old - mn) * hd["acc"][mp][...] + _dot(vtb, p)

    n_pairs = (i + 2) // 2
    t1 = n_pairs - 1
    t0 = jnp.maximum(n_pairs - 2, 0)
    n_far = t0
    stage_a(t1, 0, 1)
    stage_a(t0, 1, 0)
    stage_b(t1, 0)

    def far_body(j, carry):
        stage_a(2 * j, 0, None)
        stage_b(jnp.where(j == 0, t0, 2 * j - 1), 1)
        stage_a(2 * j + 1, 1, None)
        stage_b(2 * j, 0)
        return carry

    lax.fori_loop(0, n_far // 2, far_body, 0)
    pending = jnp.where(n_far < 2, t0, ((n_far // 2) * 2) - 1)

    @pl.when(n_far % 2 == 1)
    def _():
        stage_a(n_far - 1, 0, None)
        stage_b(pending, 1)
        stage_b(n_far - 1, 0)

    @pl.when(n_far % 2 == 0)
    def _():
        stage_b(pending, 1)

    lam = _lambda_full(lamv_ref, laminit_ref)
    for h, hd in enumerate(heads):
        a1, a2 = hd["acc"][0][...], hd["acc"][1][...]
        o_t = a1[0:dv] / a1[dv:dv + 1] - lam * (a2[0:dv] / a2[dv:dv + 1])
        o_ref[:, h * dv:(h + 1) * dv] = _sub_rms(o_t.T, g_ref[...], 1.0 - laminit_ref[...]).astype(BF16)


def _near_bias_tiles(rel_bias, blk):
    cidx = jnp.arange(blk, dtype=jnp.int32)[:, None]
    ridx = jnp.arange(blk, dtype=jnp.int32)[None, :]
    rel = (rel_bias - rel_bias[N_BUCKETS - 1][None, :]) * LOG2E

    def lookup(dist):
        onehot = (_rel_bucket(dist)[..., None] == jnp.arange(N_BUCKETS, dtype=jnp.int32)).astype(F32)
        return jnp.einsum("crb,bh->hcr", onehot, rel, precision=lax.Precision.HIGHEST)

    dist0 = ridx - cidx
    diag = jnp.where((dist0 >= 0)[None], lookup(jnp.maximum(dist0, 0)), NEG)
    prev = lookup(dist0 + blk)
    masked = jnp.full_like(diag, NEG)
    zeros = jnp.zeros_like(diag)
    cat = lambda a, b: jnp.concatenate([a, b], axis=1)
    variants = [(cat(masked, masked), cat(diag, masked)), (cat(masked, masked), cat(prev, diag)),
                (cat(zeros, prev), cat(diag, masked)), (cat(zeros, zeros), cat(prev, diag))]
    return jnp.stack([jnp.stack(v, axis=1) for v in variants], axis=1)


def _attn_prompt(qa, kb, vb, lamv, laminit, near_bias, subln_g, batch, seq, blk):
    m = batch * seq
    nq = seq // blk
    dv = 2 * D_HEAD_A
    assert blk >= NEAR_DIST - 1 and nq % PAIR == 0 and N_HEADS % HPS == 0
    qspec = pl.BlockSpec((blk, HPS * dv), lambda b, h, i: (b * nq + i, h))
    kvspec = pl.BlockSpec((seq, HPS * dv), lambda b, h, i: (b, h))
    near = pl.BlockSpec((HPS, None, 2, PAIR * blk, blk),
                        lambda b, h, i: (h, jnp.where(i < 2, i, 2 + (i & 1)), 0, 0, 0))
    const2 = lambda shape: pl.BlockSpec(shape, lambda b, h, i: (0, 0))
    per_head = ([pltpu.VMEM((V_AUG, seq), BF16)] + [pltpu.VMEM((1, blk), F32)] * 2
                + [pltpu.VMEM((V_AUG, blk), F32)] * 2 + [pltpu.VMEM((PAIR, blk, blk), F32)] * 4
                + [pltpu.VMEM((1, blk), F32)] * 4)
    return pl.pallas_call(
        functools.partial(_attn_prompt_kernel, blk=blk, seq=seq),
        grid=(batch, N_HEADS // HPS, nq),
        in_specs=[const2((4, D_HEAD_A)), const2((1, 1)),
                  qspec, kvspec, kvspec, near, const2((1, dv))],
        out_specs=qspec,
        out_shape=jax.ShapeDtypeStruct((m, W_A), BF16),
        scratch_shapes=per_head * HPS,
        compiler_params=_cparams("parallel", "parallel", "arbitrary"),
        name="attn_prompt",
    )(lamv, laminit, qa, kb, vb, near_bias, subln_g.reshape(1, -1))


def _attn_sample_kernel(pt_ref, lamv_ref, laminit_ref, q_ref, kn_ref, vn_ref, bfar_ref, blast_ref, bnew_ref, g_ref,
                        *refs, group, n_steps, rows):
    k_refs = refs[:group]
    v_refs = refs[group:2 * group]
    o_ref = refs[2 * group]
    m_sc, l_sc, acc_sc = refs[2 * group + 1:]
    s_id = pl.program_id(1)
    cols = k_refs[0].shape[0]

    q = q_ref[...]
    lane = lax.broadcasted_iota(jnp.int32, (1, 2 * D_HEAD_A), 1)
    lo = lane < D_HEAD_A
    zero = jnp.zeros((), BF16)
    wq = jnp.concatenate([jnp.where(lo, q, zero), jnp.where(lo, zero, q)], axis=0)

    def update(s, pv_of):
        m = m_sc[...]
        mn = jnp.maximum(m, jnp.max(s, axis=-1, keepdims=True))
        al = jnp.exp2(m - mn)
        p = jnp.exp2(s - mn)
        m_sc[...] = mn
        l_sc[...] = al * l_sc[...] + jnp.sum(p, axis=-1, keepdims=True)
        acc_sc[...] = al * acc_sc[...] + pv_of(p.astype(BF16))

    @pl.when(s_id == 0)
    def _():
        m_sc[...] = jnp.full(m_sc.shape, NEG, F32)
        l_sc[...] = jnp.zeros(l_sc.shape, F32)
        acc_sc[...] = jnp.zeros(acc_sc.shape, F32)
        update(_dot_nt(wq, kn_ref[...]) + bnew_ref[...], lambda p: _dot(p, vn_ref[...]))

    bfar = bfar_ref[...]
    parts = []
    for g in range(group):
        bias = bfar
        if g == group - 1:
            bias = jnp.where(s_id == n_steps - 1, blast_ref[...], bfar)
        parts.append(_dot_nt(wq, k_refs[g][...].astype(BF16)) + bias)

    def pv_pages(p):
        return sum(_dot(p[:, g * cols:(g + 1) * cols], v_refs[g][...].astype(BF16)) for g in range(group))

    update(jnp.concatenate(parts, axis=1), pv_pages)

    @pl.when(s_id == n_steps - 1)
    def _():
        lam = _lambda_full(lamv_ref, laminit_ref)
        o = acc_sc[...] / l_sc[...]
        od = o[0:rows] - lam * o[rows:2 * rows]
        o_ref[...] = _sub_rms(od, g_ref[...], 1.0 - laminit_ref[...]).astype(BF16)


def _sample_bias_tiles(rel_bias, t_new, n_pages, page):
    rows = t_new * N_HEADS
    cols = page * N_HEADS
    past = n_pages * page
    assert page + 1 >= NEAR_DIST
    rel = (rel_bias - rel_bias[N_BUCKETS - 1][None, :]) * LOG2E
    r = jnp.arange(2 * rows, dtype=jnp.int32)[:, None]
    rt, rh = (r % rows) // N_HEADS, r % N_HEADS
    rel_rows = rel[:, rh[:, 0]]

    def lookup(dist):
        onehot = (_rel_bucket(dist)[..., None] == jnp.arange(N_BUCKETS, dtype=jnp.int32)).astype(F32)
        return jnp.einsum("rcb,br->rc", onehot, rel_rows, precision=lax.Precision.HIGHEST)

    c = jnp.arange(cols, dtype=jnp.int32)[None, :]
    ct, ch = c // N_HEADS, c % N_HEADS
    same = rh == ch
    bfar = jnp.where(same, 0.0, NEG).astype(F32)
    dist_last = (past + rt) - ((n_pages - 1) * page + ct)
    blast = jnp.where(same, lookup(dist_last), NEG).astype(F32)
    cn = jnp.arange(rows, dtype=jnp.int32)[None, :]
    cnt, cnh = cn // N_HEADS, cn % N_HEADS
    dist_new = rt - cnt
    bnew = jnp.where((rh == cnh) & (dist_new >= 0), lookup(jnp.maximum(dist_new, 0)), NEG).astype(F32)
    return bfar, blast, bnew


def _attn_sample(qa, kb, vb, cache_k4, cache_v4, pt_flat, layer, lamv, laminit, bias_tiles, subln_g,
                 n_dec, t_new, n_pages, page, group):
    rows = t_new * N_HEADS
    cols = page * N_HEADS
    n_steps = n_pages // group
    bfar, blast, bnew = bias_tiles
    q3 = qa.reshape(n_dec, rows, 2 * D_HEAD_A)
    kn3 = kb.reshape(n_dec, rows, 2 * D_HEAD_A)
    vn3 = vb.reshape(n_dec, rows, 2 * D_HEAD_A)
    per_b = pl.BlockSpec((None, rows, 2 * D_HEAD_A), lambda b, s, pt: (b, 0, 0))
    const = lambda shape: pl.BlockSpec(shape, lambda b, s, pt: (0, 0))

    def page_spec(g):
        return pl.BlockSpec((None, None, cols, 2 * D_HEAD_A),
                            lambda b, s, pt: (layer, pt[b * n_pages + s * group + g], 0, 0))

    in_specs = [const((4, D_HEAD_A)), const((1, 1)), per_b, per_b, per_b,
                const((2 * rows, cols)), const((2 * rows, cols)), const((2 * rows, rows)),
                const((1, 2 * D_HEAD_A))]
    in_specs += [page_spec(g) for g in range(group)] * 2
    out = pl.pallas_call(
        functools.partial(_attn_sample_kernel, group=group, n_steps=n_steps, rows=rows),
        grid_spec=pltpu.PrefetchScalarGridSpec(
            num_scalar_prefetch=1,
            grid=(n_dec, n_steps),
            in_specs=in_specs,
            out_specs=per_b,
            scratch_shapes=[pltpu.VMEM((2 * rows, 1), F32), pltpu.VMEM((2 * rows, 1), F32),
                            pltpu.VMEM((2 * rows, 2 * D_HEAD_A), F32)],
        ),
        out_shape=jax.ShapeDtypeStruct((n_dec, rows, 2 * D_HEAD_A), BF16),
        compiler_params=_cparams("parallel", "arbitrary"),
        name="attn_sample",
    )(pt_flat, lamv, laminit, q3, kn3, vn3, bfar, blast, bnew, subln_g.reshape(1, -1),
      *([cache_k4] * group), *([cache_v4] * group))
    return out.reshape(n_dec * t_new, W_A)


def _ret_kernel(pb_ref, cos_ref, sin_ref, s0_ref, o_ref, sfin_ref, st_sc, *, chunk, n_sub, n_valid, n_steps):
    c_id = pl.program_id(1)
    lg = chunk.bit_length() - 1
    rows = N_HEADS * chunk
    tb = n_sub * chunk
    masks = _head_masks()
    ones_bd = _head_ones()

    @pl.when(c_id == 0)
    def _():
        st_sc[...] = _load_block_diag(s0_ref)

    pb = pb_ref[...]
    q, k, v, g = pb[:, 0:W_BC], pb[:, W_BC:2 * W_BC], pb[:, 2 * W_BC:3 * W_BC], pb[:, 3 * W_BC:4 * W_BC]
    lane = lax.broadcasted_iota(jnp.int32, (1, W_BC), 1)
    first_half = (lane & (D_HEAD - 1)) < (D_HEAD // 2)
    cosf, sins = cos_ref[...], sin_ref[...]

    def rot(x):
        swapped = jnp.where(first_half, pltpu.roll(x, W_BC - D_HEAD // 2, 1), pltpu.roll(x, D_HEAD // 2, 1))
        return x * cosf + swapped * sins

    q = rot(q)
    k = rot(k) * (D_HEAD ** -0.5)
    row = lax.broadcasted_iota(jnp.int32, (tb, 1), 0)
    if n_valid < chunk:
        valid = row < n_valid
        k = jnp.where(valid, k, 0.0)
        v = jnp.where(valid, v, 0.0)

    log_g = [math.log1p(-(2.0 ** (-5 - h))) for h in range(N_HEADS)]
    lg_lane = sum(jnp.where(m, log_g[h], 0.0) for h, m in enumerate(masks))
    ri = lax.broadcasted_iota(jnp.int32, (rows, rows), 0)
    ci = lax.broadcasted_iota(jnp.int32, (rows, rows), 1)
    rh = lax.broadcasted_iota(jnp.int32, (rows, 1), 0) >> lg
    rpos = (lax.broadcasted_iota(jnp.int32, (rows, 1), 0) & (chunk - 1)).astype(F32)
    lg_row = sum(jnp.where(rh == h, log_g[h], 0.0) for h in range(N_HEADS))
    diff = (ri & (chunk - 1)) - (ci & (chunk - 1))
    keep = ((ri >> lg) == (ci >> lg)) & (diff >= 0)
    dmask = jnp.where(keep, jnp.exp(lg_row * jnp.maximum(diff, 0).astype(F32)), 0.0)
    xi = jnp.exp(lg_row * (rpos + 1.0))
    posf = (row & (chunk - 1)).astype(F32)
    zeta = jnp.exp(lg_lane * jnp.maximum(float(n_valid - 1) - posf, 0.0))
    g_chunk = jnp.exp(lg_lane * float(n_valid))
    qb, kb, vb, kzb = q.astype(BF16), k.astype(BF16), v.astype(BF16), (k * zeta).astype(BF16)

    def prepare(c):
        sl = slice(c * chunk, (c + 1) * chunk)
        q_st, k_st, v_st = (_stack_heads(x[sl], masks) for x in (qb, kb, vb))
        s_blk = _dot_nt(q_st, k_st) * dmask
        return q_st, _dot(s_blk.astype(BF16), v_st), _dot_tn(_stack_heads(kzb[sl], masks), v_st)

    prepared = [prepare(c) for c in range(n_sub)]
    state = st_sc[...]
    o_chunks = []
    for q_st, inner, incr in prepared:
        cross = _dot(q_st, state.astype(BF16)) * xi
        o_chunks.append(_unstack_heads(inner + cross, chunk))
        state = state * g_chunk + incr
    st_sc[...] = state

    o = o_chunks[0] if n_sub == 1 else jnp.concatenate(o_chunks, axis=0)
    ms = _head_sum(o * o, ones_bd) * (1.0 / D_HEAD)
    o_ref[...] = (jax.nn.silu(g) * (o * lax.rsqrt(ms + RMS_EPS))).astype(BF16)

    @pl.when(c_id == n_steps - 1)
    def _():
        _store_block_diag(sfin_ref, state)


def _retention(pb, cos_t, sin_t, s0, batch, t_pad, chunk, n_sub, n_valid):
    tb = chunk * n_sub
    n_steps = t_pad // tb
    rowspec = lambda w: pl.BlockSpec((tb, w), lambda b, c: (b * n_steps + c, 0))
    stspec = pl.BlockSpec((None, N_HEADS, D_HEAD, D_HEAD), lambda b, c: (b, 0, 0, 0))
    return pl.pallas_call(
        functools.partial(_ret_kernel, chunk=chunk, n_sub=n_sub, n_valid=n_valid, n_steps=n_steps),
        grid=(batch, n_steps),
        in_specs=[rowspec(PROJ_B), pl.BlockSpec((tb, W_BC), lambda b, c: (c, 0)),
                  pl.BlockSpec((tb, W_BC), lambda b, c: (c, 0)), stspec],
        out_specs=[rowspec(W_BC), stspec],
        out_shape=[jax.ShapeDtypeStruct((batch * t_pad, W_BC), BF16),
                   jax.ShapeDtypeStruct((batch, N_HEADS, D_HEAD, D_HEAD), F32)],
        scratch_shapes=[pltpu.VMEM((W_BC, W_BC), F32)],
        compiler_params=_cparams("parallel", "arbitrary"),
        name="retention",
    )(pb, cos_t, sin_t, s0)


def _rwkv_kernel(pc_ref, shift_ref, s0_ref, mu_ref, w0_ref, w2_ref, a0_ref, a2_ref, g2_ref, kk_ref, ka_ref,
                 rk_ref, lng_ref, lnb_ref, o_ref, sfin_ref, st_sc, prev_sc, *, chunk, n_sub, n_valid, n_steps):
    c_id = pl.program_id(1)
    lg = chunk.bit_length() - 1
    rows = N_HEADS * chunk
    tb = n_sub * chunk
    masks = _head_masks()
    ones_bd = _head_ones()

    @pl.when(c_id == 0)
    def _():
        st_sc[...] = _load_block_diag(s0_ref)
        prev_sc[...] = shift_ref[...]

    pc = pc_ref[...]
    row = lax.broadcasted_iota(jnp.int32, (tb, 1), 0)
    prev = jnp.where(row == 0, prev_sc[...], pltpu.roll(pc, 1, 0))
    prev_sc[...] = pc[tb - 1:tb, :]
    xm = pc + (prev - pc) * mu_ref[...]
    r, k, v, xt = xm[:, 0:W_BC], xm[:, W_BC:2 * W_BC], xm[:, 2 * W_BC:3 * W_BC], xm[:, 3 * W_BC:4 * W_BC]

    w_log = -_softplus(-(w0_ref[...] + _dot(jnp.tanh(xt).astype(BF16), w2_ref[...]))) - 0.5
    logw = -jnp.exp(w_log)
    a = jax.nn.sigmoid(a0_ref[...] + _dot(xt.astype(BF16), a2_ref[...]))
    g = _dot(jax.nn.sigmoid(xt).astype(BF16), g2_ref[...])

    kk = k * kk_ref[...]
    kk = kk / jnp.maximum(jnp.sqrt(_head_sum(kk * kk, ones_bd)), 1e-12)
    k2 = k * (1.0 + (a - 1.0) * ka_ref[...])
    if n_valid < chunk:
        valid = row < n_valid
        logw = jnp.where(valid, logw, 0.0)
        kk = jnp.where(valid, kk, 0.0)
        k2 = jnp.where(valid, k2, 0.0)
        v = jnp.where(valid, v, 0.0)

    ti = lax.broadcasted_iota(jnp.int32, (tb, tb), 0)
    tj = lax.broadcasted_iota(jnp.int32, (tb, tb), 1)
    same_chunk = (ti >> lg) == (tj >> lg)
    tri = jnp.where(same_chunk & (ti >= tj), 1.0, 0.0).astype(BF16)
    cum = sum(_dot(tri, part) for part in _split3(logw))
    ends = [jnp.broadcast_to(cum[(c + 1) * chunk - 1:(c + 1) * chunk], (chunk, W_BC)) for c in range(n_sub)]
    cum_end = ends[0] if n_sub == 1 else jnp.concatenate(ends, axis=0)
    e_bwd = jnp.exp(-cum)
    e_end = jnp.exp(cum_end - cum)
    kka = kk * a
    a_t = -kk * jnp.exp(cum - logw)
    b_t = kka * e_bwd
    k_t = k2 * e_bwd
    r_t = r * jnp.exp(cum)
    b_h = kka * e_end
    k_h = k2 * e_end
    p_end = jnp.exp(cum_end)

    ri = lax.broadcasted_iota(jnp.int32, (rows, rows), 0)
    ci = lax.broadcasted_iota(jnp.int32, (rows, rows), 1)
    same = (ri >> lg) == (ci >> lg)
    dpos = (ri & (chunk - 1)) - (ci & (chunk - 1))
    strict = same & (dpos > 0)
    incl = same & (dpos >= 0)
    eye = jnp.where(ri == ci, 1.0, 0.0)

    def prepare(c):
        sl = slice(c * chunk, (c + 1) * chunk)
        st = lambda x: _stack_heads(x[sl].astype(BF16), masks)
        a_st, b_st, k_st, r_st, v_st, bh_st, kh_st = st(a_t), st(b_t), st(k_t), st(r_t), st(v), st(b_h), st(k_h)
        n_bd = jnp.where(strict, _dot_nt(a_st, b_st), 0.0)
        ak_bd = jnp.where(strict, _dot_nt(a_st, k_st), 0.0)
        m_rb = jnp.where(incl, _dot_nt(r_st, b_st), 0.0).astype(BF16)
        m_rk = jnp.where(incl, _dot_nt(r_st, k_st), 0.0).astype(BF16)
        t_inv = eye + n_bd
        pw = n_bd
        for _ in range(lg - 1):
            pwb = pw.astype(BF16)
            pw = _dot(pwb, pwb)
            t_inv = t_inv + _dot(t_inv.astype(BF16), pw.astype(BF16))
        tbf = t_inv.astype(BF16)
        w_st = _dot(tbf, a_st).astype(BF16)
        y_st = _dot(tbf, _dot(ak_bd.astype(BF16), v_st).astype(BF16))
        o_pre = _dot(m_rk, v_st)
        g_pre = _dot_tn(v_st, kh_st)
        return w_st, y_st, r_st, m_rb, o_pre, bh_st, g_pre, p_end[c * chunk:c * chunk + 1]

    prepared = [prepare(c) for c in range(n_sub)]

    state = st_sc[...]
    o_chunks = []
    for w_st, y_st, r_st, m_rb, o_pre, bh_st, g_pre, p_c in prepared:
        sb = state.astype(BF16)
        u = _dot_nt(w_st, sb) + y_st
        ub = u.astype(BF16)
        o_chunks.append(_unstack_heads(_dot_nt(r_st, sb) + _dot(m_rb, ub) + o_pre, chunk))
        state = state * p_c + _dot_tn(ub, bh_st) + g_pre
    st_sc[...] = state

    o = o_chunks[0] if n_sub == 1 else jnp.concatenate(o_chunks, axis=0)
    inv = 1.0 / D_HEAD
    mean = _head_sum(o, ones_bd) * inv
    d = o - mean
    var = _head_sum(d * d, ones_bd) * inv
    o_n = d * lax.rsqrt(var + RWKV_GN_EPS) * lng_ref[...] + lnb_ref[...]
    bonus = _head_sum(r * k2 * rk_ref[...], ones_bd) * v
    o_ref[...] = ((o_n + bonus) * g).astype(BF16)

    @pl.when(c_id == n_steps - 1)
    def _():
        _store_block_diag(sfin_ref, state)


def _rwkv(pc, shift0, s0, params, batch, t_pad, chunk, n_sub, n_valid):
    tb = chunk * n_sub
    n_steps = t_pad // tb
    rowspec = lambda w: pl.BlockSpec((tb, w), lambda b, c: (b * n_steps + c, 0))
    stspec = pl.BlockSpec((None, N_HEADS, D_HEAD, D_HEAD), lambda b, c: (b, 0, 0, 0))
    const = lambda arr: pl.BlockSpec(arr.shape, lambda b, c: (0,) * arr.ndim)
    return pl.pallas_call(
        functools.partial(_rwkv_kernel, chunk=chunk, n_sub=n_sub, n_valid=n_valid, n_steps=n_steps),
        grid=(batch, n_steps),
        in_specs=[rowspec(PROJ_C), pl.BlockSpec((None, 1, PROJ_C), lambda b, c: (b, 0, 0)), stspec]
                 + [const(p) for p in params],
        out_specs=[rowspec(W_BC), stspec],
        out_shape=[jax.ShapeDtypeStruct((batch * t_pad, W_BC), BF16),
                   jax.ShapeDtypeStruct((batch, N_HEADS, D_HEAD, D_HEAD), F32)],
        scratch_shapes=[pltpu.VMEM((W_BC, W_BC), F32), pltpu.VMEM((1, PROJ_C), F32)],
        compiler_params=_cparams("parallel", "arbitrary"),
        name="rwkv7",
    )(pc, shift0, s0, *params)


def _out_proj_kernel(x_ref, a_ref, b_ref, c_ref, w_ref, g_ref, bb_ref, y_ref, *, alpha):
    acc = _dot(a_ref[...], w_ref[0:W_A, :])
    acc += _dot(b_ref[...], w_ref[W_A:W_A + W_BC, :])
    acc += _dot(c_ref[...], w_ref[W_A + W_BC:W_A + 2 * W_BC, :])
    y_ref[...] = _layer_norm(alpha * x_ref[...] + acc, g_ref[...], bb_ref[...])


def _out_proj(x2d, oa, ob, oc, w_bf, g, b, alpha, tm):
    m = x2d.shape[0]
    row = lambda w: pl.BlockSpec((tm, w), lambda i: (i, 0))
    const = lambda shape: pl.BlockSpec(shape, lambda i: (0, 0))
    return pl.pallas_call(
        functools.partial(_out_proj_kernel, alpha=alpha),
        grid=(m // tm,),
        in_specs=[row(D_MODEL), row(W_A), row(W_BC), row(W_BC), const((D_MODEL, D_MODEL)),
                  const((1, D_MODEL)), const((1, D_MODEL))],
        out_specs=row(D_MODEL),
        out_shape=jax.ShapeDtypeStruct((m, D_MODEL), F32),
        compiler_params=_cparams("parallel"),
        name="out_proj",
    )(x2d, oa, ob, oc, w_bf, g.reshape(1, -1), b.reshape(1, -1))


def _ffn_kernel(x_ref, w1_ref, w3_ref, w2_ref, g_ref, b_ref, y_ref, acc_sc, *, alpha):
    f = pl.program_id(1)
    x = x_ref[...]
    xb = x.astype(BF16)

    @pl.when(f == 0)
    def _():
        acc_sc[...] = jnp.zeros(acc_sc.shape, F32)

    h = jax.nn.silu(_dot(xb, w1_ref[...])) * _dot(xb, w3_ref[...])
    acc_sc[...] += _dot(h.astype(BF16), w2_ref[...])

    @pl.when(f == pl.num_programs(1) - 1)
    def _():
        y_ref[...] = _layer_norm(alpha * x + acc_sc[...], g_ref[...], b_ref[...])


def _ffn(x2d, w1, w3, w2, g, b, alpha, tm, tf):
    m = x2d.shape[0]
    d_ff = w1.shape[1]
    return pl.pallas_call(
        functools.partial(_ffn_kernel, alpha=alpha),
        grid=(m // tm, d_ff // tf),
        in_specs=[pl.BlockSpec((tm, D_MODEL), lambda i, f: (i, 0)),
                  pl.BlockSpec((D_MODEL, tf), lambda i, f: (0, f)),
                  pl.BlockSpec((D_MODEL, tf), lambda i, f: (0, f)),
                  pl.BlockSpec((tf, D_MODEL), lambda i, f: (f, 0)),
                  pl.BlockSpec((1, D_MODEL), lambda i, f: (0, 0)),
                  pl.BlockSpec((1, D_MODEL), lambda i, f: (0, 0))],
        out_specs=pl.BlockSpec((tm, D_MODEL), lambda i, f: (i, 0)),
        out_shape=jax.ShapeDtypeStruct((m, D_MODEL), F32),
        scratch_shapes=[pltpu.VMEM((tm, D_MODEL), F32)],
        compiler_params=_cparams("parallel", "arbitrary"),
        name="ffn",
    )(x2d, w1, w3, w2, g.reshape(1, -1), b.reshape(1, -1))


def _moe_kernel(x_ref, rh_ref, rl_ref, w1_ref, w3_ref, w2_ref, g_ref, b_ref, y_ref, acc_sc, gate_sc, xb_sc,
                *, alpha, n_experts):
    e = pl.program_id(1)
    lane = lax.broadcasted_iota(jnp.int32, (1, LANES), 1)

    @pl.when(e == 0)
    def _():
        x = x_ref[...]
        xh, xl = _split2(x)
        xb_sc[...] = xh
        logits = _dot(xh, rh_ref[...]) + _dot(xl, rh_ref[...]) + _dot(xh, rl_ref[...])
        logits = jnp.where(lane < n_experts, logits, NEG)
        v1 = jnp.max(logits, axis=-1, keepdims=True)
        i1 = jnp.min(jnp.where(logits == v1, lane, LANES), axis=-1, keepdims=True)
        rest = jnp.where(lane == i1, NEG, logits)
        v2 = jnp.max(rest, axis=-1, keepdims=True)
        i2 = jnp.min(jnp.where(rest == v2, lane, LANES), axis=-1, keepdims=True)
        ex = jnp.exp(v2 - v1)
        g1 = 1.0 / (1.0 + ex)
        g2 = ex / (1.0 + ex)
        gate_sc[...] = jnp.where(lane == i1, g1, 0.0) + jnp.where(lane == i2, g2, 0.0)
        acc_sc[...] = jnp.zeros(acc_sc.shape, F32)

    xb = xb_sc[...]
    h = jax.nn.silu(_dot(xb, w1_ref[...])) * _dot(xb, w3_ref[...])
    f = _dot(h.astype(BF16), w2_ref[...])
    ge = jnp.sum(jnp.where(lane == e, gate_sc[...], 0.0), axis=-1, keepdims=True)
    acc_sc[...] += ge * f

    @pl.when(e == n_experts - 1)
    def _():
        y_ref[...] = _layer_norm(alpha * x_ref[...] + acc_sc[...], g_ref[...], b_ref[...])


def _moe(x2d, router, w1, w3, w2, g, b, alpha, tm):
    m = x2d.shape[0]
    n_experts, _, d_e = w1.shape
    r_pad = jnp.zeros((D_MODEL, LANES), F32).at[:, :n_experts].set(router)
    r_hi = r_pad.astype(BF16)
    r_lo = (r_pad - r_hi.astype(F32)).astype(BF16)
    return pl.pallas_call(
        functools.partial(_moe_kernel, alpha=alpha, n_experts=n_experts),
        grid=(m // tm, n_experts),
        in_specs=[pl.BlockSpec((tm, D_MODEL), lambda i, e: (i, 0)),
                  pl.BlockSpec((D_MODEL, LANES), lambda i, e: (0, 0)),
                  pl.BlockSpec((D_MODEL, LANES), lambda i, e: (0, 0)),
                  pl.BlockSpec((None, D_MODEL, d_e), lambda i, e: (e, 0, 0)),
                  pl.BlockSpec((None, D_MODEL, d_e), lambda i, e: (e, 0, 0)),
                  pl.BlockSpec((None, d_e, D_MODEL), lambda i, e: (e, 0, 0)),
                  pl.BlockSpec((1, D_MODEL), lambda i, e: (0, 0)),
                  pl.BlockSpec((1, D_MODEL), lambda i, e: (0, 0))],
        out_specs=pl.BlockSpec((tm, D_MODEL), lambda i, e: (i, 0)),
        out_shape=jax.ShapeDtypeStruct((m, D_MODEL), F32),
        scratch_shapes=[pltpu.VMEM((tm, D_MODEL), F32), pltpu.VMEM((tm, LANES), F32),
                        pltpu.VMEM((tm, D_MODEL), BF16)],
        compiler_params=_cparams("parallel", "arbitrary"),
        name="moe",
    )(x2d, r_hi, r_lo, w1, w3, w2, g.reshape(1, -1), b.reshape(1, -1))


MOE_TILE = 512


def _moe_router_kernel(x_ref, rh_ref, rl_ref, info_ref, cnt_ref, carry_sc, *, n_experts):
    i = pl.program_id(0)
    tm = x_ref.shape[0]
    lane = lax.broadcasted_iota(jnp.int32, (1, LANES), 1)

    @pl.when(i == 0)
    def _():
        carry_sc[...] = jnp.zeros(carry_sc.shape, F32)

    xh, xl = _split2(x_ref[...])
    logits = _dot(xh, rh_ref[...]) + _dot(xl, rh_ref[...]) + _dot(xh, rl_ref[...])
    logits = jnp.where(lane < n_experts, logits, NEG)
    v1 = jnp.max(logits, axis=-1, keepdims=True)
    i1 = jnp.min(jnp.where(logits == v1, lane, LANES), axis=-1, keepdims=True)
    rest = jnp.where(lane == i1, NEG, logits)
    v2 = jnp.max(rest, axis=-1, keepdims=True)
    i2 = jnp.min(jnp.where(rest == v2, lane, LANES), axis=-1, keepdims=True)
    ex = jnp.exp(v2 - v1)
    g1 = 1.0 / (1.0 + ex)
    g2 = ex / (1.0 + ex)
    m1, m2 = lane == i1, lane == i2
    member = jnp.where(m1 | m2, 1.0, 0.0)
    r = lax.broadcasted_iota(jnp.int32, (tm, tm), 0)
    c = lax.broadcasted_iota(jnp.int32, (tm, tm), 1)
    before = jnp.where(r > c, 1.0, 0.0).astype(BF16)
    rank = _dot(before, member.astype(BF16)) + carry_sc[...]
    rank1 = jnp.sum(jnp.where(m1, rank, 0.0), axis=-1, keepdims=True)
    rank2 = jnp.sum(jnp.where(m2, rank, 0.0), axis=-1, keepdims=True)
    total = carry_sc[...] + jnp.sum(member, axis=0, keepdims=True)
    carry_sc[...] = total
    cnt_ref[...] = total
    fields = (i1.astype(F32), i2.astype(F32), g1, g2, rank1, rank2)
    info = jnp.zeros((tm, LANES), F32)
    for k, f in enumerate(fields):
        info = jnp.where(lane == k, f, info)
    info_ref[...] = info


def _moe_dispatch_kernel(pos1_ref, pos2_ref, x_ref, xs_in_ref, xs_ref, sem, *, tm):
    del xs_in_ref
    base = pl.program_id(0) * tm

    def row_copy(t, p):
        return pltpu.make_async_copy(x_ref.at[pl.ds(t, 1)], xs_ref.at[pl.ds(p, 1)], sem)

    def issue(t, carry):
        row_copy(t, pos1_ref[base + t]).start()
        row_copy(t, pos2_ref[base + t]).start()
        return carry

    def drain(t, carry):
        row_copy(0, 0).wait()
        row_copy(0, 0).wait()
        return carry

    lax.fori_loop(0, tm, issue, 0, unroll=8)
    lax.fori_loop(0, tm, drain, 0, unroll=8)


def _moe_expert_kernel(te_ref, nv_ref, x_ref, w1_ref, w3_ref, w2_ref, y_ref):
    j = pl.program_id(0)

    @pl.when(j < nv_ref[0])
    def _():
        xb = x_ref[...].astype(BF16)
        h = jax.nn.silu(_dot(xb, w1_ref[...])) * _dot(xb, w3_ref[...])
        y_ref[...] = _dot(h.astype(BF16), w2_ref[...])

    @pl.when(j >= nv_ref[0])
    def _():
        y_ref[...] = jnp.zeros(y_ref.shape, F32)


def _moe_combine_kernel(pos1_ref, pos2_ref, x_ref, info_ref, ys_ref, g_ref, b_ref, y_ref, buf1, buf2, sem,
                        *, tm, alpha):
    base = pl.program_id(0) * tm

    def row_copy(p, buf, t):
        return pltpu.make_async_copy(ys_ref.at[pl.ds(p, 1)], buf.at[pl.ds(t, 1)], sem)

    def issue(t, carry):
        row_copy(pos1_ref[base + t], buf1, t).start()
        row_copy(pos2_ref[base + t], buf2, t).start()
        return carry

    def drain(t, carry):
        row_copy(0, buf1, 0).wait()
        row_copy(0, buf2, 0).wait()
        return carry

    lax.fori_loop(0, tm, issue, 0, unroll=8)
    lax.fori_loop(0, tm, drain, 0, unroll=8)
    lane = lax.broadcasted_iota(jnp.int32, (1, LANES), 1)
    info = info_ref[...]
    g1 = jnp.sum(jnp.where(lane == 2, info, 0.0), axis=-1, keepdims=True)
    g2 = jnp.sum(jnp.where(lane == 3, info, 0.0), axis=-1, keepdims=True)
    f = g1 * buf1[...] + g2 * buf2[...]
    y_ref[...] = _layer_norm(alpha * x_ref[...] + f, g_ref[...], b_ref[...])


def _moe_routed(x2d, router, w1, w3, w2, g, b, alpha):
    m = x2d.shape[0]
    n_experts, _, d_e = w1.shape
    tile = MOE_TILE
    n_tiles = (2 * m) // tile + n_experts
    rows = n_tiles * tile
    r_pad = jnp.zeros((D_MODEL, LANES), F32).at[:, :n_experts].set(router)
    r_hi = r_pad.astype(BF16)
    r_lo = (r_pad - r_hi.astype(F32)).astype(BF16)
    row_spec = lambda w: pl.BlockSpec((tile, w), lambda i, *_: (i, 0))

    info, cnt = pl.pallas_call(
        functools.partial(_moe_router_kernel, n_experts=n_experts),
        grid=(m // tile,),
        in_specs=[row_spec(D_MODEL), pl.BlockSpec((D_MODEL, LANES), lambda i: (0, 0)),
                  pl.BlockSpec((D_MODEL, LANES), lambda i: (0, 0))],
        out_specs=[row_spec(LANES), pl.BlockSpec((1, LANES), lambda i: (0, 0))],
        out_shape=[jax.ShapeDtypeStruct((m, LANES), F32), jax.ShapeDtypeStruct((1, LANES), F32)],
        scratch_shapes=[pltpu.VMEM((1, LANES), F32)],
        compiler_params=_cparams("arbitrary"),
        name="moe_router",
    )(x2d, r_hi, r_lo)

    counts = cnt[0, :n_experts].astype(jnp.int32)
    tiles_e = (counts + tile - 1) // tile
    tile_end = jnp.cumsum(tiles_e)
    row_off = (tile_end - tiles_e) * tile
    eids = jnp.arange(n_experts, dtype=jnp.int32)

    def dest(col_e, col_r):
        e = info[:, col_e].astype(jnp.int32)
        off = jnp.sum(jnp.where(e[:, None] == eids[None, :], row_off[None, :], 0), axis=1)
        return off + info[:, col_r].astype(jnp.int32)

    pos1, pos2 = dest(0, 4), dest(1, 5)
    tile_ids = jnp.arange(n_tiles, dtype=jnp.int32)
    tile_expert = jnp.minimum(jnp.sum(tile_ids[:, None] >= tile_end[None, :], axis=1), n_experts - 1).astype(jnp.int32)
    n_valid = tile_end[n_experts - 1:].astype(jnp.int32)

    xs = pl.pallas_call(
        functools.partial(_moe_dispatch_kernel, tm=tile),
        grid_spec=pltpu.PrefetchScalarGridSpec(
            num_scalar_prefetch=2,
            grid=(m // tile,),
            in_specs=[row_spec(D_MODEL), pl.BlockSpec(memory_space=pl.ANY)],
            out_specs=pl.BlockSpec(memory_space=pl.ANY),
            scratch_shapes=[pltpu.SemaphoreType.DMA],
        ),
        out_shape=jax.ShapeDtypeStruct((rows, D_MODEL), F32),
        input_output_aliases={3: 0},
        compiler_params=_cparams("arbitrary"),
        name="moe_dispatch",
    )(pos1, pos2, x2d, jnp.zeros((rows, D_MODEL), F32))

    ys = pl.pallas_call(
        _moe_expert_kernel,
        grid_spec=pltpu.PrefetchScalarGridSpec(
            num_scalar_prefetch=2,
            grid=(n_tiles,),
            in_specs=[row_spec(D_MODEL),
                      pl.BlockSpec((None, D_MODEL, d_e), lambda j, te, nv: (te[j], 0, 0)),
                      pl.BlockSpec((None, D_MODEL, d_e), lambda j, te, nv: (te[j], 0, 0)),
                      pl.BlockSpec((None, d_e, D_MODEL), lambda j, te, nv: (te[j], 0, 0))],
            out_specs=row_spec(D_MODEL),
        ),
        out_shape=jax.ShapeDtypeStruct((rows, D_MODEL), F32),
        compiler_params=_cparams("arbitrary"),
        name="moe_experts",
    )(tile_expert, n_valid, xs, w1, w3, w2)

    return pl.pallas_call(
        functools.partial(_moe_combine_kernel, tm=tile, alpha=alpha),
        grid_spec=pltpu.PrefetchScalarGridSpec(
            num_scalar_prefetch=2,
            grid=(m // tile,),
            in_specs=[row_spec(D_MODEL), row_spec(LANES), pl.BlockSpec(memory_space=pl.ANY),
                      pl.BlockSpec((1, D_MODEL), lambda i, *_: (0, 0)),
                      pl.BlockSpec((1, D_MODEL), lambda i, *_: (0, 0))],
            out_specs=row_spec(D_MODEL),
            scratch_shapes=[pltpu.VMEM((tile, D_MODEL), F32), pltpu.VMEM((tile, D_MODEL), F32),
                            pltpu.SemaphoreType.DMA],
        ),
        out_shape=jax.ShapeDtypeStruct((m, D_MODEL), F32),
        compiler_params=_cparams("arbitrary"),
        name="moe_combine",
    )(pos1, pos2, x2d, info, ys, g.reshape(1, -1), b.reshape(1, -1))


def _rotary_tables(pos0, t_pad):
    half = D_HEAD // 2
    inv = 1.0 / (10000.0 ** (jnp.arange(half, dtype=F32) / half))
    ang = (pos0 + jnp.arange(t_pad, dtype=jnp.int32)).astype(F32)[:, None] * inv[None, :]
    cos, sin = jnp.cos(ang), jnp.sin(ang)
    cos_t = jnp.tile(jnp.concatenate([cos, cos], axis=-1), (1, N_HEADS))
    sin_t = jnp.tile(jnp.concatenate([-sin, sin], axis=-1), (1, N_HEADS))
    return cos_t, sin_t


def _pad_rows(lo, arr):
    return jnp.zeros((W_BC, W_BC), F32).at[lo:lo + arr.shape[0]].set(arr).astype(BF16)


def _ffn_tile(d_ff):
    best = LANES if d_ff % LANES == 0 else d_ff
    for t in range(LANES, min(d_ff, 1536) + 1, LANES):
        if d_ff % t == 0:
            best = t
    return best


def kernel(x_prompt, x_sample, cache_k, cache_v, page_table, state_ret, state_wkv, state_shift, rel_bias, w_in, w_out, lambda_q1, lambda_k1, lambda_q2, lambda_k2, subln_g, tshift_mu, decay_w0, decay_w2, iclr_a0, iclr_a2, gate_w2, k_k, k_a, r_k, lnx_g, lnx_b, ln1_g, ln1_b, ln2_g, ln2_b, ffn_w1, ffn_w3, ffn_w2, router_w, expert_w1, expert_w3, expert_w2):
    bp, seq, _ = x_prompt.shape
    n_dec, t_new, _ = x_sample.shape
    depth, n_pool, page, _, _ = cache_k.shape
    n_pages = page_table.shape[1]
    past = n_pages * page
    alpha = (2 * depth) ** 0.25

    mp, ms = bp * seq, n_dec * t_new
    tm_p = 256 if mp % 256 == 0 else mp
    tm_s = ms
    blk = 256 if seq % 256 == 0 else seq
    chunk_p = 64 if seq % 64 == 0 else seq
    sub_p = 4 if seq % (4 * chunk_p) == 0 else 1
    chunk_s = 8
    group = next(g for g in (32, 16, 8, 1) if n_pages % g == 0)

    cache_k4 = cache_k.reshape(depth, n_pool, page * N_HEADS, 2 * D_HEAD_A)
    cache_v4 = cache_v.reshape(depth, n_pool, page * N_HEADS, 2 * D_HEAD_A)
    pt_flat = page_table.reshape(-1).astype(jnp.int32)

    rel_bias = rel_bias.astype(F32)
    near_bias = _near_bias_tiles(rel_bias, blk)
    sample_tiles = _sample_bias_tiles(rel_bias, t_new, n_pages, page)
    cos_p, sin_p = _rotary_tables(0, seq)
    cos_s, sin_s = _rotary_tables(past, chunk_s)
    zero_st_p = jnp.zeros((bp, N_HEADS, D_HEAD, D_HEAD), F32)
    zero_shift_p = jnp.zeros((bp, 1, PROJ_C), F32)

    def pad_sample(a):
        w = a.shape[-1]
        a3 = a.reshape(n_dec, t_new, w)
        return jnp.pad(a3, ((0, 0), (0, chunk_s - t_new), (0, 0))).reshape(n_dec * chunk_s, w)

    def unpad_sample(a):
        w = a.shape[-1]
        return a.reshape(n_dec, chunk_s, w)[:, :t_new].reshape(n_dec * t_new, w)

    xp = x_prompt.reshape(mp, D_MODEL)
    xs = x_sample.reshape(ms, D_MODEL)
    kp_all = jnp.zeros((depth, mp * N_HEADS, 2 * D_HEAD_A), F32)
    vp_all = jnp.zeros((depth, mp * N_HEADS, 2 * D_HEAD_A), F32)
    ks_all = jnp.zeros((depth, ms * N_HEADS, 2 * D_HEAD_A), F32)
    vs_all = jnp.zeros((depth, ms * N_HEADS, 2 * D_HEAD_A), F32)
    outs = {k: [] for k in ("rp", "rs", "wp", "ws", "sp", "ss")}
    for l in range(depth):
        w_in_bf = w_in[l].astype(BF16)
        w_out_bf = w_out[l].astype(BF16)
        lam_init = 0.8 - 0.6 * math.exp(-0.3 * l)
        lamv = jnp.stack([lambda_q1[l], lambda_k1[l], lambda_q2[l], lambda_k2[l]]).astype(F32)
        laminit = jnp.full((1, 1), lam_init, F32)
        row = lambda a: a.reshape(1, -1).astype(F32)
        rwkv_params = [row(tshift_mu[l]), row(decay_w0[l]), _pad_rows(0, decay_w2[l]), row(iclr_a0[l]),
                       _pad_rows(64, iclr_a2[l]), _pad_rows(128, gate_w2[l]),
                       row(k_k[l]), row(k_a[l]), row(r_k[l]), row(lnx_g[l]), row(lnx_b[l])]

        qa, kp_all, vp_all, kb, vb, pb, pc = _proj_in(xp, w_in_bf, kp_all, vp_all, l, depth, tm_p)
        oa = _attn_prompt(qa, kb, vb, lamv, laminit, near_bias, subln_g[l], bp, seq, blk)
        ob, ret_p = _retention(pb, cos_p, sin_p, zero_st_p, bp, seq, chunk_p, sub_p, chunk_p)
        oc, wkv_p = _rwkv(pc, zero_shift_p, zero_st_p, rwkv_params, bp, seq, chunk_p, sub_p, chunk_p)
        xp = _out_proj(xp, oa, ob, oc, w_out_bf, ln1_g[l], ln1_b[l], alpha, 512 if mp % 512 == 0 else tm_p)
        outs["rp"].append(ret_p)
        outs["wp"].append(wkv_p)
        outs["sp"].append(pc.reshape(bp, seq, PROJ_C)[:, -1])

        qa, ks_all, vs_all, kb, vb, pb, pc = _proj_in(xs, w_in_bf, ks_all, vs_all, l, depth, tm_s)
        oa = _attn_sample(qa, kb, vb, cache_k4, cache_v4, pt_flat, l, lamv, laminit, sample_tiles, subln_g[l],
                          n_dec, t_new, n_pages, page, group)
        ob, ret_s = _retention(pad_sample(pb), cos_s, sin_s, state_ret[l], n_dec, chunk_s, chunk_s, 1, t_new)
        oc, wkv_s = _rwkv(pad_sample(pc), state_shift[l].astype(F32).reshape(n_dec, 1, PROJ_C),
                          state_wkv[l], rwkv_params, n_dec, chunk_s, chunk_s, 1, t_new)
        xs = _out_proj(xs, oa, unpad_sample(ob), unpad_sample(oc), w_out_bf, ln1_g[l], ln1_b[l], alpha, tm_s)
        outs["rs"].append(ret_s)
        outs["ws"].append(wkv_s)
        outs["ss"].append(pc.reshape(n_dec, t_new, PROJ_C)[:, -1])

        j = l // 2
        if l % 2 == 0:
            w1, w3, w2 = ffn_w1[j].astype(BF16), ffn_w3[j].astype(BF16), ffn_w2[j].astype(BF16)
            tf = _ffn_tile(w1.shape[1])
            xp = _ffn(xp, w1, w3, w2, ln2_g[l], ln2_b[l], alpha, min(512, mp), tf)
            xs = _ffn(xs, w1, w3, w2, ln2_g[l], ln2_b[l], alpha, tm_s, tf)
        else:
            w1, w3, w2 = expert_w1[j].astype(BF16), expert_w3[j].astype(BF16), expert_w2[j].astype(BF16)
            if mp % MOE_TILE == 0:
                xp = _moe_routed(xp, router_w[j], w1, w3, w2, ln2_g[l], ln2_b[l], alpha)
            else:
                xp = _moe(xp, router_w[j], w1, w3, w2, ln2_g[l], ln2_b[l], alpha, mp)
            xs = _moe(xs, router_w[j], w1, w3, w2, ln2_g[l], ln2_b[l], alpha, tm_s)

    st = lambda key: jnp.stack(outs[key], axis=0)
    kv_p = lambda a: a.reshape(depth, bp, seq, N_HEADS, 2 * D_HEAD_A)
    kv_s = lambda a: a.reshape(depth, n_dec, t_new, N_HEADS, 2 * D_HEAD_A)
    return (xp.reshape(bp, seq, D_MODEL), xs.reshape(n_dec, t_new, D_MODEL),
            kv_p(kp_all), kv_p(vp_all), kv_s(ks_all), kv_s(vs_all),
            st("rp"), st("rs"), st("wp"), st("ws"), st("sp"), st("ss"))
```

```python
import functools
import math

import jax
import jax.numpy as jnp
from jax import lax
from jax.experimental import pallas as pl
from jax.experimental.pallas import tpu as pltpu

F32 = jnp.float32
BF16 = jnp.bfloat16

D_MODEL = 1024
N_HEADS = 4
D_HEAD_A = 64
W_A = N_HEADS * 2 * D_HEAD_A
D_HEAD = 64
W_BC = N_HEADS * D_HEAD
PROJ_A = 3 * W_A
PROJ_B = 4 * W_BC
PROJ_C = 1024
N_BUCKETS = 32
MAX_DISTANCE = 128
NEAR_DIST = 113
RWKV_GN_EPS = 64e-5
LN_EPS = 1e-5
RMS_EPS = 1e-5
NEG = -1e30
LOG2E = 1.4426950408889634

VMEM_LIMIT_BYTES = 56 * 1024 * 1024
LANES = 128


def _cparams(*sem):
    return pltpu.CompilerParams(dimension_semantics=sem, vmem_limit_bytes=VMEM_LIMIT_BYTES)


def _dot(a, b):
    return jnp.dot(a, b, preferred_element_type=F32)


def _dot_nt(a, b):
    return lax.dot_general(a, b, (((1,), (1,)), ((), ())), preferred_element_type=F32)


def _dot_tn(a, b):
    return lax.dot_general(a, b, (((0,), (0,)), ((), ())), preferred_element_type=F32)


def _split2(x):
    hi = x.astype(BF16)
    lo = (x - hi.astype(F32)).astype(BF16)
    return hi, lo


def _split3(x):
    hi = x.astype(BF16)
    r1 = x - hi.astype(F32)
    mid = r1.astype(BF16)
    lo = (r1 - mid.astype(F32)).astype(BF16)
    return hi, mid, lo


def _softplus(z):
    return jnp.maximum(z, 0.0) + jnp.log1p(jnp.exp(-jnp.abs(z)))


def _layer_norm(z, g, b):
    mu = jnp.mean(z, axis=-1, keepdims=True)
    d = z - mu
    var = jnp.mean(d * d, axis=-1, keepdims=True)
    return d * lax.rsqrt(var + LN_EPS) * g + b


def _head_masks():
    lane = lax.broadcasted_iota(jnp.int32, (1, W_BC), 1)
    return [(lane >= h * D_HEAD) & (lane < (h + 1) * D_HEAD) for h in range(N_HEADS)]


def _stack_heads(x, masks):
    zero = jnp.zeros((), x.dtype)
    return jnp.concatenate([jnp.where(m, x, zero) for m in masks], axis=0)


def _unstack_heads(z, c):
    return z[0:c] + z[c:2 * c] + z[2 * c:3 * c] + z[3 * c:4 * c]


def _load_block_diag(s_ref):
    rows = []
    for h in range(N_HEADS):
        pieces = [jnp.zeros((D_HEAD, D_HEAD), F32)] * N_HEADS
        pieces[h] = s_ref[h].astype(F32)
        rows.append(jnp.concatenate(pieces, axis=1))
    return jnp.concatenate(rows, axis=0)


def _store_block_diag(s_ref, state):
    for h in range(N_HEADS):
        s_ref[h] = state[h * D_HEAD:(h + 1) * D_HEAD, h * D_HEAD:(h + 1) * D_HEAD]


def _head_ones():
    r = lax.broadcasted_iota(jnp.int32, (W_BC, W_BC), 0)
    c = lax.broadcasted_iota(jnp.int32, (W_BC, W_BC), 1)
    return jnp.where((r // D_HEAD) == (c // D_HEAD), 1.0, 0.0).astype(BF16)


def _head_sum(x, ones_bd):
    return _dot(x.astype(BF16), ones_bd)


def _proj_in_kernel(x_ref, w_ref, k_in_ref, v_in_ref, qa_ref, k_ref, v_ref, kb_ref, vb_ref, pb_ref, pc_ref):
    del k_in_ref, v_in_ref
    xb = x_ref[...].astype(BF16)

    def mm(lo, hi):
        return _dot(xb, w_ref[:, lo:hi])

    qa_ref[...] = (mm(0, W_A) * (D_HEAD_A ** -0.5 * LOG2E)).astype(BF16)
    tm = xb.shape[0]
    dv = 2 * D_HEAD_A
    for ref, bref, lo in ((k_ref, kb_ref, W_A), (v_ref, vb_ref, 2 * W_A)):
        val = mm(lo, lo + W_A)
        bref[...] = val.astype(BF16)
        for h in range(N_HEADS):
            ref[pl.ds(h, tm, stride=N_HEADS), :] = val[:, h * dv:(h + 1) * dv]
    pb_ref[...] = mm(PROJ_A, PROJ_A + PROJ_B)
    pc_ref[...] = mm(PROJ_A + PROJ_B, PROJ_A + PROJ_B + PROJ_C)


def _proj_in(x2d, w_bf, k_all, v_all, layer, depth, tm):
    m = x2d.shape[0]
    n = w_bf.shape[1]
    row = lambda i: (i, 0)
    stack_spec = pl.BlockSpec((None, tm * N_HEADS, 2 * D_HEAD_A), lambda i: (layer, i, 0))
    stack_shape = jax.ShapeDtypeStruct((depth, m * N_HEADS, 2 * D_HEAD_A), F32)
    return pl.pallas_call(
        _proj_in_kernel,
        grid=(m // tm,),
        in_specs=[pl.BlockSpec((tm, D_MODEL), row), pl.BlockSpec((D_MODEL, n), lambda i: (0, 0)),
                  pl.BlockSpec(memory_space=pl.ANY), pl.BlockSpec(memory_space=pl.ANY)],
        out_specs=[pl.BlockSpec((tm, W_A), row), stack_spec, stack_spec,
                   pl.BlockSpec((tm, W_A), row), pl.BlockSpec((tm, W_A), row),
                   pl.BlockSpec((tm, PROJ_B), row), pl.BlockSpec((tm, PROJ_C), row)],
        out_shape=[jax.ShapeDtypeStruct((m, W_A), BF16), stack_shape, stack_shape,
                   jax.ShapeDtypeStruct((m, W_A), BF16),
                   jax.ShapeDtypeStruct((m, W_A), BF16), jax.ShapeDtypeStruct((m, PROJ_B), F32),
                   jax.ShapeDtypeStruct((m, PROJ_C), F32)],
        input_output_aliases={2: 1, 3: 2},
        compiler_params=_cparams("parallel"),
        name="proj_in",
    )(x2d, w_bf, k_all, v_all)


def _lambda_full(lamv_ref, laminit_ref):
    lv = lamv_ref[...]
    s1 = jnp.sum(lv[0:1] * lv[1:2], axis=-1, keepdims=True)
    s2 = jnp.sum(lv[2:3] * lv[3:4], axis=-1, keepdims=True)
    return jnp.exp(s1) - jnp.exp(s2) + laminit_ref[...]


def _sub_rms(o, g, scale):
    y = o * lax.rsqrt(jnp.mean(o * o, axis=-1, keepdims=True) + RMS_EPS)
    return y * g * scale


def _rel_bucket(n):
    max_exact = N_BUCKETS // 2
    large = max_exact + (jnp.log(jnp.maximum(n, 1).astype(F32) / max_exact)
                         / math.log(MAX_DISTANCE / max_exact) * (N_BUCKETS - max_exact)).astype(jnp.int32)
    large = jnp.minimum(large, N_BUCKETS - 1)
    return jnp.where(n < max_exact, n, large)


V_AUG = 2 * D_HEAD_A + 16
PAIR = 2
HPS = 4


def _attn_prompt_kernel(lamv_ref, laminit_ref, q_ref, k_ref, v_ref, near_ref, g_ref, o_ref, *scratch, blk, seq):
    i = pl.program_id(2)
    dv = 2 * D_HEAD_A
    kp = PAIR * blk
    per_head = 1 + 2 + 2 + 4 + 4
    heads = []
    for h in range(HPS):
        sc = scratch[h * per_head:(h + 1) * per_head]
        heads.append(dict(vt=sc[0], m=sc[1:3], acc=sc[3:5], s=(sc[5:7], sc[7:9]), cm=(sc[9:11], sc[11:13])))
    lane = lax.broadcasted_iota(jnp.int32, (1, dv), 1)
    lo = lane < D_HEAD_A
    zero = jnp.zeros((), BF16)

    @pl.when(i == 0)
    def _():
        for h, hd in enumerate(heads):
            for c in range(seq // blk):
                vt = v_ref[c * blk:(c + 1) * blk, h * dv:(h + 1) * dv].astype(F32).T
                hd["vt"][0:dv, c * blk:(c + 1) * blk] = vt.astype(BF16)
            hd["vt"][dv:V_AUG, :] = jnp.ones((V_AUG - dv, seq), BF16)

    for h, hd in enumerate(heads):
        q = q_ref[:, h * dv:(h + 1) * dv]
        hd["q"] = (jnp.where(lo, q, zero), jnp.where(lo, zero, q))
        for mp in range(2):
            hd["m"][mp][...] = jnp.full(hd["m"][mp].shape, NEG, F32)
            hd["acc"][mp][...] = jnp.zeros(hd["acc"][mp].shape, F32)

    def stage_a(pair, slot, near_idx):
        off = pl.multiple_of(pair * kp, kp)
        for h, hd in enumerate(heads):
            cmax = [None, None]
            for sub in range(PAIR):
                kblk = k_ref[pl.ds(off + sub * blk, blk), h * dv:(h + 1) * dv]
                for mp in range(2):
                    s = _dot_nt(kblk, hd["q"][mp])
                    if near_idx is not None:
                        s = s + near_ref[h, near_idx, sub * blk:(sub + 1) * blk, :]
                    hd["s"][slot][mp][sub] = s
                    smax = jnp.max(s, axis=0, keepdims=True)
                    cmax[mp] = smax if cmax[mp] is None else jnp.maximum(cmax[mp], smax)
            for mp in range(2):
                hd["cm"][slot][mp][...] = cmax[mp]

    def stage_b(pair, slot):
        off = pl.multiple_of(pair * kp, kp)
        for hd in heads:
            vtb = hd["vt"][:, pl.ds(off, kp)]
            for mp in range(2):
                m_old = hd["m"][mp][...]
                mn = jnp.maximum(m_old, hd["cm"][slot][mp][...])
                p = jnp.concatenate([jnp.exp2(hd["s"][slot][mp][sub] - mn).astype(BF16) for sub in range(PAIR)],
                                    axis=0)
                hd["m"][mp][...] = mn
                hd["acc"][mp][...] = jnp.exp2(m_old - mn) * hd["acc"][mp][...] + _dot(vtb, p)

    n_pairs = (i + 2) // 2
    t1 = n_pairs - 1
    t0 = jnp.maximum(n_pairs - 2, 0)
    n_far = t0
    stage_a(t1, 0, 1)
    stage_a(t0, 1, 0)
    stage_b(t1, 0)

    def far_body(j, carry):
        stage_a(2 * j, 0, None)
        stage_b(jnp.where(j == 0, t0, 2 * j - 1), 1)
        stage_a(2 * j + 1, 1, None)
        stage_b(2 * j, 0)
        return carry

    lax.fori_loop(0, n_far // 2, far_body, 0)
    pending = jnp.where(n_far < 2, t0, ((n_far // 2) * 2) - 1)

    @pl.when(n_far % 2 == 1)
    def _():
        stage_a(n_far - 1, 0, None)
        stage_b(pending, 1)
        stage_b(n_far - 1, 0)

    @pl.when(n_far % 2 == 0)
    def _():
        stage_b(pending, 1)

    lam = _lambda_full(lamv_ref, laminit_ref)
    for h, hd in enumerate(heads):
        a1, a2 = hd["acc"][0][...], hd["acc"][1][...]
        o_t = a1[0:dv] / a1[dv:dv + 1] - lam * (a2[0:dv] / a2[dv:dv + 1])
        o_ref[:, h * dv:(h + 1) * dv] = _sub_rms(o_t.T, g_ref[...], 1.0 - laminit_ref[...]).astype(BF16)


def _near_bias_tiles(rel_bias, blk):
    cidx = jnp.arange(blk, dtype=jnp.int32)[:, None]
    ridx = jnp.arange(blk, dtype=jnp.int32)[None, :]
    rel = (rel_bias - rel_bias[N_BUCKETS - 1][None, :]) * LOG2E

    def lookup(dist):
        onehot = (_rel_bucket(dist)[..., None] == jnp.arange(N_BUCKETS, dtype=jnp.int32)).astype(F32)
        return jnp.einsum("crb,bh->hcr", onehot, rel, precision=lax.Precision.HIGHEST)

    dist0 = ridx - cidx
    diag = jnp.where((dist0 >= 0)[None], lookup(jnp.maximum(dist0, 0)), NEG)
    prev = lookup(dist0 + blk)
    masked = jnp.full_like(diag, NEG)
    zeros = jnp.zeros_like(diag)
    cat = lambda a, b: jnp.concatenate([a, b], axis=1)
    variants = [(cat(masked, masked), cat(diag, masked)), (cat(masked, masked), cat(prev, diag)),
                (cat(zeros, prev), cat(diag, masked)), (cat(zeros, zeros), cat(prev, diag))]
    return jnp.stack([jnp.stack(v, axis=1) for v in variants], axis=1)


def _attn_prompt(qa, kb, vb, lamv, laminit, near_bias, subln_g, batch, seq, blk):
    m = batch * seq
    nq = seq // blk
    dv = 2 * D_HEAD_A
    assert blk >= NEAR_DIST - 1 and nq % PAIR == 0 and N_HEADS % HPS == 0
    qspec = pl.BlockSpec((blk, HPS * dv), lambda b, h, i: (b * nq + i, h))
    kvspec = pl.BlockSpec((seq, HPS * dv), lambda b, h, i: (b, h))
    near = pl.BlockSpec((HPS, None, 2, PAIR * blk, blk),
                        lambda b, h, i: (h, jnp.where(i < 2, i, 2 + (i & 1)), 0, 0, 0))
    const2 = lambda shape: pl.BlockSpec(shape, lambda b, h, i: (0, 0))
    per_head = ([pltpu.VMEM((V_AUG, seq), BF16)] + [pltpu.VMEM((1, blk), F32)] * 2
                + [pltpu.VMEM((V_AUG, blk), F32)] * 2 + [pltpu.VMEM((PAIR, blk, blk), F32)] * 4
                + [pltpu.VMEM((1, blk), F32)] * 4)
    return pl.pallas_call(
        functools.partial(_attn_prompt_kernel, blk=blk, seq=seq),
        grid=(batch, N_HEADS // HPS, nq),
        in_specs=[const2((4, D_HEAD_A)), const2((1, 1)),
                  qspec, kvspec, kvspec, near, const2((1, dv))],
        out_specs=qspec,
        out_shape=jax.ShapeDtypeStruct((m, W_A), BF16),
        scratch_shapes=per_head * HPS,
        compiler_params=_cparams("parallel", "parallel", "arbitrary"),
        name="attn_prompt",
    )(lamv, laminit, qa, kb, vb, near_bias, subln_g.reshape(1, -1))


def _attn_sample_kernel(pt_ref, lamv_ref, laminit_ref, q_ref, kn_ref, vn_ref, bfar_ref, blast_ref, bnew_ref, g_ref,
                        *refs, group, n_steps, rows):
    k_refs = refs[:group]
    v_refs = refs[group:2 * group]
    o_ref = refs[2 * group]
    m_sc, l_sc, acc_sc = refs[2 * group + 1:]
    s_id = pl.program_id(1)
    cols = k_refs[0].shape[0]

    q = q_ref[...]
    lane = lax.broadcasted_iota(jnp.int32, (1, 2 * D_HEAD_A), 1)
    lo = lane < D_HEAD_A
    zero = jnp.zeros((), BF16)
    wq = jnp.concatenate([jnp.where(lo, q, zero), jnp.where(lo, zero, q)], axis=0)

    def update(s, pv_of):
        m = m_sc[...]
        mn = jnp.maximum(m, jnp.max(s, axis=-1, keepdims=True))
        al = jnp.exp2(m - mn)
        p = jnp.exp2(s - mn)
        m_sc[...] = mn
        l_sc[...] = al * l_sc[...] + jnp.sum(p, axis=-1, keepdims=True)
        acc_sc[...] = al * acc_sc[...] + pv_of(p.astype(BF16))

    @pl.when(s_id == 0)
    def _():
        m_sc[...] = jnp.full(m_sc.shape, NEG, F32)
        l_sc[...] = jnp.zeros(l_sc.shape, F32)
        acc_sc[...] = jnp.zeros(acc_sc.shape, F32)
        update(_dot_nt(wq, kn_ref[...]) + bnew_ref[...], lambda p: _dot(p, vn_ref[...]))

    bfar = bfar_ref[...]
    parts = []
    for g in range(group):
        bias = bfar
        if g == group - 1:
            bias = jnp.where(s_id == n_steps - 1, blast_ref[...], bfar)
        parts.append(_dot_nt(wq, k_refs[g][...].astype(BF16)) + bias)

    def pv_pages(p):
        return sum(_dot(p[:, g * cols:(g + 1) * cols], v_refs[g][...].astype(BF16)) for g in range(group))

    update(jnp.concatenate(parts, axis=1), pv_pages)

    @pl.when(s_id == n_steps - 1)
    def _():
        lam = _lambda_full(lamv_ref, laminit_ref)
        o = acc_sc[...] / l_sc[...]
        od = o[0:rows] - lam * o[rows:2 * rows]
        o_ref[...] = _sub_rms(od, g_ref[...], 1.0 - laminit_ref[...]).astype(BF16)


def _sample_bias_tiles(rel_bias, t_new, n_pages, page):
    rows = t_new * N_HEADS
    cols = page * N_HEADS
    past = n_pages * page
    assert page + 1 >= NEAR_DIST
    rel = (rel_bias - rel_bias[N_BUCKETS - 1][None, :]) * LOG2E
    r = jnp.arange(2 * rows, dtype=jnp.int32)[:, None]
    rt, rh = (r % rows) // N_HEADS, r % N_HEADS
    rel_rows = rel[:, rh[:, 0]]

    def lookup(dist):
        onehot = (_rel_bucket(dist)[..., None] == jnp.arange(N_BUCKETS, dtype=jnp.int32)).astype(F32)
        return jnp.einsum("rcb,br->rc", onehot, rel_rows, precision=lax.Precision.HIGHEST)

    c = jnp.arange(cols, dtype=jnp.int32)[None, :]
    ct, ch = c // N_HEADS, c % N_HEADS
    same = rh == ch
    bfar = jnp.where(same, 0.0, NEG).astype(F32)
    dist_last = (past + rt) - ((n_pages - 1) * page + ct)
    blast = jnp.where(same, lookup(dist_last), NEG).astype(F32)
    cn = jnp.arange(rows, dtype=jnp.int32)[None, :]
    cnt, cnh = cn // N_HEADS, cn % N_HEADS
    dist_new = rt - cnt
    bnew = jnp.where((rh == cnh) & (dist_new >= 0), lookup(jnp.maximum(dist_new, 0)), NEG).astype(F32)
    return bfar, blast, bnew


def _attn_sample(qa, kb, vb, cache_k4, cache_v4, pt_flat, layer, lamv, laminit, bias_tiles, subln_g,
                 n_dec, t_new, n_pages, page, group):
    rows = t_new * N_HEADS
    cols = page * N_HEADS
    n_steps = n_pages // group
    bfar, blast, bnew = bias_tiles
    q3 = qa.reshape(n_dec, rows, 2 * D_HEAD_A)
    kn3 = kb.reshape(n_dec, rows, 2 * D_HEAD_A)
    vn3 = vb.reshape(n_dec, rows, 2 * D_HEAD_A)
    per_b = pl.BlockSpec((None, rows, 2 * D_HEAD_A), lambda b, s, pt: (b, 0, 0))
    const = lambda shape: pl.BlockSpec(shape, lambda b, s, pt: (0, 0))

    def page_spec(g):
        return pl.BlockSpec((None, None, cols, 2 * D_HEAD_A),
                            lambda b, s, pt: (layer, pt[b * n_pages + s * group + g], 0, 0))

    in_specs = [const((4, D_HEAD_A)), const((1, 1)), per_b, per_b, per_b,
                const((2 * rows, cols)), const((2 * rows, cols)), const((2 * rows, rows)),
                const((1, 2 * D_HEAD_A))]
    in_specs += [page_spec(g) for g in range(group)] * 2
    out = pl.pallas_call(
        functools.partial(_attn_sample_kernel, group=group, n_steps=n_steps, rows=rows),
        grid_spec=pltpu.PrefetchScalarGridSpec(
            num_scalar_prefetch=1,
            grid=(n_dec, n_steps),
            in_specs=in_specs,
            out_specs=per_b,
            scratch_shapes=[pltpu.VMEM((2 * rows, 1), F32), pltpu.VMEM((2 * rows, 1), F32),
                            pltpu.VMEM((2 * rows, 2 * D_HEAD_A), F32)],
        ),
        out_shape=jax.ShapeDtypeStruct((n_dec, rows, 2 * D_HEAD_A), BF16),
        compiler_params=_cparams("parallel", "arbitrary"),
        name="attn_sample",
    )(pt_flat, lamv, laminit, q3, kn3, vn3, bfar, blast, bnew, subln_g.reshape(1, -1),
      *([cache_k4] * group), *([cache_v4] * group))
    return out.reshape(n_dec * t_new, W_A)


def _ret_kernel(pb_ref, cos_ref, sin_ref, s0_ref, o_ref, sfin_ref, st_sc, *, chunk, n_sub, n_valid, n_steps):
    c_id = pl.program_id(1)
    lg = chunk.bit_length() - 1
    rows = N_HEADS * chunk
    tb = n_sub * chunk
    masks = _head_masks()
    ones_bd = _head_ones()

    @pl.when(c_id == 0)
    def _():
        st_sc[...] = _load_block_diag(s0_ref)

    pb = pb_ref[...]
    q, k, v, g = pb[:, 0:W_BC], pb[:, W_BC:2 * W_BC], pb[:, 2 * W_BC:3 * W_BC], pb[:, 3 * W_BC:4 * W_BC]
    lane = lax.broadcasted_iota(jnp.int32, (1, W_BC), 1)
    first_half = (lane & (D_HEAD - 1)) < (D_HEAD // 2)
    cosf, sins = cos_ref[...], sin_ref[...]

    def rot(x):
        swapped = jnp.where(first_half, pltpu.roll(x, W_BC - D_HEAD // 2, 1), pltpu.roll(x, D_HEAD // 2, 1))
        return x * cosf + swapped * sins

    q = rot(q)
    k = rot(k) * (D_HEAD ** -0.5)
    row = lax.broadcasted_iota(jnp.int32, (tb, 1), 0)
    if n_valid < chunk:
        valid = row < n_valid
        k = jnp.where(valid, k, 0.0)
        v = jnp.where(valid, v, 0.0)

    log_g = [math.log1p(-(2.0 ** (-5 - h))) for h in range(N_HEADS)]
    lg_lane = sum(jnp.where(m, log_g[h], 0.0) for h, m in enumerate(masks))
    ri = lax.broadcasted_iota(jnp.int32, (rows, rows), 0)
    ci = lax.broadcasted_iota(jnp.int32, (rows, rows), 1)
    rh = lax.broadcasted_iota(jnp.int32, (rows, 1), 0) >> lg
    rpos = (lax.broadcasted_iota(jnp.int32, (rows, 1), 0) & (chunk - 1)).astype(F32)
    lg_row = sum(jnp.where(rh == h, log_g[h], 0.0) for h in range(N_HEADS))
    diff = (ri & (chunk - 1)) - (ci & (chunk - 1))
    keep = ((ri >> lg) == (ci >> lg)) & (diff >= 0)
    dmask = jnp.where(keep, jnp.exp(lg_row * jnp.maximum(diff, 0).astype(F32)), 0.0)
    xi = jnp.exp(lg_row * (rpos + 1.0))
    posf = (row & (chunk - 1)).astype(F32)
    zeta = jnp.exp(lg_lane * jnp.maximum(float(n_valid - 1) - posf, 0.0))
    g_chunk = jnp.exp(lg_lane * float(n_valid))
    qb, kb, vb, kzb = q.astype(BF16), k.astype(BF16), v.astype(BF16), (k * zeta).astype(BF16)

    def prepare(c):
        sl = slice(c * chunk, (c + 1) * chunk)
        q_st, k_st, v_st = (_stack_heads(x[sl], masks) for x in (qb, kb, vb))
        s_blk = _dot_nt(q_st, k_st) * dmask
        return q_st, _dot(s_blk.astype(BF16), v_st), _dot_tn(_stack_heads(kzb[sl], masks), v_st)

    prepared = [prepare(c) for c in range(n_sub)]
    state = st_sc[...]
    o_chunks = []
    for q_st, inner, incr in prepared:
        cross = _dot(q_st, state.astype(BF16)) * xi
        o_chunks.append(_unstack_heads(inner + cross, chunk))
        state = state * g_chunk + incr
    st_sc[...] = state

    o = o_chunks[0] if n_sub == 1 else jnp.concatenate(o_chunks, axis=0)
    ms = _head_sum(o * o, ones_bd) * (1.0 / D_HEAD)
    o_ref[...] = (jax.nn.silu(g) * (o * lax.rsqrt(ms + RMS_EPS))).astype(BF16)

    @pl.when(c_id == n_steps - 1)
    def _():
        _store_block_diag(sfin_ref, state)


def _retention(pb, cos_t, sin_t, s0, batch, t_pad, chunk, n_sub, n_valid):
    tb = chunk * n_sub
    n_steps = t_pad // tb
    rowspec = lambda w: pl.BlockSpec((tb, w), lambda b, c: (b * n_steps + c, 0))
    stspec = pl.BlockSpec((None, N_HEADS, D_HEAD, D_HEAD), lambda b, c: (b, 0, 0, 0))
    return pl.pallas_call(
        functools.partial(_ret_kernel, chunk=chunk, n_sub=n_sub, n_valid=n_valid, n_steps=n_steps),
        grid=(batch, n_steps),
        in_specs=[rowspec(PROJ_B), pl.BlockSpec((tb, W_BC), lambda b, c: (c, 0)),
                  pl.BlockSpec((tb, W_BC), lambda b, c: (c, 0)), stspec],
        out_specs=[rowspec(W_BC), stspec],
        out_shape=[jax.ShapeDtypeStruct((batch * t_pad, W_BC), BF16),
                   jax.ShapeDtypeStruct((batch, N_HEADS, D_HEAD, D_HEAD), F32)],
        scratch_shapes=[pltpu.VMEM((W_BC, W_BC), F32)],
        compiler_params=_cparams("parallel", "arbitrary"),
        name="retention",
    )(pb, cos_t, sin_t, s0)


def _rwkv_kernel(pc_ref, shift_ref, s0_ref, mu_ref, w0_ref, w2_ref, a0_ref, a2_ref, g2_ref, kk_ref, ka_ref,
                 rk_ref, lng_ref, lnb_ref, o_ref, sfin_ref, st_sc, prev_sc, *, chunk, n_sub, n_valid, n_steps):
    c_id = pl.program_id(1)
    lg = chunk.bit_length() - 1
    rows = N_HEADS * chunk
    tb = n_sub * chunk
    masks = _head_masks()
    ones_bd = _head_ones()

    @pl.when(c_id == 0)
    def _():
        st_sc[...] = _load_block_diag(s0_ref)
        prev_sc[...] = shift_ref[...]

    pc = pc_ref[...]
    row = lax.broadcasted_iota(jnp.int32, (tb, 1), 0)
    prev = jnp.where(row == 0, prev_sc[...], pltpu.roll(pc, 1, 0))
    prev_sc[...] = pc[tb - 1:tb, :]
    xm = pc + (prev - pc) * mu_ref[...]
    r, k, v, xt = xm[:, 0:W_BC], xm[:, W_BC:2 * W_BC], xm[:, 2 * W_BC:3 * W_BC], xm[:, 3 * W_BC:4 * W_BC]

    w_log = -_softplus(-(w0_ref[...] + _dot(jnp.tanh(xt).astype(BF16), w2_ref[...]))) - 0.5
    logw = -jnp.exp(w_log)
    a = jax.nn.sigmoid(a0_ref[...] + _dot(xt.astype(BF16), a2_ref[...]))
    g = _dot(jax.nn.sigmoid(xt).astype(BF16), g2_ref[...])

    kk = k * kk_ref[...]
    kk = kk / jnp.maximum(jnp.sqrt(_head_sum(kk * kk, ones_bd)), 1e-12)
    k2 = k * (1.0 + (a - 1.0) * ka_ref[...])
    if n_valid < chunk:
        valid = row < n_valid
        logw = jnp.where(valid, logw, 0.0)
        kk = jnp.where(valid, kk, 0.0)
        k2 = jnp.where(valid, k2, 0.0)
        v = jnp.where(valid, v, 0.0)

    ti = lax.broadcasted_iota(jnp.int32, (tb, tb), 0)
    tj = lax.broadcasted_iota(jnp.int32, (tb, tb), 1)
    same_chunk = (ti >> lg) == (tj >> lg)
    tri = jnp.where(same_chunk & (ti >= tj), 1.0, 0.0).astype(BF16)
    cum = sum(_dot(tri, part) for part in _split3(logw))
    ends = [jnp.broadcast_to(cum[(c + 1) * chunk - 1:(c + 1) * chunk], (chunk, W_BC)) for c in range(n_sub)]
    cum_end = ends[0] if n_sub == 1 else jnp.concatenate(ends, axis=0)
    e_bwd = jnp.exp(-cum)
    e_end = jnp.exp(cum_end - cum)
    kka = kk * a
    a_t = -kk * jnp.exp(cum - logw)
    b_t = kka * e_bwd
    k_t = k2 * e_bwd
    r_t = r * jnp.exp(cum)
    b_h = kka * e_end
    k_h = k2 * e_end
    p_end = jnp.exp(cum_end)

    ri = lax.broadcasted_iota(jnp.int32, (rows, rows), 0)
    ci = lax.broadcasted_iota(jnp.int32, (rows, rows), 1)
    same = (ri >> lg) == (ci >> lg)
    dpos = (ri & (chunk - 1)) - (ci & (chunk - 1))
    strict = same & (dpos > 0)
    incl = same & (dpos >= 0)
    eye = jnp.where(ri == ci, 1.0, 0.0)

    def prepare(c):
        sl = slice(c * chunk, (c + 1) * chunk)
        st = lambda x: _stack_heads(x[sl].astype(BF16), masks)
        a_st, b_st, k_st, r_st, v_st, bh_st, kh_st = st(a_t), st(b_t), st(k_t), st(r_t), st(v), st(b_h), st(k_h)
        n_bd = jnp.where(strict, _dot_nt(a_st, b_st), 0.0)
        ak_bd = jnp.where(strict, _dot_nt(a_st, k_st), 0.0)
        m_rb = jnp.where(incl, _dot_nt(r_st, b_st), 0.0).astype(BF16)
        m_rk = jnp.where(incl, _dot_nt(r_st, k_st), 0.0).astype(BF16)
        t_inv = eye + n_bd
        pw = n_bd
        for _ in range(lg - 1):
            pwb = pw.astype(BF16)
            pw = _dot(pwb, pwb)
            t_inv = t_inv + _dot(t_inv.astype(BF16), pw.astype(BF16))
        tbf = t_inv.astype(BF16)
        w_st = _dot(tbf, a_st).astype(BF16)
        y_st = _dot(tbf, _dot(ak_bd.astype(BF16), v_st).astype(BF16))
        o_pre = _dot(m_rk, v_st)
        g_pre = _dot_tn(v_st, kh_st)
        return w_st, y_st, r_st, m_rb, o_pre, bh_st, g_pre, p_end[c * chunk:c * chunk + 1]

    prepared = [prepare(c) for c in range(n_sub)]

    state = st_sc[...]
    o_chunks = []
    for w_st, y_st, r_st, m_rb, o_pre, bh_st, g_pre, p_c in prepared:
        sb = state.astype(BF16)
        u = _dot_nt(w_st, sb) + y_st
        ub = u.astype(BF16)
        o_chunks.append(_unstack_heads(_dot_nt(r_st, sb) + _dot(m_rb, ub) + o_pre, chunk))
        state = state * p_c + _dot_tn(ub, bh_st) + g_pre
    st_sc[...] = state

    o = o_chunks[0] if n_sub == 1 else jnp.concatenate(o_chunks, axis=0)
    inv = 1.0 / D_HEAD
    mean = _head_sum(o, ones_bd) * inv
    d = o - mean
    var = _head_sum(d * d, ones_bd) * inv
    o_n = d * lax.rsqrt(var + RWKV_GN_EPS) * lng_ref[...] + lnb_ref[...]
    bonus = _head_sum(r * k2 * rk_ref[...], ones_bd) * v
    o_ref[...] = ((o_n + bonus) * g).astype(BF16)

    @pl.when(c_id == n_steps - 1)
    def _():
        _store_block_diag(sfin_ref, state)


def _rwkv(pc, shift0, s0, params, batch, t_pad, chunk, n_sub, n_valid):
    tb = chunk * n_sub
    n_steps = t_pad // tb
    rowspec = lambda w: pl.BlockSpec((tb, w), lambda b, c: (b * n_steps + c, 0))
    stspec = pl.BlockSpec((None, N_HEADS, D_HEAD, D_HEAD), lambda b, c: (b, 0, 0, 0))
    const = lambda arr: pl.BlockSpec(arr.shape, lambda b, c: (0,) * arr.ndim)
    return pl.pallas_call(
        functools.partial(_rwkv_kernel, chunk=chunk, n_sub=n_sub, n_valid=n_valid, n_steps=n_steps),
        grid=(batch, n_steps),
        in_specs=[rowspec(PROJ_C), pl.BlockSpec((None, 1, PROJ_C), lambda b, c: (b, 0, 0)), stspec]
                 + [const(p) for p in params],
        out_specs=[rowspec(W_BC), stspec],
        out_shape=[jax.ShapeDtypeStruct((batch * t_pad, W_BC), BF16),
                   jax.ShapeDtypeStruct((batch, N_HEADS, D_HEAD, D_HEAD), F32)],
        scratch_shapes=[pltpu.VMEM((W_BC, W_BC), F32), pltpu.VMEM((1, PROJ_C), F32)],
        compiler_params=_cparams("parallel", "arbitrary"),
        name="rwkv7",
    )(pc, shift0, s0, *params)


def _out_proj_kernel(x_ref, a_ref, b_ref, c_ref, w_ref, g_ref, bb_ref, y_ref, *, alpha):
    acc = _dot(a_ref[...], w_ref[0:W_A, :])
    acc += _dot(b_ref[...], w_ref[W_A:W_A + W_BC, :])
    acc += _dot(c_ref[...], w_ref[W_A + W_BC:W_A + 2 * W_BC, :])
    y_ref[...] = _layer_norm(alpha * x_ref[...] + acc, g_ref[...], bb_ref[...])


def _out_proj(x2d, oa, ob, oc, w_bf, g, b, alpha, tm):
    m = x2d.shape[0]
    row = lambda w: pl.BlockSpec((tm, w), lambda i: (i, 0))
    const = lambda shape: pl.BlockSpec(shape, lambda i: (0, 0))
    return pl.pallas_call(
        functools.partial(_out_proj_kernel, alpha=alpha),
        grid=(m // tm,),
        in_specs=[row(D_MODEL), row(W_A), row(W_BC), row(W_BC), const((D_MODEL, D_MODEL)),
                  const((1, D_MODEL)), const((1, D_MODEL))],
        out_specs=row(D_MODEL),
        out_shape=jax.ShapeDtypeStruct((m, D_MODEL), F32),
        compiler_params=_cparams("parallel"),
        name="out_proj",
    )(x2d, oa, ob, oc, w_bf, g.reshape(1, -1), b.reshape(1, -1))


def _ffn_kernel(x_ref, w1_ref, w3_ref, w2_ref, g_ref, b_ref, y_ref, acc_sc, *, alpha):
    f = pl.program_id(1)
    x = x_ref[...]
    xb = x.astype(BF16)

    @pl.when(f == 0)
    def _():
        acc_sc[...] = jnp.zeros(acc_sc.shape, F32)

    h = jax.nn.silu(_dot(xb, w1_ref[...])) * _dot(xb, w3_ref[...])
    acc_sc[...] += _dot(h.astype(BF16), w2_ref[...])

    @pl.when(f == pl.num_programs(1) - 1)
    def _():
        y_ref[...] = _layer_norm(alpha * x + acc_sc[...], g_ref[...], b_ref[...])


def _ffn(x2d, w1, w3, w2, g, b, alpha, tm, tf):
    m = x2d.shape[0]
    d_ff = w1.shape[1]
    return pl.pallas_call(
        functools.partial(_ffn_kernel, alpha=alpha),
        grid=(m // tm, d_ff // tf),
        in_specs=[pl.BlockSpec((tm, D_MODEL), lambda i, f: (i, 0)),
                  pl.BlockSpec((D_MODEL, tf), lambda i, f: (0, f)),
                  pl.BlockSpec((D_MODEL, tf), lambda i, f: (0, f)),
                  pl.BlockSpec((tf, D_MODEL), lambda i, f: (f, 0)),
                  pl.BlockSpec((1, D_MODEL), lambda i, f: (0, 0)),
                  pl.BlockSpec((1, D_MODEL), lambda i, f: (0, 0))],
        out_specs=pl.BlockSpec((tm, D_MODEL), lambda i, f: (i, 0)),
        out_shape=jax.ShapeDtypeStruct((m, D_MODEL), F32),
        scratch_shapes=[pltpu.VMEM((tm, D_MODEL), F32)],
        compiler_params=_cparams("parallel", "arbitrary"),
        name="ffn",
    )(x2d, w1, w3, w2, g.reshape(1, -1), b.reshape(1, -1))


def _moe_kernel(x_ref, rh_ref, rl_ref, w1_ref, w3_ref, w2_ref, g_ref, b_ref, y_ref, acc_sc, gate_sc, xb_sc,
                *, alpha, n_experts):
    e = pl.program_id(1)
    lane = lax.broadcasted_iota(jnp.int32, (1, LANES), 1)

    @pl.when(e == 0)
    def _():
        x = x_ref[...]
        xh, xl = _split2(x)
        xb_sc[...] = xh
        logits = _dot(xh, rh_ref[...]) + _dot(xl, rh_ref[...]) + _dot(xh, rl_ref[...])
        logits = jnp.where(lane < n_experts, logits, NEG)
        v1 = jnp.max(logits, axis=-1, keepdims=True)
        i1 = jnp.min(jnp.where(logits == v1, lane, LANES), axis=-1, keepdims=True)
        rest = jnp.where(lane == i1, NEG, logits)
        v2 = jnp.max(rest, axis=-1, keepdims=True)
        i2 = jnp.min(jnp.where(rest == v2, lane, LANES), axis=-1, keepdims=True)
        ex = jnp.exp(v2 - v1)
        g1 = 1.0 / (1.0 + ex)
        g2 = ex / (1.0 + ex)
        gate_sc[...] = jnp.where(lane == i1, g1, 0.0) + jnp.where(lane == i2, g2, 0.0)
        acc_sc[...] = jnp.zeros(acc_sc.shape, F32)

    xb = xb_sc[...]
    h = jax.nn.silu(_dot(xb, w1_ref[...])) * _dot(xb, w3_ref[...])
    f = _dot(h.astype(BF16), w2_ref[...])
    ge = jnp.sum(jnp.where(lane == e, gate_sc[...], 0.0), axis=-1, keepdims=True)
    acc_sc[...] += ge * f

    @pl.when(e == n_experts - 1)
    def _():
        y_ref[...] = _layer_norm(alpha * x_ref[...] + acc_sc[...], g_ref[...], b_ref[...])


def _moe(x2d, router, w1, w3, w2, g, b, alpha, tm):
    m = x2d.shape[0]
    n_experts, _, d_e = w1.shape
    r_pad = jnp.zeros((D_MODEL, LANES), F32).at[:, :n_experts].set(router)
    r_hi = r_pad.astype(BF16)
    r_lo = (r_pad - r_hi.astype(F32)).astype(BF16)
    return pl.pallas_call(
        functools.partial(_moe_kernel, alpha=alpha, n_experts=n_experts),
        grid=(m // tm, n_experts),
        in_specs=[pl.BlockSpec((tm, D_MODEL), lambda i, e: (i, 0)),
                  pl.BlockSpec((D_MODEL, LANES), lambda i, e: (0, 0)),
                  pl.BlockSpec((D_MODEL, LANES), lambda i, e: (0, 0)),
                  pl.BlockSpec((None, D_MODEL, d_e), lambda i, e: (e, 0, 0)),
                  pl.BlockSpec((None, D_MODEL, d_e), lambda i, e: (e, 0, 0)),
                  pl.BlockSpec((None, d_e, D_MODEL), lambda i, e: (e, 0, 0)),
                  pl.BlockSpec((1, D_MODEL), lambda i, e: (0, 0)),
                  pl.BlockSpec((1, D_MODEL), lambda i, e: (0, 0))],
        out_specs=pl.BlockSpec((tm, D_MODEL), lambda i, e: (i, 0)),
        out_shape=jax.ShapeDtypeStruct((m, D_MODEL), F32),
        scratch_shapes=[pltpu.VMEM((tm, D_MODEL), F32), pltpu.VMEM((tm, LANES), F32),
                        pltpu.VMEM((tm, D_MODEL), BF16)],
        compiler_params=_cparams("parallel", "arbitrary"),
        name="moe",
    )(x2d, r_hi, r_lo, w1, w3, w2, g.reshape(1, -1), b.reshape(1, -1))


MOE_TILE = 512


def _moe_router_kernel(x_ref, rh_ref, rl_ref, info_ref, cnt_ref, carry_sc, *, n_experts):
    i = pl.program_id(0)
    tm = x_ref.shape[0]
    lane = lax.broadcasted_iota(jnp.int32, (1, LANES), 1)

    @pl.when(i == 0)
    def _():
        carry_sc[...] = jnp.zeros(carry_sc.shape, F32)

    xh, xl = _split2(x_ref[...])
    logits = _dot(xh, rh_ref[...]) + _dot(xl, rh_ref[...]) + _dot(xh, rl_ref[...])
    logits = jnp.where(lane < n_experts, logits, NEG)
    v1 = jnp.max(logits, axis=-1, keepdims=True)
    i1 = jnp.min(jnp.where(logits == v1, lane, LANES), axis=-1, keepdims=True)
    rest = jnp.where(lane == i1, NEG, logits)
    v2 = jnp.max(rest, axis=-1, keepdims=True)
    i2 = jnp.min(jnp.where(rest == v2, lane, LANES), axis=-1, keepdims=True)
    ex = jnp.exp(v2 - v1)
    g1 = 1.0 / (1.0 + ex)
    g2 = ex / (1.0 + ex)
    m1, m2 = lane == i1, lane == i2
    member = jnp.where(m1 | m2, 1.0, 0.0)
    r = lax.broadcasted_iota(jnp.int32, (tm, tm), 0)
    c = lax.broadcasted_iota(jnp.int32, (tm, tm), 1)
    before = jnp.where(r > c, 1.0, 0.0).astype(BF16)
    rank = _dot(before, member.astype(BF16)) + carry_sc[...]
    rank1 = jnp.sum(jnp.where(m1, rank, 0.0), axis=-1, keepdims=True)
    rank2 = jnp.sum(jnp.where(m2, rank, 0.0), axis=-1, keepdims=True)
    total = carry_sc[...] + jnp.sum(member, axis=0, keepdims=True)
    carry_sc[...] = total
    cnt_ref[...] = total
    fields = (i1.astype(F32), i2.astype(F32), g1, g2, rank1, rank2)
    info = jnp.zeros((tm, LANES), F32)
    for k, f in enumerate(fields):
        info = jnp.where(lane == k, f, info)
    info_ref[...] = info


def _moe_dispatch_kernel(pos1_ref, pos2_ref, x_ref, xs_in_ref, xs_ref, sem, *, tm):
    del xs_in_ref
    base = pl.program_id(0) * tm

    def row_copy(t, p):
        return pltpu.make_async_copy(x_ref.at[pl.ds(t, 1)], xs_ref.at[pl.ds(p, 1)], sem)

    def issue(t, carry):
        row_copy(t, pos1_ref[base + t]).start()
        row_copy(t, pos2_ref[base + t]).start()
        return carry

    def drain(t, carry):
        row_copy(0, 0).wait()
        row_copy(0, 0).wait()
        return carry

    lax.fori_loop(0, tm, issue, 0, unroll=8)
    lax.fori_loop(0, tm, drain, 0, unroll=8)


def _moe_expert_kernel(te_ref, nv_ref, x_ref, w1_ref, w3_ref, w2_ref, y_ref):
    j = pl.program_id(0)

    @pl.when(j < nv_ref[0])
    def _():
        xb = x_ref[...].astype(BF16)
        h = jax.nn.silu(_dot(xb, w1_ref[...])) * _dot(xb, w3_ref[...])
        y_ref[...] = _dot(h.astype(BF16), w2_ref[...])

    @pl.when(j >= nv_ref[0])
    def _():
        y_ref[...] = jnp.zeros(y_ref.shape, F32)


def _moe_combine_kernel(pos1_ref, pos2_ref, x_ref, info_ref, ys_ref, g_ref, b_ref, y_ref, buf1, buf2, sem,
                        *, tm, alpha):
    base = pl.program_id(0) * tm

    def row_copy(p, buf, t):
        return pltpu.make_async_copy(ys_ref.at[pl.ds(p, 1)], buf.at[pl.ds(t, 1)], sem)

    def issue(t, carry):
        row_copy(pos1_ref[base + t], buf1, t).start()
        row_copy(pos2_ref[base + t], buf2, t).start()
        return carry

    def drain(t, carry):
        row_copy(0, buf1, 0).wait()
        row_copy(0, buf2, 0).wait()
        return carry

    lax.fori_loop(0, tm, issue, 0, unroll=8)
    lax.fori_loop(0, tm, drain, 0, unroll=8)
    lane = lax.broadcasted_iota(jnp.int32, (1, LANES), 1)
    info = info_ref[...]
    g1 = jnp.sum(jnp.where(lane == 2, info, 0.0), axis=-1, keepdims=True)
    g2 = jnp.sum(jnp.where(lane == 3, info, 0.0), axis=-1, keepdims=True)
    f = g1 * buf1[...] + g2 * buf2[...]
    y_ref[...] = _layer_norm(alpha * x_ref[...] + f, g_ref[...], b_ref[...])


def _moe_routed(x2d, router, w1, w3, w2, g, b, alpha):
    m = x2d.shape[0]
    n_experts, _, d_e = w1.shape
    tile = MOE_TILE
    n_tiles = (2 * m) // tile + n_experts
    rows = n_tiles * tile
    r_pad = jnp.zeros((D_MODEL, LANES), F32).at[:, :n_experts].set(router)
    r_hi = r_pad.astype(BF16)
    r_lo = (r_pad - r_hi.astype(F32)).astype(BF16)
    row_spec = lambda w: pl.BlockSpec((tile, w), lambda i, *_: (i, 0))

    info, cnt = pl.pallas_call(
        functools.partial(_moe_router_kernel, n_experts=n_experts),
        grid=(m // tile,),
        in_specs=[row_spec(D_MODEL), pl.BlockSpec((D_MODEL, LANES), lambda i: (0, 0)),
                  pl.BlockSpec((D_MODEL, LANES), lambda i: (0, 0))],
        out_specs=[row_spec(LANES), pl.BlockSpec((1, LANES), lambda i: (0, 0))],
        out_shape=[jax.ShapeDtypeStruct((m, LANES), F32), jax.ShapeDtypeStruct((1, LANES), F32)],
        scratch_shapes=[pltpu.VMEM((1, LANES), F32)],
        compiler_params=_cparams("arbitrary"),
        name="moe_router",
    )(x2d, r_hi, r_lo)

    counts = cnt[0, :n_experts].astype(jnp.int32)
    tiles_e = (counts + tile - 1) // tile
    tile_end = jnp.cumsum(tiles_e)
    row_off = (tile_end - tiles_e) * tile
    eids = jnp.arange(n_experts, dtype=jnp.int32)

    def dest(col_e, col_r):
        e = info[:, col_e].astype(jnp.int32)
        off = jnp.sum(jnp.where(e[:, None] == eids[None, :], row_off[None, :], 0), axis=1)
        return off + info[:, col_r].astype(jnp.int32)

    pos1, pos2 = dest(0, 4), dest(1, 5)
    tile_ids = jnp.arange(n_tiles, dtype=jnp.int32)
    tile_expert = jnp.minimum(jnp.sum(tile_ids[:, None] >= tile_end[None, :], axis=1), n_experts - 1).astype(jnp.int32)
    n_valid = tile_end[n_experts - 1:].astype(jnp.int32)

    xs = pl.pallas_call(
        functools.partial(_moe_dispatch_kernel, tm=tile),
        grid_spec=pltpu.PrefetchScalarGridSpec(
            num_scalar_prefetch=2,
            grid=(m // tile,),
            in_specs=[row_spec(D_MODEL), pl.BlockSpec(memory_space=pl.ANY)],
            out_specs=pl.BlockSpec(memory_space=pl.ANY),
            scratch_shapes=[pltpu.SemaphoreType.DMA],
        ),
        out_shape=jax.ShapeDtypeStruct((rows, D_MODEL), F32),
        input_output_aliases={3: 0},
        compiler_params=_cparams("arbitrary"),
        name="moe_dispatch",
    )(pos1, pos2, x2d, jnp.zeros((rows, D_MODEL), F32))

    ys = pl.pallas_call(
        _moe_expert_kernel,
        grid_spec=pltpu.PrefetchScalarGridSpec(
            num_scalar_prefetch=2,
            grid=(n_tiles,),
            in_specs=[row_spec(D_MODEL),
                      pl.BlockSpec((None, D_MODEL, d_e), lambda j, te, nv: (te[j], 0, 0)),
                      pl.BlockSpec((None, D_MODEL, d_e), lambda j, te, nv: (te[j], 0, 0)),
                      pl.BlockSpec((None, d_e, D_MODEL), lambda j, te, nv: (te[j], 0, 0))],
            out_specs=row_spec(D_MODEL),
        ),
        out_shape=jax.ShapeDtypeStruct((rows, D_MODEL), F32),
        compiler_params=_cparams("arbitrary"),
        name="moe_experts",
    )(tile_expert, n_valid, xs, w1, w3, w2)

    return pl.pallas_call(
        functools.partial(_moe_combine_kernel, tm=tile, alpha=alpha),
        grid_spec=pltpu.PrefetchScalarGridSpec(
            num_scalar_prefetch=2,
            grid=(m // tile,),
            in_specs=[row_spec(D_MODEL), row_spec(LANES), pl.BlockSpec(memory_space=pl.ANY),
                      pl.BlockSpec((1, D_MODEL), lambda i, *_: (0, 0)),
                      pl.BlockSpec((1, D_MODEL), lambda i, *_: (0, 0))],
            out_specs=row_spec(D_MODEL),
            scratch_shapes=[pltpu.VMEM((tile, D_MODEL), F32), pltpu.VMEM((tile, D_MODEL), F32),
                            pltpu.SemaphoreType.DMA],
        ),
        out_shape=jax.ShapeDtypeStruct((m, D_MODEL), F32),
        compiler_params=_cparams("arbitrary"),
        name="moe_combine",
    )(pos1, pos2, x2d, info, ys, g.reshape(1, -1), b.reshape(1, -1))


def _rotary_tables(pos0, t_pad):
    half = D_HEAD // 2
    inv = 1.0 / (10000.0 ** (jnp.arange(half, dtype=F32) / half))
    ang = (pos0 + jnp.arange(t_pad, dtype=jnp.int32)).astype(F32)[:, None] * inv[None, :]
    cos, sin = jnp.cos(ang), jnp.sin(ang)
    cos_t = jnp.tile(jnp.concatenate([cos, cos], axis=-1), (1, N_HEADS))
    sin_t = jnp.tile(jnp.concatenate([-sin, sin], axis=-1), (1, N_HEADS))
    return cos_t, sin_t


def _pad_rows(lo, arr):
    return jnp.zeros((W_BC, W_BC), F32).at[lo:lo + arr.shape[0]].set(arr).astype(BF16)


def _ffn_tile(d_ff):
    best = LANES if d_ff % LANES == 0 else d_ff
    for t in range(LANES, min(d_ff, 1536) + 1, LANES):
        if d_ff % t == 0:
            best = t
    return best


def kernel(x_prompt, x_sample, cache_k, cache_v, page_table, state_ret, state_wkv, state_shift, rel_bias, w_in, w_out, lambda_q1, lambda_k1, lambda_q2, lambda_k2, subln_g, tshift_mu, decay_w0, decay_w2, iclr_a0, iclr_a2, gate_w2, k_k, k_a, r_k, lnx_g, lnx_b, ln1_g, ln1_b, ln2_g, ln2_b, ffn_w1, ffn_w3, ffn_w2, router_w, expert_w1, expert_w3, expert_w2):
    bp, seq, _ = x_prompt.shape
    n_dec, t_new, _ = x_sample.shape
    depth, n_pool, page, _, _ = cache_k.shape
    n_pages = page_table.shape[1]
    past = n_pages * page
    alpha = (2 * depth) ** 0.25

    mp, ms = bp * seq, n_dec * t_new
    tm_p = 256 if mp % 256 == 0 else mp
    tm_s = ms
    blk = 256 if seq % 256 == 0 else seq
    chunk_p = 64 if seq % 64 == 0 else seq
    sub_p = 4 if seq % (4 * chunk_p) == 0 else 1
    sub_w = 8 if seq % (8 * chunk_p) == 0 else sub_p
    chunk_s = 8
    group = next(g for g in (32, 16, 8, 1) if n_pages % g == 0)

    cache_k4 = cache_k.reshape(depth, n_pool, page * N_HEADS, 2 * D_HEAD_A)
    cache_v4 = cache_v.reshape(depth, n_pool, page * N_HEADS, 2 * D_HEAD_A)
    pt_flat = page_table.reshape(-1).astype(jnp.int32)

    rel_bias = rel_bias.astype(F32)
    near_bias = _near_bias_tiles(rel_bias, blk)
    sample_tiles = _sample_bias_tiles(rel_bias, t_new, n_pages, page)
    cos_p, sin_p = _rotary_tables(0, seq)
    cos_s, sin_s = _rotary_tables(past, chunk_s)
    zero_st_p = jnp.zeros((bp, N_HEADS, D_HEAD, D_HEAD), F32)
    zero_shift_p = jnp.zeros((bp, 1, PROJ_C), F32)

    def pad_sample(a):
        w = a.shape[-1]
        a3 = a.reshape(n_dec, t_new, w)
        return jnp.pad(a3, ((0, 0), (0, chunk_s - t_new), (0, 0))).reshape(n_dec * chunk_s, w)

    def unpad_sample(a):
        w = a.shape[-1]
        return a.reshape(n_dec, chunk_s, w)[:, :t_new].reshape(n_dec * t_new, w)

    xp = x_prompt.reshape(mp, D_MODEL)
    xs = x_sample.reshape(ms, D_MODEL)
    kp_all = jnp.zeros((depth, mp * N_HEADS, 2 * D_HEAD_A), F32)
    vp_all = jnp.zeros((depth, mp * N_HEADS, 2 * D_HEAD_A), F32)
    ks_all = jnp.zeros((depth, ms * N_HEADS, 2 * D_HEAD_A), F32)
    vs_all = jnp.zeros((depth, ms * N_HEADS, 2 * D_HEAD_A), F32)
    outs = {k: [] for k in ("rp", "rs", "wp", "ws", "sp", "ss")}
    for l in range(depth):
        w_in_bf = w_in[l].astype(BF16)
        w_out_bf = w_out[l].astype(BF16)
        lam_init = 0.8 - 0.6 * math.exp(-0.3 * l)
        lamv = jnp.stack([lambda_q1[l], lambda_k1[l], lambda_q2[l], lambda_k2[l]]).astype(F32)
        laminit = jnp.full((1, 1), lam_init, F32)
        row = lambda a: a.reshape(1, -1).astype(F32)
        rwkv_params = [row(tshift_mu[l]), row(decay_w0[l]), _pad_rows(0, decay_w2[l]), row(iclr_a0[l]),
                       _pad_rows(64, iclr_a2[l]), _pad_rows(128, gate_w2[l]),
                       row(k_k[l]), row(k_a[l]), row(r_k[l]), row(lnx_g[l]), row(lnx_b[l])]

        qa, kp_all, vp_all, kb, vb, pb, pc = _proj_in(xp, w_in_bf, kp_all, vp_all, l, depth, tm_p)
        oa = _attn_prompt(qa, kb, vb, lamv, laminit, near_bias, subln_g[l], bp, seq, blk)
        ob, ret_p = _retention(pb, cos_p, sin_p, zero_st_p, bp, seq, chunk_p, sub_p, chunk_p)
        oc, wkv_p = _rwkv(pc, zero_shift_p, zero_st_p, rwkv_params, bp, seq, chunk_p, sub_w, chunk_p)
        xp = _out_proj(xp, oa, ob, oc, w_out_bf, ln1_g[l], ln1_b[l], alpha, next(t for t in (1024, 512, tm_p) if mp % t == 0))
        outs["rp"].append(ret_p)
        outs["wp"].append(wkv_p)
        outs["sp"].append(pc.reshape(bp, seq, PROJ_C)[:, -1])

        qa, ks_all, vs_all, kb, vb, pb, pc = _proj_in(xs, w_in_bf, ks_all, vs_all, l, depth, tm_s)
        oa = _attn_sample(qa, kb, vb, cache_k4, cache_v4, pt_flat, l, lamv, laminit, sample_tiles, subln_g[l],
                          n_dec, t_new, n_pages, page, group)
        ob, ret_s = _retention(pad_sample(pb), cos_s, sin_s, state_ret[l], n_dec, chunk_s, chunk_s, 1, t_new)
        oc, wkv_s = _rwkv(pad_sample(pc), state_shift[l].astype(F32).reshape(n_dec, 1, PROJ_C),
                          state_wkv[l], rwkv_params, n_dec, chunk_s, chunk_s, 1, t_new)
        xs = _out_proj(xs, oa, unpad_sample(ob), unpad_sample(oc), w_out_bf, ln1_g[l], ln1_b[l], alpha, tm_s)
        outs["rs"].append(ret_s)
        outs["ws"].append(wkv_s)
        outs["ss"].append(pc.reshape(n_dec, t_new, PROJ_C)[:, -1])

        j = l // 2
        if l % 2 == 0:
            w1, w3, w2 = ffn_w1[j].astype(BF16), ffn_w3[j].astype(BF16), ffn_w2[j].astype(BF16)
            tf = _ffn_tile(w1.shape[1])
            xp = _ffn(xp, w1, w3, w2, ln2_g[l], ln2_b[l], alpha, min(512, mp), tf)
            xs = _ffn(xs, w1, w3, w2, ln2_g[l], ln2_b[l], alpha, tm_s, tf)
        else:
            w1, w3, w2 = expert_w1[j].astype(BF16), expert_w3[j].astype(BF16), expert_w2[j].astype(BF16)
            if mp % MOE_TILE == 0:
                xp = _moe_routed(xp, router_w[j], w1, w3, w2, ln2_g[l], ln2_b[l], alpha)
            else:
                xp = _moe(xp, router_w[j], w1, w3, w2, ln2_g[l], ln2_b[l], alpha, mp)
            xs = _moe(xs, router_w[j], w1, w3, w2, ln2_g[l], ln2_b[l], alpha, tm_s)

    st = lambda key: jnp.stack(outs[key], axis=0)
    kv_p = lambda a: a.reshape(depth, bp, seq, N_HEADS, 2 * D_HEAD_A)
    kv_s = lambda a: a.reshape(depth, n_dec, t_new, N_HEADS, 2 * D_HEAD_A)
    return (xp.reshape(bp, seq, D_MODEL), xs.reshape(n_dec, t_new, D_MODEL),
            kv_p(kp_all), kv_p(vp_all), kv_s(ks_all), kv_s(vs_all),
            st("rp"), st("rs"), st("wp"), st("ws"), st("sp"), st("ss"))
```
